```python
import math, functools
import jax, jax.numpy as jnp
from jax import lax
import numpy as np

D_MODEL = 1024
BATCH = 16
SEQ = 2048
DEPTH = 2

CTX_LEN = 256
GRID_W = 64
D_FF = 2816
N_MOD = 9
EPS = 1e-6
ROPE_BASE = 10000.0
Q_BLOCK = 128

MLA_HEADS = 8
MLA_Q_RANK = 256
MLA_KV_RANK = 128
MLA_NOPE = 64
MLA_ROPE = 32
MLA_V = 64

SWA_HEADS = 8
SWA_KV_HEADS = 2
SWA_HEAD_DIM = 64
SWA_WINDOW = 128
SWA_BLOCK = 128

DIFF_HEADS = 8
DIFF_HEAD_DIM = 64

L0_IN_SIZES = (MLA_Q_RANK, MLA_KV_RANK, MLA_ROPE, SWA_HEADS * SWA_HEAD_DIM, SWA_KV_HEADS * SWA_HEAD_DIM, SWA_KV_HEADS * SWA_HEAD_DIM)
L0_IN_WIDTH = sum(L0_IN_SIZES)
L0_OUT_WIDTH = MLA_HEADS * MLA_V + SWA_HEADS * SWA_HEAD_DIM
L1_IN_WIDTH = 3 * DIFF_HEADS * 2 * DIFF_HEAD_DIM
L1_OUT_WIDTH = DIFF_HEADS * 2 * DIFF_HEAD_DIM

kernel_name = "hybrid_diffusion_mla_swa_diffattn_block"


def rms_norm(x, g):
    xf = x.astype(jnp.float32)
    y = xf * lax.rsqrt(jnp.mean(xf * xf, axis=-1, keepdims=True) + EPS)
    return (y * g.astype(jnp.float32)).astype(x.dtype)


def adaln(x, g, mod, k):
    return rms_norm(x, g) * (1 + mod[..., 3 * k + 1, :]) + mod[..., 3 * k, :]


def swiglu(h, wg, wu, wd):
    return (jax.nn.silu(h @ wg) * (h @ wu)) @ wd


def lambda_init_fn(layer_idx):
    return 0.8 - 0.6 * math.exp(-0.3 * layer_idx)


def axial_rope(n_rows, rot_dim):
    t_row = jnp.repeat(jnp.arange(n_rows, dtype=jnp.float32), GRID_W)
    t_col = jnp.broadcast_to(jnp.arange(GRID_W, dtype=jnp.float32)[None, :], (n_rows, GRID_W)).reshape(-1)
    n_f = rot_dim // 4
    inv = ROPE_BASE ** (-jnp.arange(n_f, dtype=jnp.float32) / n_f)
    ang = jnp.concatenate([t_row[:, None] * inv, t_col[:, None] * inv], axis=-1)
    return jnp.cos(ang), jnp.sin(ang)


def apply_rope(x, cos, sin):
    half = x.shape[-1] // 2
    shape = (cos.shape[0],) + (1,) * (x.ndim - 3) + (half,)
    cos, sin = cos.reshape(shape), sin.reshape(shape)
    x1, x2 = x[..., :half], x[..., half:]
    return jnp.concatenate([x1 * cos - x2 * sin, x1 * sin + x2 * cos], axis=-1).astype(x.dtype)


def sweep_query_blocks(fn, q):
    B, S = q.shape[:2]
    nb = S // Q_BLOCK
    qb = jnp.moveaxis(q.reshape((B, nb, Q_BLOCK) + q.shape[2:]), 1, 0)
    out = lax.map(fn, qb)
    return jnp.moveaxis(out, 0, 1).reshape((B, S) + out.shape[3:])


def softmax_attention_block(qb, k, v, scale):
    s = jnp.einsum('bqhd,bkhd->bhqk', qb, k).astype(jnp.float32) * scale
    p = jax.nn.softmax(s, axis=-1).astype(v.dtype)
    return jnp.einsum('bhqk,bkhd->bqhd', p, v)


def mla_queries(p_qa, qa_g, wqb, q_g, rope):
    B, N = p_qa.shape[:2]
    q = (rms_norm(p_qa, qa_g) @ wqb).reshape(B, N, MLA_HEADS, MLA_NOPE + MLA_ROPE)
    q = rms_norm(q, q_g)
    if rope is not None:
        q = jnp.concatenate([q[..., :MLA_NOPE], apply_rope(q[..., MLA_NOPE:], *rope)], axis=-1)
    return q


def mla_keys_values(p_kva, p_kr, kva_g, wkvb, k_g, rope):
    B, N = p_kva.shape[:2]
    kv = (rms_norm(p_kva, kva_g) @ wkvb).reshape(B, N, MLA_HEADS, MLA_NOPE + MLA_V)
    k_rope = jnp.broadcast_to(p_kr[:, :, None, :], (B, N, MLA_HEADS, MLA_ROPE))
    k = rms_norm(jnp.concatenate([kv[..., :MLA_NOPE], k_rope], axis=-1), k_g)
    if rope is not None:
        k = jnp.concatenate([k[..., :MLA_NOPE], apply_rope(k[..., MLA_NOPE:], *rope)], axis=-1)
    return k, kv[..., MLA_NOPE:]


def window_attention_latent(q, k, v, kc, vc, sink, scale):
    B, S, Hq, d = q.shape
    Hkv = k.shape[2]
    G = Hq // Hkv
    W = SWA_BLOCK
    nb = S // W
    L = kc.shape[1]

    def band(t):
        tb = t.reshape(B, nb, W, Hkv, d)
        tp = jnp.pad(tb, ((0, 0), (1, 1), (0, 0), (0, 0), (0, 0)))
        return jnp.moveaxis(jnp.concatenate([tp[:, :-2], tp[:, 1:-1], tp[:, 2:]], axis=2), 1, 0)

    qb = jnp.moveaxis(q.reshape(B, nb, W, Hkv, G, d), 1, 0)
    q_pos = jnp.arange(nb)[:, None] * W + jnp.arange(W)[None, :]
    k_pos = (jnp.arange(nb)[:, None] - 1) * W + jnp.arange(3 * W)[None, :]
    valid = ((jnp.abs(k_pos[:, None, :] - q_pos[:, :, None]) <= SWA_WINDOW)
             & (k_pos[:, None, :] >= 0) & (k_pos[:, None, :] < S))
    sink_logit = sink.reshape(Hkv, G)[None, :, :, None, None].astype(jnp.float32)

    def one_block(args):
        qblk, kblk, vblk, mask = args
        s_loc = jnp.einsum('bqhgd,bkhd->bhgqk', qblk, kblk).astype(jnp.float32) * scale
        s_loc = jnp.where(mask, s_loc, -jnp.inf)
        s_ctx = jnp.einsum('bqhgd,blhd->bhgql', qblk, kc).astype(jnp.float32) * scale
        s_sink = jnp.broadcast_to(sink_logit, s_ctx.shape[:-1] + (1,))
        p = jax.nn.softmax(jnp.concatenate([s_ctx, s_loc, s_sink], axis=-1), axis=-1).astype(v.dtype)
        return (jnp.einsum('bhgql,blhd->bqhgd', p[..., :L], vc)
                + jnp.einsum('bhgqk,bkhd->bqhgd', p[..., L:L + 3 * W], vblk))

    out = lax.map(one_block, (qb, band(k), band(v), valid))
    return jnp.moveaxis(out, 0, 1).reshape(B, S, Hq, d)


def window_attention_context(qc, kc, vc, sink, scale):
    B, L, Hq, d = qc.shape
    Hkv = kc.shape[2]
    G = Hq // Hkv
    qg = qc.reshape(B, L, Hkv, G, d)
    s = jnp.einsum('blhgd,bmhd->bhglm', qg, kc).astype(jnp.float32) * scale
    s_sink = jnp.broadcast_to(sink.reshape(Hkv, G)[None, :, :, None, None].astype(jnp.float32), s.shape[:-1] + (1,))
    p = jax.nn.softmax(jnp.concatenate([s, s_sink], axis=-1), axis=-1)[..., :-1].astype(vc.dtype)
    return jnp.einsum('bhglm,bmhd->blhgd', p, vc).reshape(B, L, Hq, d)


def mixer_mla_swa(h, hc, need_ctx, rope_mla, rope_swa, w_in, mla_qa_g, mla_wqb, mla_kva_g, mla_wkvb,
                  mla_q_g, mla_k_g, swa_q_g, swa_k_g, swa_sink, w_out):
    B, S = h.shape[:2]
    split_at = [int(i) for i in np.cumsum(L0_IN_SIZES)[:-1]]
    qa, kva, kr, sq, sk, sv = jnp.split(h @ w_in, split_at, axis=-1)
    qa_c, kva_c, kr_c, sq_c, sk_c, sv_c = jnp.split(hc @ w_in, split_at, axis=-1)
    mla_scale = (MLA_NOPE + MLA_ROPE) ** -0.5
    swa_scale = SWA_HEAD_DIM ** -0.5

    q = mla_queries(qa, mla_qa_g, mla_wqb, mla_q_g, rope_mla)
    k, v = mla_keys_values(kva, kr, mla_kva_g, mla_wkvb, mla_k_g, rope_mla)
    k_c, v_c = mla_keys_values(kva_c, kr_c, mla_kva_g, mla_wkvb, mla_k_g, None)
    k_all = jnp.concatenate([k_c, k], axis=1)
    v_all = jnp.concatenate([v_c, v], axis=1)
    a = sweep_query_blocks(lambda qb: softmax_attention_block(qb, k_all, v_all, mla_scale), q)

    def swa_heads(p, n_heads, g, rope):
        t = p.reshape(p.shape[0], p.shape[1], n_heads, SWA_HEAD_DIM)
        t = rms_norm(t, g) if g is not None else t
        return apply_rope(t, *rope) if rope is not None else t

    sq_l = swa_heads(sq, SWA_HEADS, swa_q_g, rope_swa)
    sk_l = swa_heads(sk, SWA_KV_HEADS, swa_k_g, rope_swa)
    sv_l = swa_heads(sv, SWA_KV_HEADS, None, None)
    sk_cc = swa_heads(sk_c, SWA_KV_HEADS, swa_k_g, None)
    sv_cc = swa_heads(sv_c, SWA_KV_HEADS, None, None)
    b = window_attention_latent(sq_l, sk_l, sv_l, sk_cc, sv_cc, swa_sink, swa_scale)

    y = jnp.concatenate([a.reshape(B, S, -1), b.reshape(B, S, -1)], axis=-1) @ w_out
    if not need_ctx:
        return y, None
    L = hc.shape[1]
    a_c = softmax_attention_block(mla_queries(qa_c, mla_qa_g, mla_wqb, mla_q_g, None), k_c, v_c, mla_scale)
    b_c = window_attention_context(swa_heads(sq_c, SWA_HEADS, swa_q_g, None), sk_cc, sv_cc, swa_sink, swa_scale)
    y_c = jnp.concatenate([a_c.reshape(B, L, -1), b_c.reshape(B, L, -1)], axis=-1) @ w_out
    return y, y_c


def mixer_diff(h, hc, need_ctx, rope, w_in, q_g, k_g, lambda_q1, lambda_k1, lambda_q2, lambda_k2,
               subln_g, w_out, lambda_init):
    scale = DIFF_HEAD_DIM ** -0.5

    def split_heads(t):
        B, N = t.shape[:2]
        q, k, v = jnp.split(t @ w_in, 3, axis=-1)
        return (q.reshape(B, N, DIFF_HEADS, 2, DIFF_HEAD_DIM),
                k.reshape(B, N, DIFF_HEADS, 2, DIFF_HEAD_DIM),
                v.reshape(B, N, DIFF_HEADS, 2 * DIFF_HEAD_DIM))

    def qk_prep(t, g, rope_t):
        t = rms_norm(t, g)
        return apply_rope(t, *rope_t) if rope_t is not None else t

    lam = (jnp.exp(jnp.sum(lambda_q1.astype(jnp.float32) * lambda_k1.astype(jnp.float32)))
           - jnp.exp(jnp.sum(lambda_q2.astype(jnp.float32) * lambda_k2.astype(jnp.float32)))
           + lambda_init)

    def diff_block(qb, k, v):
        s = jnp.einsum('bqhmd,bkhmd->bhmqk', qb, k).astype(jnp.float32) * scale
        p = jax.nn.softmax(s, axis=-1)
        a = (p[:, :, 0] - lam * p[:, :, 1]).astype(v.dtype)
        return jnp.einsum('bhqk,bkhd->bqhd', a, v)

    def heads_out(o):
        o = rms_norm(o, subln_g) * (1 - lambda_init)
        return o.reshape(o.shape[0], o.shape[1], -1) @ w_out

    q, k, v = split_heads(h)
    q, k = qk_prep(q, q_g, rope), qk_prep(k, k_g, rope)
    q_c, k_c, v_c = split_heads(hc)
    k_c = qk_prep(k_c, k_g, None)
    k_all = jnp.concatenate([k_c, k], axis=1)
    v_all = jnp.concatenate([v_c, v], axis=1)
    y = heads_out(sweep_query_blocks(lambda qb: diff_block(qb, k_all, v_all), q))
    if not need_ctx:
        return y, None
    y_c = heads_out(diff_block(qk_prep(q_c, q_g, None), k_c, v_c))
    return y, y_c


def setup_inputs(seed: int = 0) -> dict:
    key = jax.random.key(seed)
    keys = iter(jax.random.split(key, 48))
    f32 = jnp.float32

    def nrm(shape, s):
        return jax.random.normal(next(keys), shape, f32) * s

    def lin(shape, s=1.0):
        return nrm(shape, s * shape[-2] ** -0.5)

    def gain(shape):
        return 1.0 + nrm(shape, 0.05)

    def common(p):
        return {
            p + "ada_w": lin((D_MODEL, N_MOD * D_MODEL), 0.5),
            p + "ada_b": nrm((N_MOD * D_MODEL,), 0.02),
            p + "norm_g": gain((3, D_MODEL)),
            p + "ffn_wg": lin((2, D_MODEL, D_FF)),
            p + "ffn_wu": lin((2, D_MODEL, D_FF)),
            p + "ffn_wd": lin((2, D_FF, D_MODEL)),
        }

    inputs = {
        "x": nrm((BATCH, SEQ, D_MODEL), 1.0),
        "c": nrm((BATCH, D_MODEL), 1.0),
        "ctx": nrm((BATCH, CTX_LEN, D_MODEL), 1.0),
        "c_ctx": nrm((D_MODEL,), 1.0),
    }
    inputs.update(common("l0_"))
    inputs.update({
        "l0_w_in": lin((D_MODEL, L0_IN_WIDTH)),
        "l0_mla_qa_g": gain((MLA_Q_RANK,)),
        "l0_mla_wqb": lin((MLA_Q_RANK, MLA_HEADS * (MLA_NOPE + MLA_ROPE))),
        "l0_mla_kva_g": gain((MLA_KV_RANK,)),
        "l0_mla_wkvb": lin((MLA_KV_RANK, MLA_HEADS * (MLA_NOPE + MLA_V))),
        "l0_mla_q_g": gain((MLA_NOPE + MLA_ROPE,)),
        "l0_mla_k_g": gain((MLA_NOPE + MLA_ROPE,)),
        "l0_swa_q_g": gain((SWA_HEAD_DIM,)),
        "l0_swa_k_g": gain((SWA_HEAD_DIM,)),
        "l0_swa_sink": nrm((SWA_HEADS,), 1.0),
        "l0_w_out": lin((L0_OUT_WIDTH, D_MODEL)),
    })
    inputs.update(common("l1_"))
    inputs.update({
        "l1_w_in": lin((D_MODEL, L1_IN_WIDTH)),
        "l1_q_g": gain((DIFF_HEAD_DIM,)),
        "l1_k_g": gain((DIFF_HEAD_DIM,)),
        "l1_lambda_q1": nrm((DIFF_HEAD_DIM,), 0.1),
        "l1_lambda_k1": nrm((DIFF_HEAD_DIM,), 0.1),
        "l1_lambda_q2": nrm((DIFF_HEAD_DIM,), 0.1),
        "l1_lambda_k2": nrm((DIFF_HEAD_DIM,), 0.1),
        "l1_subln_g": gain((2 * DIFF_HEAD_DIM,)),
        "l1_w_out": lin((L1_OUT_WIDTH, D_MODEL)),
    })
    return inputs


def reference(x, c, ctx, c_ctx,
              l0_ada_w, l0_ada_b, l0_norm_g, l0_ffn_wg, l0_ffn_wu, l0_ffn_wd,
              l0_w_in, l0_mla_qa_g, l0_mla_wqb, l0_mla_kva_g, l0_mla_wkvb, l0_mla_q_g, l0_mla_k_g,
              l0_swa_q_g, l0_swa_k_g, l0_swa_sink, l0_w_out,
              l1_ada_w, l1_ada_b, l1_norm_g, l1_ffn_wg, l1_ffn_wu, l1_ffn_wd,
              l1_w_in, l1_q_g, l1_k_g, l1_lambda_q1, l1_lambda_k1, l1_lambda_q2, l1_lambda_k2,
              l1_subln_g, l1_w_out):
    n_rows = x.shape[1] // GRID_W
    rope_mla = axial_rope(n_rows, MLA_ROPE)
    rope_swa = axial_rope(n_rows, SWA_HEAD_DIM)
    rope_diff = axial_rope(n_rows, DIFF_HEAD_DIM)

    layers = [
        ((l0_ada_w, l0_ada_b, l0_norm_g, l0_ffn_wg, l0_ffn_wu, l0_ffn_wd),
         functools.partial(mixer_mla_swa, rope_mla=rope_mla, rope_swa=rope_swa, w_in=l0_w_in,
                           mla_qa_g=l0_mla_qa_g, mla_wqb=l0_mla_wqb, mla_kva_g=l0_mla_kva_g,
                           mla_wkvb=l0_mla_wkvb, mla_q_g=l0_mla_q_g, mla_k_g=l0_mla_k_g,
                           swa_q_g=l0_swa_q_g, swa_k_g=l0_swa_k_g, swa_sink=l0_swa_sink, w_out=l0_w_out)),
        ((l1_ada_w, l1_ada_b, l1_norm_g, l1_ffn_wg, l1_ffn_wu, l1_ffn_wd),
         functools.partial(mixer_diff, rope=rope_diff, w_in=l1_w_in, q_g=l1_q_g, k_g=l1_k_g,
                           lambda_q1=l1_lambda_q1, lambda_k1=l1_lambda_k1,
                           lambda_q2=l1_lambda_q2, lambda_k2=l1_lambda_k2,
                           subln_g=l1_subln_g, w_out=l1_w_out, lambda_init=lambda_init_fn(1))),
    ]

    h, hc = x, ctx
    for layer in range(DEPTH):
        (ada_w, ada_b, norm_g, wg, wu, wd), mixer = layers[layer]
        need_ctx = layer < DEPTH - 1
        mod = (jax.nn.silu(c) @ ada_w + ada_b).reshape(c.shape[0], 1, N_MOD, D_MODEL)
        mod_c = (jax.nn.silu(c_ctx) @ ada_w + ada_b).reshape(N_MOD, D_MODEL)
        h = h + 0.5 * mod[..., 2, :] * swiglu(adaln(h, norm_g[0], mod, 0), wg[0], wu[0], wd[0])
        hc = hc + 0.5 * mod_c[..., 2, :] * swiglu(adaln(hc, norm_g[0], mod_c, 0), wg[0], wu[0], wd[0])
        y, y_c = mixer(adaln(h, norm_g[1], mod, 1), adaln(hc, norm_g[1], mod_c, 1), need_ctx)
        h = h + mod[..., 5, :] * y
        h = h + 0.5 * mod[..., 8, :] * swiglu(adaln(h, norm_g[2], mod, 2), wg[1], wu[1], wd[1])
        if need_ctx:
            hc = hc + mod_c[..., 5, :] * y_c
            hc = hc + 0.5 * mod_c[..., 8, :] * swiglu(adaln(hc, norm_g[2], mod_c, 2), wg[1], wu[1], wd[1])
    return h
```

```python
import functools
import math

import numpy as np
import jax
import jax.numpy as jnp
from jax import lax
from jax.experimental import pallas as pl
from jax.experimental.pallas import tpu as pltpu

F32 = jnp.float32
BF16 = jnp.bfloat16

D_MODEL = 1024
D_FF = 2816
N_MOD = 9
EPS = 1e-6
ROPE_BASE = 10000.0
GRID_W = 64
LOG2E = 1.4426950408889634

MLA_HEADS = 8
MLA_Q_RANK = 256
MLA_KV_RANK = 128
MLA_NOPE = 64
MLA_ROPE = 32
MLA_V = 64
SWA_HEADS = 8
SWA_KV_HEADS = 2
SWA_GROUP = SWA_HEADS // SWA_KV_HEADS
SWA_HEAD_DIM = 64
SWA_WINDOW = 128
DIFF_HEADS = 8
DIFF_HEAD_DIM = 64
LAMBDA_INIT_L1 = 0.8 - 0.6 * math.exp(-0.3 * 1)

LANES = 128
HALF = LANES // 2
ATT_TQ = 256
SWA_BLK = 128
TOK_TILE = 512
OUT_TILE = 256
VMEM_LIMIT = 56 * 1024 * 1024
MASKED = -1e30

NT_DIMS = (((1,), (1,)), ((), ()))
TN_DIMS = (((0,), (0,)), ((), ()))


def _mla_lane_map():
    m = np.full(LANES, -1, np.int64)
    m[0:16] = 64 + np.arange(16)
    m[16:48] = np.arange(32)
    m[64:80] = 80 + np.arange(16)
    m[80:112] = 32 + np.arange(32)
    return m


def _swa_lane_map():
    m = np.full(LANES, -1, np.int64)
    m[0:32] = np.arange(32)
    m[64:96] = 32 + np.arange(32)
    return m


def _diff_lane_map():
    m = np.zeros(LANES, np.int64)
    m[0:32] = np.arange(32)
    m[32:64] = 64 + np.arange(32)
    m[64:96] = 32 + np.arange(32)
    m[96:128] = 96 + np.arange(32)
    return m


def _take_cols(w, idx):
    idx = np.asarray(idx)
    out = jnp.take(w, jnp.asarray(np.where(idx < 0, 0, idx), jnp.int32), axis=-1)
    return jnp.where(jnp.asarray(idx >= 0), out, jnp.zeros((), w.dtype))


def _per_head(lane_map, n_heads, stride, base=0):
    cols = [np.where(lane_map >= 0, base + h * stride + lane_map, -1) for h in range(n_heads)]
    return np.concatenate(cols)


def _adaln(x, g, shift, scale):
    ms = jnp.mean(x * x, axis=-1, keepdims=True)
    return x * lax.rsqrt(ms + EPS) * g * (1.0 + scale) + shift


def _rms(t, g, inv_dim):
    ms = jnp.sum(t * t, axis=-1, keepdims=True) * inv_dim
    return t * lax.rsqrt(ms + EPS) * g


def _rope(t, cos, sin):
    return t * cos + pltpu.roll(t, HALF, 1) * sin


def _dot(a, b):
    return jnp.dot(a, b, preferred_element_type=F32)


def _dot_nt(a, b):
    return lax.dot_general(a, b, NT_DIMS, preferred_element_type=F32)


def _dot_tn(a, b):
    return lax.dot_general(a, b, TN_DIMS, preferred_element_type=F32)


def _params(n_axes):
    return pltpu.CompilerParams(dimension_semantics=("arbitrary",) * n_axes,
                                vmem_limit_bytes=VMEM_LIMIT)


def _const_spec(shape):
    nd = len(shape)
    return pl.BlockSpec(shape, lambda *_: (0,) * nd, pipeline_mode=pl.Buffered(1))


def _ada_kernel(c_ref, w_ref, b_ref, o_ref):
    c = c_ref[...]
    a = (c * jax.nn.sigmoid(c)).astype(BF16)
    o_ref[...] = _dot(a, w_ref[...].astype(BF16)) + b_ref[...]


def _ada(cc, w, b):
    rows = cc.shape[0]
    n = w.shape[1]
    tn = 1152
    return pl.pallas_call(
        _ada_kernel,
        grid=(n // tn,),
        in_specs=[pl.BlockSpec((rows, D_MODEL), lambda j: (0, 0)),
                  pl.BlockSpec((D_MODEL, tn), lambda j: (0, j)),
                  pl.BlockSpec((1, tn), lambda j: (0, j))],
        out_specs=pl.BlockSpec((rows, tn), lambda j: (0, j)),
        out_shape=jax.ShapeDtypeStruct((rows, n), F32),
        compiler_params=_params(1),
        name="ada_mod",
    )(cc, w, b.reshape(1, n))


def _rope_table_kernel(inv_ref, userow_ref, sign_ref, cos_ref, sin_ref):
    n = cos_ref.shape[0]
    t = lax.broadcasted_iota(jnp.int32, (n, LANES), 0)
    row = lax.shift_right_logical(t, GRID_W.bit_length() - 1).astype(F32)
    col = (t & (GRID_W - 1)).astype(F32)
    pos = jnp.where(userow_ref[...] > 0.5, row, col)
    ang = pos * inv_ref[...]
    sign = sign_ref[...]
    active = sign != 0.0
    cos_ref[...] = jnp.where(active, jnp.cos(ang), 1.0)
    sin_ref[...] = jnp.where(active, sign * jnp.sin(ang), 0.0)


def _rope_tables(seq, rot_dim, x1_lanes, x2_lanes):
    n_f = rot_dim // 4
    half = rot_dim // 2
    inv_f = (ROPE_BASE ** (-np.arange(n_f, dtype=np.float64) / n_f)).astype(np.float32)
    inv_half = np.concatenate([inv_f, inv_f])
    use_row_half = np.concatenate([np.ones(n_f), np.zeros(n_f)])
    inv = np.zeros((1, LANES), np.float32)
    use_row = np.zeros((1, LANES), np.float32)
    sign = np.zeros((1, LANES), np.float32)
    for lanes, sgn in ((x1_lanes, -1.0), (x2_lanes, 1.0)):
        for start, n in lanes:
            assert n == half
            inv[0, start:start + n] = inv_half
            use_row[0, start:start + n] = use_row_half
            sign[0, start:start + n] = sgn
    vec = pl.BlockSpec((1, LANES), lambda: (0, 0))
    tab = pl.BlockSpec((seq, LANES), lambda: (0, 0))
    return pl.pallas_call(
        _rope_table_kernel,
        in_specs=[vec, vec, vec],
        out_specs=[tab, tab],
        out_shape=[jax.ShapeDtypeStruct((seq, LANES), F32)] * 2,
        name="rope_table",
    )(jnp.asarray(inv), jnp.asarray(use_row), jnp.asarray(sign))


def _ffn_kernel(x_ref, mod_ref, g_ref, wg_ref, wu_ref, wd_ref, o_ref, *, k):
    x = x_ref[...]
    shift = mod_ref[3 * k:3 * k + 1, :]
    scale = mod_ref[3 * k + 1:3 * k + 2, :]
    gate = mod_ref[3 * k + 2:3 * k + 3, :]
    hn = _adaln(x, g_ref[...], shift, scale).astype(BF16)
    g = _dot(hn, wg_ref[...])
    u = _dot(hn, wu_ref[...])
    a = (g * jax.nn.sigmoid(g) * u).astype(BF16)
    o_ref[...] = x + (0.5 * gate) * _dot(a, wd_ref[...])


def _mod_spec(per_batch):
    if per_batch:
        return pl.BlockSpec((None, N_MOD, D_MODEL), lambda b, t: (b, 0, 0))
    return pl.BlockSpec((None, N_MOD, D_MODEL), lambda b, t: (0, 0, 0))


def _tok_spec(tm):
    return pl.BlockSpec((None, tm, D_MODEL), lambda b, t: (b, t, 0))


def _ffn(x, mod, per_batch, g, wg, wu, wd, k):
    bsz, n, _ = x.shape
    tm = min(TOK_TILE, n)
    return pl.pallas_call(
        functools.partial(_ffn_kernel, k=k),
        grid=(bsz, n // tm),
        in_specs=[_tok_spec(tm), _mod_spec(per_batch), _const_spec((1, D_MODEL)),
                  _const_spec((D_MODEL, D_FF)), _const_spec((D_MODEL, D_FF)),
                  _const_spec((D_FF, D_MODEL))],
        out_specs=_tok_spec(tm),
        out_shape=jax.ShapeDtypeStruct(x.shape, F32),
        compiler_params=_params(2),
        name="ffn",
    )(x, mod, g, wg, wu, wd)


L0_QA = (0, 256)
L0_KVA = (256, 384)
L0_KR = (384, 512)
L0_SQ = (512, 1536)
L0_SK = (1536, 1792)
L0_WIDTH = 1792


def _l0_proj_kernel(*refs, rope):
    (h_ref, mod_ref, g_ref, w0_ref, wsv_ref, qag_ref, wqb_ref, kvag_ref, wkn_ref, wvt_ref,
     mqg_ref, mkg_ref, sqg_ref, skg_ref) = refs[:14]
    if rope:
        cm_ref, sm_ref, cs_ref, ss_ref = refs[14:18]
        outs = refs[18:]
        cm, sm, cs, ss = cm_ref[...], sm_ref[...], cs_ref[...], ss_ref[...]
    else:
        outs = refs[14:]
    qm_ref, km_ref, vmt_ref, qs_ref, ks_ref, vst_ref = outs

    hn = _adaln(h_ref[...], g_ref[...], mod_ref[3:4, :], mod_ref[4:5, :]).astype(BF16)
    p = _dot(hn, w0_ref[...])

    mla_inv = 1.0 / (MLA_NOPE + MLA_ROPE)
    swa_inv = 1.0 / SWA_HEAD_DIM
    mqg = mqg_ref[...] * (mla_inv ** 0.5 * LOG2E)
    sqg = sqg_ref[...] * (swa_inv ** 0.5 * LOG2E)
    mkg = mkg_ref[...]
    skg = skg_ref[...]

    qa = _rms(p[:, L0_QA[0]:L0_QA[1]], qag_ref[...], 1.0 / MLA_Q_RANK).astype(BF16)
    q = _dot(qa, wqb_ref[...])
    for h in range(MLA_HEADS):
        t = _rms(q[:, h * LANES:(h + 1) * LANES], mqg, mla_inv)
        if rope:
            t = _rope(t, cm, sm)
        qm_ref[h] = t.astype(BF16)

    kva = _rms(p[:, L0_KVA[0]:L0_KVA[1]], kvag_ref[...], 1.0 / MLA_KV_RANK).astype(BF16)
    kn = _dot(kva, wkn_ref[...])
    kr = p[:, L0_KR[0]:L0_KR[1]]
    for h in range(MLA_HEADS):
        t = _rms(kn[:, h * LANES:(h + 1) * LANES] + kr, mkg, mla_inv)
        if rope:
            t = _rope(t, cm, sm)
        km_ref[h] = t.astype(BF16)
    vt = _dot_nt(wvt_ref[...], kva)
    for h in range(MLA_HEADS):
        vmt_ref[h] = vt[h * MLA_V:(h + 1) * MLA_V, :].astype(BF16)

    for h in range(SWA_HEADS):
        t = _rms(p[:, L0_SQ[0] + h * LANES:L0_SQ[0] + (h + 1) * LANES], sqg, swa_inv)
        if rope:
            t = _rope(t, cs, ss)
        qs_ref[h] = t.astype(BF16)
    for j in range(SWA_KV_HEADS):
        t = _rms(p[:, L0_SK[0] + j * LANES:L0_SK[0] + (j + 1) * LANES], skg, swa_inv)
        if rope:
            t = _rope(t, cs, ss)
        ks_ref[j] = t.astype(BF16)
    svt = _dot_nt(wsv_ref[...], hn)
    n_blk = svt.shape[1] // SWA_BLK
    for j in range(SWA_KV_HEADS):
        for i in range(n_blk):
            vst_ref[j, i] = svt[j * SWA_HEAD_DIM:(j + 1) * SWA_HEAD_DIM,
                                i * SWA_BLK:(i + 1) * SWA_BLK].astype(BF16)


def _l0_proj(h, mod, per_batch, g, w, tables):
    bsz, n, _ = h.shape
    tm = min(TOK_TILE, n)
    rope = tables is not None
    weights = [g, w["w0"], w["wsv"], w["qa_g"], w["wqb"], w["kva_g"], w["wkn"], w["wvt"],
               w["mq_g"], w["mk_g"], w["sq_g"], w["sk_g"]]
    in_specs = [_tok_spec(tm), _mod_spec(per_batch)] + [_const_spec(a.shape) for a in weights]
    args = [h, mod] + weights
    if rope:
        in_specs += [pl.BlockSpec((tm, LANES), lambda b, t: (t, 0))] * 4
        args += list(tables)
    head = lambda nh: pl.BlockSpec((None, nh, tm, LANES), lambda b, t: (b, 0, t, 0))
    out_specs = [head(MLA_HEADS), head(MLA_HEADS),
                 pl.BlockSpec((None, MLA_HEADS, MLA_V, tm), lambda b, t: (b, 0, 0, t)),
                 head(SWA_HEADS), head(SWA_KV_HEADS),
                 pl.BlockSpec((None, SWA_KV_HEADS, tm // SWA_BLK, SWA_HEAD_DIM, SWA_BLK),
                              lambda b, t: (b, 0, t, 0, 0))]
    out_shape = [jax.ShapeDtypeStruct((bsz, MLA_HEADS, n, LANES), BF16),
                 jax.ShapeDtypeStruct((bsz, MLA_HEADS, n, LANES), BF16),
                 jax.ShapeDtypeStruct((bsz, MLA_HEADS, MLA_V, n), BF16),
                 jax.ShapeDtypeStruct((bsz, SWA_HEADS, n, LANES), BF16),
                 jax.ShapeDtypeStruct((bsz, SWA_KV_HEADS, n, LANES), BF16),
                 jax.ShapeDtypeStruct((bsz, SWA_KV_HEADS, n // SWA_BLK, SWA_HEAD_DIM, SWA_BLK), BF16)]
    return pl.pallas_call(
        functools.partial(_l0_proj_kernel, rope=rope),
        grid=(bsz, n // tm),
        in_specs=in_specs, out_specs=out_specs, out_shape=out_shape,
        compiler_params=_params(2),
        name="l0_proj",
    )(*args)


def _mla_attn_kernel(*refs, has_latent):
    if has_latent:
        q_ref, kc_ref, vct_ref, kl_ref, vlt_ref, o_ref = refs
    else:
        q_ref, kc_ref, vct_ref, o_ref = refs
    n_tiles = o_ref.shape[0]

    def tile(i, carry):
        q = q_ref[pl.ds(pl.multiple_of(i * ATT_TQ, ATT_TQ), ATT_TQ), :]
        sc = _dot_nt(kc_ref[...], q)
        m = jnp.max(sc, axis=0, keepdims=True)
        if has_latent:
            sl = _dot_nt(kl_ref[...], q)
            m = jnp.maximum(m, jnp.max(sl, axis=0, keepdims=True))
        pc = jnp.exp2(sc - m)
        l = jnp.sum(pc, axis=0, keepdims=True)
        o = _dot(vct_ref[...], pc.astype(BF16))
        if has_latent:
            pl_ = jnp.exp2(sl - m)
            l = l + jnp.sum(pl_, axis=0, keepdims=True)
            o = o + _dot(vlt_ref[...], pl_.astype(BF16))
        o_ref[i] = (o * (1.0 / l)).astype(BF16)
        return carry

    lax.fori_loop(0, n_tiles, tile, 0)


def _mla_attn(q, kc, vct, kl=None, vlt=None):
    bsz, nh, nq, _ = q.shape
    has_latent = kl is not None
    bh = lambda shape: pl.BlockSpec((None, None) + shape, lambda b, h: (b, h, 0, 0))
    in_specs = [bh((nq, LANES)), bh(kc.shape[2:]), bh(vct.shape[2:])]
    args = [q, kc, vct]
    if has_latent:
        in_specs += [bh(kl.shape[2:]), bh(vlt.shape[2:])]
        args += [kl, vlt]
    n_tiles = nq // ATT_TQ
    return pl.pallas_call(
        functools.partial(_mla_attn_kernel, has_latent=has_latent),
        grid=(bsz, nh),
        in_specs=in_specs,
        out_specs=pl.BlockSpec((None, n_tiles, None, MLA_V, ATT_TQ), lambda b, h: (b, 0, h, 0, 0)),
        out_shape=jax.ShapeDtypeStruct((bsz, n_tiles, nh, MLA_V, ATT_TQ), BF16),
        compiler_params=_params(2),
        name="mla_attn",
    )(*args)


def _swa_attn_kernel(*refs, has_band):
    if has_band:
        sink_ref, q_ref, kc_ref, vct_ref, kl_ref, vlt_ref, o_ref = refs
    else:
        sink_ref, q_ref, kc_ref, vct_ref, o_ref = refs
    n_blk = o_ref.shape[0]
    j = pl.program_id(1)
    nq = SWA_GROUP * SWA_BLK
    sink = jnp.concatenate(
        [jnp.full((1, SWA_BLK), sink_ref[j * SWA_GROUP + g] * LOG2E, F32) for g in range(SWA_GROUP)],
        axis=1)
    band = 3 * SWA_BLK

    def block(n, carry):
        q0 = pl.multiple_of(n * SWA_BLK, SWA_BLK)
        q = jnp.concatenate([q_ref[g, pl.ds(q0, SWA_BLK), :] for g in range(SWA_GROUP)], axis=0)
        sc = _dot_nt(kc_ref[...], q)
        m = jnp.maximum(jnp.max(sc, axis=0, keepdims=True), sink)
        if has_band:
            b0 = jnp.clip(n - 1, 0, n_blk - 3)
            k0 = pl.multiple_of(b0 * SWA_BLK, SWA_BLK)
            sb = _dot_nt(kl_ref[pl.ds(k0, band), :], q)
            kpos = k0 + lax.broadcasted_iota(jnp.int32, (band, nq), 0)
            qpos = q0 + (lax.broadcasted_iota(jnp.int32, (band, nq), 1) & (SWA_BLK - 1))
            sb = jnp.where(jnp.abs(kpos - qpos) <= SWA_WINDOW, sb, MASKED)
            m = jnp.maximum(m, jnp.max(sb, axis=0, keepdims=True))
        pc = jnp.exp2(sc - m)
        l = jnp.sum(pc, axis=0, keepdims=True) + jnp.exp2(sink - m)
        o = _dot(vct_ref[...], pc.astype(BF16))
        if has_band:
            pb = jnp.exp2(sb - m)
            l = l + jnp.sum(pb, axis=0, keepdims=True)
            vb = jnp.concatenate([vlt_ref[b0], vlt_ref[b0 + 1], vlt_ref[b0 + 2]], axis=1)
            o = o + _dot(vb, pb.astype(BF16))
        o = o * (1.0 / l)
        for g in range(SWA_GROUP):
            o_ref[n, g] = o[:, g * SWA_BLK:(g + 1) * SWA_BLK].astype(BF16)
        return carry

    lax.fori_loop(0, n_blk, block, 0)


def _swa_attn(sink, q, kc, vct, kl=None, vlt=None):
    bsz, nh, nq, _ = q.shape
    has_band = kl is not None
    n_blk = nq // SWA_BLK
    in_specs = [pl.BlockSpec(memory_space=pltpu.SMEM),
                pl.BlockSpec((None, SWA_GROUP, nq, LANES), lambda b, j: (b, j, 0, 0)),
                pl.BlockSpec((None, None) + kc.shape[2:], lambda b, j: (b, j, 0, 0)),
                pl.BlockSpec((None, None) + vct.shape[2:], lambda b, j: (b, j, 0, 0))]
    args = [sink, q, kc, vct]
    if has_band:
        in_specs += [pl.BlockSpec((None, None) + kl.shape[2:], lambda b, j: (b, j, 0, 0)),
                     pl.BlockSpec((None, None) + vlt.shape[2:], lambda b, j: (b, j, 0, 0, 0))]
        args += [kl, vlt]
    return pl.pallas_call(
        functools.partial(_swa_attn_kernel, has_band=has_band),
        grid=(bsz, SWA_KV_HEADS),
        in_specs=in_specs,
        out_specs=pl.BlockSpec((None, n_blk, SWA_GROUP, SWA_HEAD_DIM, SWA_BLK),
                               lambda b, j: (b, 0, j, 0, 0)),
        out_shape=jax.ShapeDtypeStruct((bsz, n_blk, nh, SWA_HEAD_DIM, SWA_BLK), BF16),
        compiler_params=_params(2),
        name="swa_attn",
    )(*args)


def _l0_out_kernel(h_ref, mod_ref, at_ref, bt_ref, wa_ref, wb_ref, o_ref):
    a = at_ref[...].reshape(MLA_HEADS * MLA_V, OUT_TILE)
    y = _dot_tn(a, wa_ref[...])
    yb = []
    for i in range(OUT_TILE // SWA_BLK):
        bt = bt_ref[i].reshape(SWA_HEADS * SWA_HEAD_DIM, SWA_BLK)
        yb.append(_dot_tn(bt, wb_ref[...]))
    y = y + jnp.concatenate(yb, axis=0)
    o_ref[...] = h_ref[...] + mod_ref[5:6, :] * y


def _l0_out(h, mod, per_batch, at, bt, wa, wb):
    bsz, n, _ = h.shape
    tm = OUT_TILE
    return pl.pallas_call(
        _l0_out_kernel,
        grid=(bsz, n // tm),
        in_specs=[_tok_spec(tm), _mod_spec(per_batch),
                  pl.BlockSpec((None, None, MLA_HEADS, MLA_V, ATT_TQ), lambda b, t: (b, t, 0, 0, 0)),
                  pl.BlockSpec((None, tm // SWA_BLK, SWA_HEADS, SWA_HEAD_DIM, SWA_BLK),
                               lambda b, t: (b, t, 0, 0, 0)),
                  _const_spec(wa.shape), _const_spec(wb.shape)],
        out_specs=_tok_spec(tm),
        out_shape=jax.ShapeDtypeStruct(h.shape, F32),
        compiler_params=_params(2),
        name="l0_out",
    )(h, mod, at, bt, wa, wb)


def _diff_norm(t, g, mask0):
    sq = t * t
    s0 = jnp.sum(jnp.where(mask0, sq, 0.0), axis=-1, keepdims=True)
    s1 = jnp.sum(jnp.where(mask0, 0.0, sq), axis=-1, keepdims=True)
    inv = 1.0 / DIFF_HEAD_DIM
    r = jnp.where(mask0, lax.rsqrt(s0 * inv + EPS), lax.rsqrt(s1 * inv + EPS))
    return t * r * g


def _l1_proj_kernel(*refs, latent):
    if latent:
        (h_ref, mod_ref, g_ref, wq_ref, wk_ref, wvt_ref, qg_ref, kg_ref, cd_ref, sd_ref,
         q_ref, k_ref, vt_ref) = refs
        cd, sd = cd_ref[...], sd_ref[...]
    else:
        h_ref, mod_ref, g_ref, wk_ref, wvt_ref, kg_ref, k_ref, vt_ref = refs
    hn = _adaln(h_ref[...], g_ref[...], mod_ref[3:4, :], mod_ref[4:5, :]).astype(BF16)
    tm = hn.shape[0]
    lane = lax.broadcasted_iota(jnp.int32, (tm, LANES), 1)
    mask0 = (lane & (HALF - 1)) < (HALF // 2)
    if latent:
        qg = qg_ref[...] * (DIFF_HEAD_DIM ** -0.5 * LOG2E)
        q = _dot(hn, wq_ref[...])
        for h in range(DIFF_HEADS):
            t = _rope(_diff_norm(q[:, h * LANES:(h + 1) * LANES], qg, mask0), cd, sd)
            q_ref[h] = t.astype(BF16)
    kg = kg_ref[...]
    k = _dot(hn, wk_ref[...])
    for h in range(DIFF_HEADS):
        t = _diff_norm(k[:, h * LANES:(h + 1) * LANES], kg, mask0)
        if latent:
            t = _rope(t, cd, sd)
        k_ref[h, 0] = jnp.where(mask0, t, 0.0).astype(BF16)
        k_ref[h, 1] = jnp.where(mask0, 0.0, t).astype(BF16)
    vt = _dot_nt(wvt_ref[...], hn)
    for h in range(DIFF_HEADS):
        vt_ref[h] = vt[h * LANES:(h + 1) * LANES, :].astype(BF16)


def _l1_proj(h, mod, per_batch, g, w, tables):
    bsz, n, _ = h.shape
    tm = min(TOK_TILE, n)
    latent = tables is not None
    k_spec = pl.BlockSpec((None, DIFF_HEADS, 2, tm, LANES), lambda b, t: (b, 0, 0, t, 0))
    vt_spec = pl.BlockSpec((None, DIFF_HEADS, LANES, tm), lambda b, t: (b, 0, 0, t))
    k_shape = jax.ShapeDtypeStruct((bsz, DIFF_HEADS, 2, n, LANES), BF16)
    vt_shape = jax.ShapeDtypeStruct((bsz, DIFF_HEADS, LANES, n), BF16)
    if latent:
        weights = [g, w["wq"], w["wk"], w["wvt"], w["q_g"], w["k_g"]]
        in_specs = ([_tok_spec(tm), _mod_spec(per_batch)] + [_const_spec(a.shape) for a in weights]
                    + [pl.BlockSpec((tm, LANES), lambda b, t: (t, 0))] * 2)
        args = [h, mod] + weights + list(tables)
        out_specs = [pl.BlockSpec((None, DIFF_HEADS, tm, LANES), lambda b, t: (b, 0, t, 0)),
                     k_spec, vt_spec]
        out_shape = [jax.ShapeDtypeStruct((bsz, DIFF_HEADS, n, LANES), BF16), k_shape, vt_shape]
    else:
        weights = [g, w["wk"], w["wvt"], w["k_g"]]
        in_specs = [_tok_spec(tm), _mod_spec(per_batch)] + [_const_spec(a.shape) for a in weights]
        args = [h, mod] + weights
        out_specs = [k_spec, vt_spec]
        out_shape = [k_shape, vt_shape]
    return pl.pallas_call(
        functools.partial(_l1_proj_kernel, latent=latent),
        grid=(bsz, n // tm),
        in_specs=in_specs, out_specs=out_specs, out_shape=out_shape,
        compiler_params=_params(2),
        name="l1_proj",
    )(*args)


def _diff_attn_kernel(lam_ref, subg_ref, q_ref, kc_ref, vct_ref, kl_ref, vlt_ref, o_ref):
    n_tiles = o_ref.shape[0]
    lv = lam_ref[...]
    lam = (jnp.exp(jnp.sum(lv[0:1] * lv[1:2], axis=-1, keepdims=True))
           - jnp.exp(jnp.sum(lv[2:3] * lv[3:4], axis=-1, keepdims=True)) + LAMBDA_INIT_L1)
    subg = subg_ref[...] * (1.0 - LAMBDA_INIT_L1)

    def tile(i, carry):
        q = q_ref[pl.ds(pl.multiple_of(i * ATT_TQ, ATT_TQ), ATT_TQ), :]
        es = []
        for s in range(2):
            sc = _dot_nt(kc_ref[s], q)
            sl = _dot_nt(kl_ref[s], q)
            m = jnp.maximum(jnp.max(sc, axis=0, keepdims=True), jnp.max(sl, axis=0, keepdims=True))
            ec = jnp.exp2(sc - m)
            el = jnp.exp2(sl - m)
            l = jnp.sum(ec, axis=0, keepdims=True) + jnp.sum(el, axis=0, keepdims=True)
            es.append((ec, el, 1.0 / l))
        c1 = es[0][2]
        c2 = lam * es[1][2]
        ac = (es[0][0] * c1 - es[1][0] * c2).astype(BF16)
        al = (es[0][1] * c1 - es[1][1] * c2).astype(BF16)
        o = _dot(vct_ref[...], ac) + _dot(vlt_ref[...], al)
        ms = jnp.mean(o * o, axis=0, keepdims=True)
        o_ref[i] = (o * lax.rsqrt(ms + EPS) * subg).astype(BF16)
        return carry

    lax.fori_loop(0, n_tiles, tile, 0)


def _diff_attn(lamv, subg, q, kc, vct, kl, vlt):
    bsz, nh, nq, _ = q.shape
    n_tiles = nq // ATT_TQ
    bh = lambda shape: pl.BlockSpec((None, None) + tuple(shape), lambda b, h: (b, h) + (0,) * len(shape))
    return pl.pallas_call(
        _diff_attn_kernel,
        grid=(bsz, nh),
        in_specs=[_const_spec(lamv.shape), _const_spec(subg.shape),
                  bh(q.shape[2:]), bh(kc.shape[2:]), bh(vct.shape[2:]), bh(kl.shape[2:]), bh(vlt.shape[2:])],
        out_specs=pl.BlockSpec((None, n_tiles, None, LANES, ATT_TQ), lambda b, h: (b, 0, h, 0, 0)),
        out_shape=jax.ShapeDtypeStruct((bsz, n_tiles, nh, LANES, ATT_TQ), BF16),
        compiler_params=_params(2),
        name="diff_attn",
    )(lamv, subg, q, kc, vct, kl, vlt)


def _l1_out_kernel(h_ref, mod_ref, ot_ref, w_ref, o_ref):
    a = ot_ref[...].reshape(DIFF_HEADS * LANES, OUT_TILE)
    o_ref[...] = h_ref[...] + mod_ref[5:6, :] * _dot_tn(a, w_ref[...])


def _l1_out(h, mod, ot, w):
    bsz, n, _ = h.shape
    tm = OUT_TILE
    return pl.pallas_call(
        _l1_out_kernel,
        grid=(bsz, n // tm),
        in_specs=[_tok_spec(tm), _mod_spec(True),
                  pl.BlockSpec((None, None, DIFF_HEADS, LANES, ATT_TQ), lambda b, t: (b, t, 0, 0, 0)),
                  _const_spec(w.shape)],
        out_specs=_tok_spec(tm),
        out_shape=jax.ShapeDtypeStruct(h.shape, F32),
        compiler_params=_params(2),
        name="l1_out",
    )(h, mod, ot, w)


def _prep_l0(w_in, qa_g, wqb, kva_g, wkvb, q_g, k_g, sq_g, sk_g, w_out):
    mla = _mla_lane_map()
    swa = _swa_lane_map()
    kr_cols = np.where(mla >= MLA_NOPE, 384 + mla - MLA_NOPE, -1)
    w0 = jnp.concatenate([
        w_in[:, 0:384],
        _take_cols(w_in, kr_cols),
        _take_cols(w_in, _per_head(swa, SWA_HEADS, SWA_HEAD_DIM, base=416)),
        _take_cols(w_in, _per_head(swa, SWA_KV_HEADS, SWA_HEAD_DIM, base=928)),
    ], axis=1)
    assert w0.shape[1] == L0_WIDTH
    nope_map = np.where((mla >= 0) & (mla < MLA_NOPE), mla, -1)
    v_cols = np.concatenate([h * (MLA_NOPE + MLA_V) + MLA_NOPE + np.arange(MLA_V) for h in range(MLA_HEADS)])
    row = lambda v: v.reshape(1, -1).astype(F32)
    return {
        "w0": w0.astype(BF16),
        "wsv": w_in[:, 1056:1184].T.astype(BF16),
        "qa_g": row(qa_g),
        "wqb": _take_cols(wqb, _per_head(mla, MLA_HEADS, MLA_NOPE + MLA_ROPE)).astype(BF16),
        "kva_g": row(kva_g),
        "wkn": _take_cols(wkvb, _per_head(nope_map, MLA_HEADS, MLA_NOPE + MLA_V)).astype(BF16),
        "wvt": jnp.take(wkvb, jnp.asarray(v_cols, jnp.int32), axis=1).T.astype(BF16),
        "mq_g": row(_take_cols(q_g, mla)),
        "mk_g": row(_take_cols(k_g, mla)),
        "sq_g": row(_take_cols(sq_g, swa)),
        "sk_g": row(_take_cols(sk_g, swa)),
        "wa": w_out[:MLA_HEADS * MLA_V].astype(BF16),
        "wb": w_out[MLA_HEADS * MLA_V:].astype(BF16),
    }


def _prep_l1(w_in, q_g, k_g, w_out):
    dm = _diff_lane_map()
    width = DIFF_HEADS * LANES
    row = lambda v: v.reshape(1, -1).astype(F32)
    return {
        "wq": _take_cols(w_in, _per_head(dm, DIFF_HEADS, LANES)).astype(BF16),
        "wk": _take_cols(w_in, _per_head(dm, DIFF_HEADS, LANES, base=width)).astype(BF16),
        "wvt": w_in[:, 2 * width:].T.astype(BF16),
        "q_g": row(_take_cols(q_g, dm % DIFF_HEAD_DIM)),
        "k_g": row(_take_cols(k_g, dm % DIFF_HEAD_DIM)),
        "w_out": w_out.astype(BF16),
    }


def kernel(x, c, ctx, c_ctx, l0_ada_w, l0_ada_b, l0_norm_g, l0_ffn_wg, l0_ffn_wu, l0_ffn_wd, l0_w_in, l0_mla_qa_g, l0_mla_wqb, l0_mla_kva_g, l0_mla_wkvb, l0_mla_q_g, l0_mla_k_g, l0_swa_q_g, l0_swa_k_g, l0_swa_sink, l0_w_out, l1_ada_w, l1_ada_b, l1_norm_g, l1_ffn_wg, l1_ffn_wu, l1_ffn_wd, l1_w_in, l1_q_g, l1_k_g, l1_lambda_q1, l1_lambda_k1, l1_lambda_q2, l1_lambda_k2, l1_subln_g, l1_w_out):
    bsz, seq, _ = x.shape

    pad = (-(bsz + 1)) % 8
    cc = jnp.concatenate([c, c_ctx[None, :], jnp.zeros((pad, D_MODEL), F32)], axis=0)

    def mods(ada_w, ada_b):
        m = _ada(cc, ada_w, ada_b).reshape(cc.shape[0], N_MOD, D_MODEL)
        return m[:bsz], m[bsz:bsz + 1]

    def ffn_weights(wg, wu, wd, i):
        return wg[i].astype(BF16), wu[i].astype(BF16), wd[i].astype(BF16)

    mla_tab = _rope_tables(seq, MLA_ROPE, [(0, 16)], [(64, 16)])
    swa_tab = _rope_tables(seq, SWA_HEAD_DIM, [(0, 32)], [(64, 32)])
    diff_tab = _rope_tables(seq, DIFF_HEAD_DIM, [(0, 32), (32, 32)], [(64, 32), (96, 32)])

    mod, mod_c = mods(l0_ada_w, l0_ada_b)
    g = [l0_norm_g[i:i + 1] for i in range(3)]
    w = _prep_l0(l0_w_in, l0_mla_qa_g, l0_mla_wqb, l0_mla_kva_g, l0_mla_wkvb, l0_mla_q_g, l0_mla_k_g,
                 l0_swa_q_g, l0_swa_k_g, l0_w_out)
    f0 = ffn_weights(l0_ffn_wg, l0_ffn_wu, l0_ffn_wd, 0)
    f1 = ffn_weights(l0_ffn_wg, l0_ffn_wu, l0_ffn_wd, 1)

    h = _ffn(x, mod, True, g[0], *f0, k=0)
    hc = _ffn(ctx, mod_c, False, g[0], *f0, k=0)

    qm, km, vmt, qs, ks, vst = _l0_proj(h, mod, True, g[1], w, mla_tab + swa_tab)
    qm_c, km_c, vmt_c, qs_c, ks_c, vst_c = _l0_proj(hc, mod_c, False, g[1], w, None)
    vsc_t = vst_c.transpose(0, 1, 3, 2, 4).reshape(bsz, SWA_KV_HEADS, SWA_HEAD_DIM, -1)

    at = _mla_attn(qm, km_c, vmt_c, km, vmt)
    bt = _swa_attn(l0_swa_sink, qs, ks_c, vsc_t, ks, vst)
    at_c = _mla_attn(qm_c, km_c, vmt_c)
    bt_c = _swa_attn(l0_swa_sink, qs_c, ks_c, vsc_t)

    h = _l0_out(h, mod, True, at, bt, w["wa"], w["wb"])
    hc = _l0_out(hc, mod_c, False, at_c, bt_c, w["wa"], w["wb"])
    h = _ffn(h, mod, True, g[2], *f1, k=2)
    hc = _ffn(hc, mod_c, False, g[2], *f1, k=2)

    mod, mod_c = mods(l1_ada_w, l1_ada_b)
    g = [l1_norm_g[i:i + 1] for i in range(3)]
    w = _prep_l1(l1_w_in, l1_q_g, l1_k_g, l1_w_out)
    f0 = ffn_weights(l1_ffn_wg, l1_ffn_wu, l1_ffn_wd, 0)
    f1 = ffn_weights(l1_ffn_wg, l1_ffn_wu, l1_ffn_wd, 1)

    h = _ffn(h, mod, True, g[0], *f0, k=0)
    hc = _ffn(hc, mod_c, False, g[0], *f0, k=0)

    qd, kd, vdt = _l1_proj(h, mod, True, g[1], w, diff_tab)
    kd_c, vdt_c = _l1_proj(hc, mod_c, False, g[1], w, None)
    lamv = jnp.stack([l1_lambda_q1, l1_lambda_k1, l1_lambda_q2, l1_lambda_k2]).astype(F32)
    ot = _diff_attn(lamv, l1_subln_g.reshape(-1, 1).astype(F32), qd, kd_c, vdt_c, kd, vdt)
    h = _l1_out(h, mod, ot, w["w_out"])
    return _ffn(h, mod, True, g[2], *f1, k=2)
```

```python
import functools
import math

import numpy as np
import jax
import jax.numpy as jnp
from jax import lax
from jax.experimental import pallas as pl
from jax.experimental.pallas import tpu as pltpu

F32 = jnp.float32
BF16 = jnp.bfloat16

D_MODEL = 1024
D_FF = 2816
N_MOD = 9
EPS = 1e-6
ROPE_BASE = 10000.0
GRID_W = 64
LOG2E = 1.4426950408889634

MLA_HEADS = 8
MLA_Q_RANK = 256
MLA_KV_RANK = 128
MLA_NOPE = 64
MLA_ROPE = 32
MLA_V = 64
SWA_HEADS = 8
SWA_KV_HEADS = 2
SWA_GROUP = SWA_HEADS // SWA_KV_HEADS
SWA_HEAD_DIM = 64
SWA_WINDOW = 128
DIFF_HEADS = 8
DIFF_HEAD_DIM = 64
LAMBDA_INIT_L1 = 0.8 - 0.6 * math.exp(-0.3 * 1)

LANES = 128
HALF = LANES // 2
ATT_TQ = 256
SWA_BLK = 128
TOK_TILE = 512
OUT_TILE = 256
VMEM_LIMIT = 56 * 1024 * 1024
MASKED = -1e30

NT_DIMS = (((1,), (1,)), ((), ()))
TN_DIMS = (((0,), (0,)), ((), ()))


def _mla_lane_map():
    m = np.full(LANES, -1, np.int64)
    m[0:16] = 64 + np.arange(16)
    m[16:48] = np.arange(32)
    m[64:80] = 80 + np.arange(16)
    m[80:112] = 32 + np.arange(32)
    return m


def _swa_lane_map():
    m = np.full(LANES, -1, np.int64)
    m[0:32] = np.arange(32)
    m[64:96] = 32 + np.arange(32)
    return m


def _diff_lane_map():
    m = np.zeros(LANES, np.int64)
    m[0:32] = np.arange(32)
    m[32:64] = 64 + np.arange(32)
    m[64:96] = 32 + np.arange(32)
    m[96:128] = 96 + np.arange(32)
    return m


def _take_cols(w, idx):
    idx = np.asarray(idx)
    out = jnp.take(w, jnp.asarray(np.where(idx < 0, 0, idx), jnp.int32), axis=-1)
    return jnp.where(jnp.asarray(idx >= 0), out, jnp.zeros((), w.dtype))


def _per_head(lane_map, n_heads, stride, base=0):
    cols = [np.where(lane_map >= 0, base + h * stride + lane_map, -1) for h in range(n_heads)]
    return np.concatenate(cols)


def _adaln(x, g, shift, scale):
    ms = jnp.mean(x * x, axis=-1, keepdims=True)
    return x * lax.rsqrt(ms + EPS) * g * (1.0 + scale) + shift


def _rms(t, g, inv_dim):
    ms = jnp.sum(t * t, axis=-1, keepdims=True) * inv_dim
    return t * lax.rsqrt(ms + EPS) * g


def _rope(t, cos, sin):
    return t * cos + pltpu.roll(t, HALF, 1) * sin


def _dot(a, b):
    return jnp.dot(a, b, preferred_element_type=F32)


def _dot_nt(a, b):
    return lax.dot_general(a, b, NT_DIMS, preferred_element_type=F32)


def _dot_tn(a, b):
    return lax.dot_general(a, b, TN_DIMS, preferred_element_type=F32)


def _params(n_axes):
    return pltpu.CompilerParams(dimension_semantics=("arbitrary",) * n_axes,
                                vmem_limit_bytes=VMEM_LIMIT)


def _const_spec(shape):
    nd = len(shape)
    return pl.BlockSpec(shape, lambda *_: (0,) * nd, pipeline_mode=pl.Buffered(1))


def _ada_kernel(c_ref, w_ref, b_ref, o_ref):
    c = c_ref[...]
    a = (c * jax.nn.sigmoid(c)).astype(BF16)
    o_ref[...] = _dot(a, w_ref[...].astype(BF16)) + b_ref[...]


def _ada(cc, w, b):
    rows = cc.shape[0]
    n = w.shape[1]
    tn = 1152
    return pl.pallas_call(
        _ada_kernel,
        grid=(n // tn,),
        in_specs=[pl.BlockSpec((rows, D_MODEL), lambda j: (0, 0)),
                  pl.BlockSpec((D_MODEL, tn), lambda j: (0, j)),
                  pl.BlockSpec((1, tn), lambda j: (0, j))],
        out_specs=pl.BlockSpec((rows, tn), lambda j: (0, j)),
        out_shape=jax.ShapeDtypeStruct((rows, n), F32),
        compiler_params=_params(1),
        name="ada_mod",
    )(cc, w, b.reshape(1, n))


def _rope_table_kernel(inv_ref, userow_ref, sign_ref, cos_ref, sin_ref):
    n = cos_ref.shape[0]
    t = lax.broadcasted_iota(jnp.int32, (n, LANES), 0)
    row = lax.shift_right_logical(t, GRID_W.bit_length() - 1).astype(F32)
    col = (t & (GRID_W - 1)).astype(F32)
    pos = jnp.where(userow_ref[...] > 0.5, row, col)
    ang = pos * inv_ref[...]
    sign = sign_ref[...]
    active = sign != 0.0
    cos_ref[...] = jnp.where(active, jnp.cos(ang), 1.0)
    sin_ref[...] = jnp.where(active, sign * jnp.sin(ang), 0.0)


def _rope_tables(seq, rot_dim, x1_lanes, x2_lanes):
    n_f = rot_dim // 4
    half = rot_dim // 2
    inv_f = (ROPE_BASE ** (-np.arange(n_f, dtype=np.float64) / n_f)).astype(np.float32)
    inv_half = np.concatenate([inv_f, inv_f])
    use_row_half = np.concatenate([np.ones(n_f), np.zeros(n_f)])
    inv = np.zeros((1, LANES), np.float32)
    use_row = np.zeros((1, LANES), np.float32)
    sign = np.zeros((1, LANES), np.float32)
    for lanes, sgn in ((x1_lanes, -1.0), (x2_lanes, 1.0)):
        for start, n in lanes:
            assert n == half
            inv[0, start:start + n] = inv_half
            use_row[0, start:start + n] = use_row_half
            sign[0, start:start + n] = sgn
    vec = pl.BlockSpec((1, LANES), lambda: (0, 0))
    tab = pl.BlockSpec((seq, LANES), lambda: (0, 0))
    return pl.pallas_call(
        _rope_table_kernel,
        in_specs=[vec, vec, vec],
        out_specs=[tab, tab],
        out_shape=[jax.ShapeDtypeStruct((seq, LANES), F32)] * 2,
        name="rope_table",
    )(jnp.asarray(inv), jnp.asarray(use_row), jnp.asarray(sign))


def _ffn_kernel(x_ref, mod_ref, g_ref, wg_ref, wu_ref, wd_ref, o_ref, *, k):
    x = x_ref[...]
    shift = mod_ref[3 * k:3 * k + 1, :]
    scale = mod_ref[3 * k + 1:3 * k + 2, :]
    gate = mod_ref[3 * k + 2:3 * k + 3, :]
    hn = _adaln(x, g_ref[...], shift, scale).astype(BF16)
    g = _dot(hn, wg_ref[...])
    u = _dot(hn, wu_ref[...])
    a = (g * jax.nn.sigmoid(g) * u).astype(BF16)
    o_ref[...] = x + (0.5 * gate) * _dot(a, wd_ref[...])


def _mod_spec(per_batch):
    if per_batch:
        return pl.BlockSpec((None, N_MOD, D_MODEL), lambda b, t: (b, 0, 0))
    return pl.BlockSpec((None, N_MOD, D_MODEL), lambda b, t: (0, 0, 0))


def _tok_spec(tm):
    return pl.BlockSpec((None, tm, D_MODEL), lambda b, t: (b, t, 0))


def _ffn(x, mod, per_batch, g, wg, wu, wd, k):
    bsz, n, _ = x.shape
    tm = min(TOK_TILE, n)
    return pl.pallas_call(
        functools.partial(_ffn_kernel, k=k),
        grid=(bsz, n // tm),
        in_specs=[_tok_spec(tm), _mod_spec(per_batch), _const_spec((1, D_MODEL)),
                  _const_spec((D_MODEL, D_FF)), _const_spec((D_MODEL, D_FF)),
                  _const_spec((D_FF, D_MODEL))],
        out_specs=_tok_spec(tm),
        out_shape=jax.ShapeDtypeStruct(x.shape, F32),
        compiler_params=_params(2),
        name="ffn",
    )(x, mod, g, wg, wu, wd)


L0_QA = (0, 256)
L0_KVA = (256, 384)
L0_KR = (384, 512)
L0_SQ = (512, 1536)
L0_SK = (1536, 1792)
L0_WIDTH = 1792


def _l0_proj_kernel(*refs, rope):
    (h_ref, mod_ref, g_ref, w0_ref, wsv_ref, qag_ref, wqb_ref, kvag_ref, wkn_ref, wvt_ref,
     mqg_ref, mkg_ref, sqg_ref, skg_ref) = refs[:14]
    if rope:
        cm_ref, sm_ref, cs_ref, ss_ref = refs[14:18]
        outs = refs[18:]
        cm, sm, cs, ss = cm_ref[...], sm_ref[...], cs_ref[...], ss_ref[...]
    else:
        outs = refs[14:]
    qm_ref, km_ref, vmt_ref, qs_ref, ks_ref, vst_ref = outs

    hn = _adaln(h_ref[...], g_ref[...], mod_ref[3:4, :], mod_ref[4:5, :]).astype(BF16)
    p = _dot(hn, w0_ref[...])

    mla_inv = 1.0 / (MLA_NOPE + MLA_ROPE)
    swa_inv = 1.0 / SWA_HEAD_DIM
    mqg = mqg_ref[...] * (mla_inv ** 0.5 * LOG2E)
    sqg = sqg_ref[...] * (swa_inv ** 0.5 * LOG2E)
    mkg = mkg_ref[...]
    skg = skg_ref[...]

    qa = _rms(p[:, L0_QA[0]:L0_QA[1]], qag_ref[...], 1.0 / MLA_Q_RANK).astype(BF16)
    q = _dot(qa, wqb_ref[...])
    for h in range(MLA_HEADS):
        t = _rms(q[:, h * LANES:(h + 1) * LANES], mqg, mla_inv)
        if rope:
            t = _rope(t, cm, sm)
        qm_ref[h] = t.astype(BF16)

    kva = _rms(p[:, L0_KVA[0]:L0_KVA[1]], kvag_ref[...], 1.0 / MLA_KV_RANK).astype(BF16)
    kn = _dot(kva, wkn_ref[...])
    kr = p[:, L0_KR[0]:L0_KR[1]]
    for h in range(MLA_HEADS):
        t = _rms(kn[:, h * LANES:(h + 1) * LANES] + kr, mkg, mla_inv)
        if rope:
            t = _rope(t, cm, sm)
        km_ref[h] = t.astype(BF16)
    vt = _dot_nt(wvt_ref[...], kva)
    for h in range(MLA_HEADS):
        vmt_ref[h] = vt[h * MLA_V:(h + 1) * MLA_V, :].astype(BF16)

    for h in range(SWA_HEADS):
        t = _rms(p[:, L0_SQ[0] + h * LANES:L0_SQ[0] + (h + 1) * LANES], sqg, swa_inv)
        if rope:
            t = _rope(t, cs, ss)
        qs_ref[h] = t.astype(BF16)
    for j in range(SWA_KV_HEADS):
        t = _rms(p[:, L0_SK[0] + j * LANES:L0_SK[0] + (j + 1) * LANES], skg, swa_inv)
        if rope:
            t = _rope(t, cs, ss)
        ks_ref[j] = t.astype(BF16)
    svt = _dot_nt(wsv_ref[...], hn)
    n_blk = svt.shape[1] // SWA_BLK
    for j in range(SWA_KV_HEADS):
        for i in range(n_blk):
            vst_ref[j, i] = svt[j * SWA_HEAD_DIM:(j + 1) * SWA_HEAD_DIM,
                                i * SWA_BLK:(i + 1) * SWA_BLK].astype(BF16)


def _l0_proj(h, mod, per_batch, g, w, tables):
    bsz, n, _ = h.shape
    tm = min(TOK_TILE, n)
    rope = tables is not None
    weights = [g, w["w0"], w["wsv"], w["qa_g"], w["wqb"], w["kva_g"], w["wkn"], w["wvt"],
               w["mq_g"], w["mk_g"], w["sq_g"], w["sk_g"]]
    in_specs = [_tok_spec(tm), _mod_spec(per_batch)] + [_const_spec(a.shape) for a in weights]
    args = [h, mod] + weights
    if rope:
        in_specs += [pl.BlockSpec((tm, LANES), lambda b, t: (t, 0))] * 4
        args += list(tables)
    head = lambda nh: pl.BlockSpec((None, nh, tm, LANES), lambda b, t: (b, 0, t, 0))
    out_specs = [head(MLA_HEADS), head(MLA_HEADS),
                 pl.BlockSpec((None, MLA_HEADS, MLA_V, tm), lambda b, t: (b, 0, 0, t)),
                 head(SWA_HEADS), head(SWA_KV_HEADS),
                 pl.BlockSpec((None, SWA_KV_HEADS, tm // SWA_BLK, SWA_HEAD_DIM, SWA_BLK),
                              lambda b, t: (b, 0, t, 0, 0))]
    out_shape = [jax.ShapeDtypeStruct((bsz, MLA_HEADS, n, LANES), BF16),
                 jax.ShapeDtypeStruct((bsz, MLA_HEADS, n, LANES), BF16),
                 jax.ShapeDtypeStruct((bsz, MLA_HEADS, MLA_V, n), BF16),
                 jax.ShapeDtypeStruct((bsz, SWA_HEADS, n, LANES), BF16),
                 jax.ShapeDtypeStruct((bsz, SWA_KV_HEADS, n, LANES), BF16),
                 jax.ShapeDtypeStruct((bsz, SWA_KV_HEADS, n // SWA_BLK, SWA_HEAD_DIM, SWA_BLK), BF16)]
    return pl.pallas_call(
        functools.partial(_l0_proj_kernel, rope=rope),
        grid=(bsz, n // tm),
        in_specs=in_specs, out_specs=out_specs, out_shape=out_shape,
        compiler_params=_params(2),
        name="l0_proj",
    )(*args)


def _two_stage_loop(n_items, produce, consume, slot0, slot1):
    produce(0, slot0)

    def pair(j, carry):
        i = 2 * j
        produce(i + 1, slot1)
        consume(i, slot0)
        produce(jnp.minimum(i + 2, n_items - 1), slot0)
        consume(i + 1, slot1)
        return carry

    lax.fori_loop(0, n_items // 2, pair, 0)


KEY_CHUNK = 256
SUBLANES = 8


def _chunk_fold(x, op):
    return op(x.reshape(KEY_CHUNK // SUBLANES, SUBLANES, x.shape[-1]), axis=0)


def _column_max(s_ref):
    n_keys = s_ref.shape[0]
    mx = None
    for k0 in range(0, n_keys, KEY_CHUNK):
        c = _chunk_fold(s_ref[k0:k0 + KEY_CHUNK, :], jnp.max)
        mx = c if mx is None else jnp.maximum(mx, c)
    return jnp.max(mx, axis=0, keepdims=True)


def _exp_pv(s_ref, m, value_t):
    n_keys = s_ref.shape[0]
    l8 = None
    o = None
    for k0 in range(0, n_keys, KEY_CHUNK):
        e = jnp.exp2(s_ref[k0:k0 + KEY_CHUNK, :] - m)
        part = _chunk_fold(e, jnp.sum)
        l8 = part if l8 is None else l8 + part
        d = _dot(value_t(k0), e.astype(BF16))
        o = d if o is None else o + d
    return o, jnp.sum(l8, axis=0, keepdims=True)


def _mla_attn_kernel(q_ref, kc_ref, vct_ref, kl_ref, vlt_ref, o_ref, s0_ref, m0_ref, s1_ref, m1_ref):
    n_tiles, n_heads = o_ref.shape[:2]
    n_ctx = kc_ref.shape[1]

    def split(i):
        return lax.div(i, n_tiles), lax.rem(i, n_tiles)

    def scores(i, slot):
        s_ref, m_ref = slot
        h, t = split(i)
        q = q_ref[h, pl.ds(pl.multiple_of(t * ATT_TQ, ATT_TQ), ATT_TQ), :]
        s_ref[0:n_ctx, :] = _dot_nt(kc_ref[h], q)
        s_ref[n_ctx:, :] = _dot_nt(kl_ref[h], q)
        m_ref[...] = _column_max(s_ref)

    def softmax_pv(i, slot):
        s_ref, m_ref = slot
        h, t = split(i)

        def value_t(k0):
            if k0 < n_ctx:
                return vct_ref[h, :, k0:k0 + KEY_CHUNK]
            return vlt_ref[h, :, k0 - n_ctx:k0 - n_ctx + KEY_CHUNK]

        o, l = _exp_pv(s_ref, m_ref[...], value_t)
        o_ref[t, h] = (o * (1.0 / l)).astype(BF16)

    _two_stage_loop(n_tiles * n_heads, scores, softmax_pv, (s0_ref, m0_ref), (s1_ref, m1_ref))


ATT_HEADS_PER_STEP = 4


def _mla_attn(q, kc, vct, kl, vlt):
    bsz, nh, nq, _ = q.shape
    hs = ATT_HEADS_PER_STEP
    n_keys = kc.shape[2] + kl.shape[2]
    grp = lambda shape: pl.BlockSpec((None, hs) + tuple(shape), lambda b, g: (b, g) + (0,) * len(shape))
    n_tiles = nq // ATT_TQ
    slot = [pltpu.VMEM((n_keys, ATT_TQ), F32), pltpu.VMEM((1, ATT_TQ), F32)]
    return pl.pallas_call(
        _mla_attn_kernel,
        grid=(bsz, nh // hs),
        in_specs=[grp(q.shape[2:]), grp(kc.shape[2:]), grp(vct.shape[2:]), grp(kl.shape[2:]), grp(vlt.shape[2:])],
        out_specs=pl.BlockSpec((None, n_tiles, hs, MLA_V, ATT_TQ), lambda b, g: (b, 0, g, 0, 0)),
        out_shape=jax.ShapeDtypeStruct((bsz, n_tiles, nh, MLA_V, ATT_TQ), BF16),
        scratch_shapes=slot + slot,
        compiler_params=_params(2),
        name="mla_attn",
    )(q, kc, vct, kl, vlt)


def _mla_ctx_attn_kernel(q_ref, kc_ref, vct_ref, o_ref):
    sc = _dot_nt(kc_ref[...], q_ref[...])
    pc = jnp.exp2(sc - jnp.max(sc, axis=0, keepdims=True))
    l = jnp.sum(pc, axis=0, keepdims=True)
    o = _dot(vct_ref[...], pc.astype(BF16))
    o_ref[...] = (o * (1.0 / l)).astype(BF16)


def _mla_ctx_attn(q, kc, vct):
    bsz, nh, nq, _ = q.shape
    assert nq == ATT_TQ
    bh = lambda shape: pl.BlockSpec((None, None) + tuple(shape), lambda b, h: (b, h, 0, 0))
    return pl.pallas_call(
        _mla_ctx_attn_kernel,
        grid=(bsz, nh),
        in_specs=[bh(q.shape[2:]), bh(kc.shape[2:]), bh(vct.shape[2:])],
        out_specs=pl.BlockSpec((None, None, None, MLA_V, ATT_TQ), lambda b, h: (b, 0, h, 0, 0)),
        out_shape=jax.ShapeDtypeStruct((bsz, 1, nh, MLA_V, ATT_TQ), BF16),
        compiler_params=_params(2),
        name="mla_ctx_attn",
    )(q, kc, vct)


def _swa_attn_kernel(*refs, has_band):
    if has_band:
        sink_ref, q_ref, kc_ref, vct_ref, kl_ref, vlt_ref, o_ref = refs
    else:
        sink_ref, q_ref, kc_ref, vct_ref, o_ref = refs
    n_blk = o_ref.shape[0]
    j = pl.program_id(1)
    nq = SWA_GROUP * SWA_BLK
    sink = jnp.concatenate(
        [jnp.full((1, SWA_BLK), sink_ref[j * SWA_GROUP + g] * LOG2E, F32) for g in range(SWA_GROUP)],
        axis=1)
    band = 3 * SWA_BLK

    def block(n, carry):
        q0 = pl.multiple_of(n * SWA_BLK, SWA_BLK)
        q = jnp.concatenate([q_ref[g, pl.ds(q0, SWA_BLK), :] for g in range(SWA_GROUP)], axis=0)
        sc = _dot_nt(kc_ref[...], q)
        m = jnp.maximum(jnp.max(sc, axis=0, keepdims=True), sink)
        if has_band:
            b0 = jnp.clip(n - 1, 0, n_blk - 3)
            k0 = pl.multiple_of(b0 * SWA_BLK, SWA_BLK)
            sb = _dot_nt(kl_ref[pl.ds(k0, band), :], q)
            kpos = k0 + lax.broadcasted_iota(jnp.int32, (band, nq), 0)
            qpos = q0 + (lax.broadcasted_iota(jnp.int32, (band, nq), 1) & (SWA_BLK - 1))
            sb = jnp.where(jnp.abs(kpos - qpos) <= SWA_WINDOW, sb, MASKED)
            m = jnp.maximum(m, jnp.max(sb, axis=0, keepdims=True))
        pc = jnp.exp2(sc - m)
        l = jnp.sum(pc, axis=0, keepdims=True) + jnp.exp2(sink - m)
        o = _dot(vct_ref[...], pc.astype(BF16))
        if has_band:
            pb = jnp.exp2(sb - m)
            l = l + jnp.sum(pb, axis=0, keepdims=True)
            vb = jnp.concatenate([vlt_ref[b0], vlt_ref[b0 + 1], vlt_ref[b0 + 2]], axis=1)
            o = o + _dot(vb, pb.astype(BF16))
        o = o * (1.0 / l)
        for g in range(SWA_GROUP):
            o_ref[n, g] = o[:, g * SWA_BLK:(g + 1) * SWA_BLK].astype(BF16)
        return carry

    lax.fori_loop(0, n_blk, block, 0)


def _swa_attn(sink, q, kc, vct, kl=None, vlt=None):
    bsz, nh, nq, _ = q.shape
    has_band = kl is not None
    n_blk = nq // SWA_BLK
    in_specs = [pl.BlockSpec(memory_space=pltpu.SMEM),
                pl.BlockSpec((None, SWA_GROUP, nq, LANES), lambda b, j: (b, j, 0, 0)),
                pl.BlockSpec((None, None) + kc.shape[2:], lambda b, j: (b, j, 0, 0)),
                pl.BlockSpec((None, None) + vct.shape[2:], lambda b, j: (b, j, 0, 0))]
    args = [sink, q, kc, vct]
    if has_band:
        in_specs += [pl.BlockSpec((None, None) + kl.shape[2:], lambda b, j: (b, j, 0, 0)),
                     pl.BlockSpec((None, None) + vlt.shape[2:], lambda b, j: (b, j, 0, 0, 0))]
        args += [kl, vlt]
    return pl.pallas_call(
        functools.partial(_swa_attn_kernel, has_band=has_band),
        grid=(bsz, SWA_KV_HEADS),
        in_specs=in_specs,
        out_specs=pl.BlockSpec((None, n_blk, SWA_GROUP, SWA_HEAD_DIM, SWA_BLK),
                               lambda b, j: (b, 0, j, 0, 0)),
        out_shape=jax.ShapeDtypeStruct((bsz, n_blk, nh, SWA_HEAD_DIM, SWA_BLK), BF16),
        compiler_params=_params(2),
        name="swa_attn",
    )(*args)


def _l0_out_kernel(h_ref, mod_ref, at_ref, bt_ref, wa_ref, wb_ref, o_ref):
    a = at_ref[...].reshape(MLA_HEADS * MLA_V, OUT_TILE)
    y = _dot_tn(a, wa_ref[...])
    yb = []
    for i in range(OUT_TILE // SWA_BLK):
        bt = bt_ref[i].reshape(SWA_HEADS * SWA_HEAD_DIM, SWA_BLK)
        yb.append(_dot_tn(bt, wb_ref[...]))
    y = y + jnp.concatenate(yb, axis=0)
    o_ref[...] = h_ref[...] + mod_ref[5:6, :] * y


def _l0_out(h, mod, per_batch, at, bt, wa, wb):
    bsz, n, _ = h.shape
    tm = OUT_TILE
    return pl.pallas_call(
        _l0_out_kernel,
        grid=(bsz, n // tm),
        in_specs=[_tok_spec(tm), _mod_spec(per_batch),
                  pl.BlockSpec((None, None, MLA_HEADS, MLA_V, ATT_TQ), lambda b, t: (b, t, 0, 0, 0)),
                  pl.BlockSpec((None, tm // SWA_BLK, SWA_HEADS, SWA_HEAD_DIM, SWA_BLK),
                               lambda b, t: (b, t, 0, 0, 0)),
                  _const_spec(wa.shape), _const_spec(wb.shape)],
        out_specs=_tok_spec(tm),
        out_shape=jax.ShapeDtypeStruct(h.shape, F32),
        compiler_params=_params(2),
        name="l0_out",
    )(h, mod, at, bt, wa, wb)


def _diff_norm(t, g, mask0):
    sq = t * t
    s0 = jnp.sum(jnp.where(mask0, sq, 0.0), axis=-1, keepdims=True)
    s1 = jnp.sum(jnp.where(mask0, 0.0, sq), axis=-1, keepdims=True)
    inv = 1.0 / DIFF_HEAD_DIM
    r = jnp.where(mask0, lax.rsqrt(s0 * inv + EPS), lax.rsqrt(s1 * inv + EPS))
    return t * r * g


def _l1_proj_kernel(*refs, latent):
    if latent:
        (h_ref, mod_ref, g_ref, wq_ref, wk_ref, wvt_ref, qg_ref, kg_ref, cd_ref, sd_ref,
         q_ref, k_ref, vt_ref) = refs
        cd, sd = cd_ref[...], sd_ref[...]
    else:
        h_ref, mod_ref, g_ref, wk_ref, wvt_ref, kg_ref, k_ref, vt_ref = refs
    hn = _adaln(h_ref[...], g_ref[...], mod_ref[3:4, :], mod_ref[4:5, :]).astype(BF16)
    tm = hn.shape[0]
    lane = lax.broadcasted_iota(jnp.int32, (tm, LANES), 1)
    mask0 = (lane & (HALF - 1)) < (HALF // 2)
    if latent:
        qg = qg_ref[...] * (DIFF_HEAD_DIM ** -0.5 * LOG2E)
        q = _dot(hn, wq_ref[...])
        for h in range(DIFF_HEADS):
            t = _rope(_diff_norm(q[:, h * LANES:(h + 1) * LANES], qg, mask0), cd, sd)
            q_ref[h] = t.astype(BF16)
    kg = kg_ref[...]
    k = _dot(hn, wk_ref[...])
    for h in range(DIFF_HEADS):
        t = _diff_norm(k[:, h * LANES:(h + 1) * LANES], kg, mask0)
        if latent:
            t = _rope(t, cd, sd)
        k_ref[h, 0] = jnp.where(mask0, t, 0.0).astype(BF16)
        k_ref[h, 1] = jnp.where(mask0, 0.0, t).astype(BF16)
    vt = _dot_nt(wvt_ref[...], hn)
    for h in range(DIFF_HEADS):
        vt_ref[h] = vt[h * LANES:(h + 1) * LANES, :].astype(BF16)


def _l1_proj(h, mod, per_batch, g, w, tables):
    bsz, n, _ = h.shape
    tm = min(TOK_TILE, n)
    latent = tables is not None
    k_spec = pl.BlockSpec((None, DIFF_HEADS, 2, tm, LANES), lambda b, t: (b, 0, 0, t, 0))
    vt_spec = pl.BlockSpec((None, DIFF_HEADS, LANES, tm), lambda b, t: (b, 0, 0, t))
    k_shape = jax.ShapeDtypeStruct((bsz, DIFF_HEADS, 2, n, LANES), BF16)
    vt_shape = jax.ShapeDtypeStruct((bsz, DIFF_HEADS, LANES, n), BF16)
    if latent:
        weights = [g, w["wq"], w["wk"], w["wvt"], w["q_g"], w["k_g"]]
        in_specs = ([_tok_spec(tm), _mod_spec(per_batch)] + [_const_spec(a.shape) for a in weights]
                    + [pl.BlockSpec((tm, LANES), lambda b, t: (t, 0))] * 2)
        args = [h, mod] + weights + list(tables)
        out_specs = [pl.BlockSpec((None, DIFF_HEADS, tm, LANES), lambda b, t: (b, 0, t, 0)),
                     k_spec, vt_spec]
        out_shape = [jax.ShapeDtypeStruct((bsz, DIFF_HEADS, n, LANES), BF16), k_shape, vt_shape]
    else:
        weights = [g, w["wk"], w["wvt"], w["k_g"]]
        in_specs = [_tok_spec(tm), _mod_spec(per_batch)] + [_const_spec(a.shape) for a in weights]
        args = [h, mod] + weights
        out_specs = [k_spec, vt_spec]
        out_shape = [k_shape, vt_shape]
    return pl.pallas_call(
        functools.partial(_l1_proj_kernel, latent=latent),
        grid=(bsz, n // tm),
        in_specs=in_specs, out_specs=out_specs, out_shape=out_shape,
        compiler_params=_params(2),
        name="l1_proj",
    )(*args)


def _diff_attn_kernel(lam_ref, subg_ref, q_ref, kc_ref, vct_ref, kl_ref, vlt_ref, o_ref,
                      s0_ref, m0_ref, s1_ref, m1_ref):
    n_tiles, n_heads = o_ref.shape[:2]
    n_ctx = kc_ref.shape[2]
    lv = lam_ref[...]
    lam = (jnp.exp(jnp.sum(lv[0:1] * lv[1:2], axis=-1, keepdims=True))
           - jnp.exp(jnp.sum(lv[2:3] * lv[3:4], axis=-1, keepdims=True)) + LAMBDA_INIT_L1)
    subg = subg_ref[...] * (1.0 - LAMBDA_INIT_L1)

    def split(i):
        return lax.div(i, n_tiles), lax.rem(i, n_tiles)

    def scores(i, slot):
        s_ref, m_ref = slot
        h, t = split(i)
        q = q_ref[h, pl.ds(pl.multiple_of(t * ATT_TQ, ATT_TQ), ATT_TQ), :]
        for s in range(2):
            s_ref[s, 0:n_ctx, :] = _dot_nt(kc_ref[h, s], q)
            s_ref[s, n_ctx:, :] = _dot_nt(kl_ref[h, s], q)
            m_ref[s] = _column_max(s_ref.at[s])

    def softmax_pv(i, slot):
        s_ref, m_ref = slot
        h, t = split(i)

        def value_t(k0):
            if k0 < n_ctx:
                return vct_ref[h, :, k0:k0 + KEY_CHUNK]
            return vlt_ref[h, :, k0 - n_ctx:k0 - n_ctx + KEY_CHUNK]

        part = []
        for s in range(2):
            o, l = _exp_pv(s_ref.at[s], m_ref[s], value_t)
            part.append((o, 1.0 / l))
        o = part[0][0] * part[0][1] - part[1][0] * (lam * part[1][1])
        ms = jnp.mean(o * o, axis=0, keepdims=True)
        o_ref[t, h] = (o * lax.rsqrt(ms + EPS) * subg).astype(BF16)

    _two_stage_loop(n_tiles * n_heads, scores, softmax_pv, (s0_ref, m0_ref), (s1_ref, m1_ref))


def _diff_attn(lamv, subg, q, kc, vct, kl, vlt):
    bsz, nh, nq, _ = q.shape
    hs = ATT_HEADS_PER_STEP
    n_tiles = nq // ATT_TQ
    n_keys = kc.shape[3] + kl.shape[3]
    grp = lambda shape: pl.BlockSpec((None, hs) + tuple(shape), lambda b, g: (b, g) + (0,) * len(shape))
    slot = [pltpu.VMEM((2, n_keys, ATT_TQ), F32), pltpu.VMEM((2, 1, ATT_TQ), F32)]
    return pl.pallas_call(
        _diff_attn_kernel,
        grid=(bsz, nh // hs),
        in_specs=[_const_spec(lamv.shape), _const_spec(subg.shape),
                  grp(q.shape[2:]), grp(kc.shape[2:]), grp(vct.shape[2:]), grp(kl.shape[2:]), grp(vlt.shape[2:])],
        out_specs=pl.BlockSpec((None, n_tiles, hs, LANES, ATT_TQ), lambda b, g: (b, 0, g, 0, 0)),
        out_shape=jax.ShapeDtypeStruct((bsz, n_tiles, nh, LANES, ATT_TQ), BF16),
        scratch_shapes=slot + slot,
        compiler_params=_params(2),
        name="diff_attn",
    )(lamv, subg, q, kc, vct, kl, vlt)


def _l1_out_kernel(h_ref, mod_ref, ot_ref, w_ref, o_ref):
    a = ot_ref[...].reshape(DIFF_HEADS * LANES, OUT_TILE)
    o_ref[...] = h_ref[...] + mod_ref[5:6, :] * _dot_tn(a, w_ref[...])


def _l1_out(h, mod, ot, w):
    bsz, n, _ = h.shape
    tm = OUT_TILE
    return pl.pallas_call(
        _l1_out_kernel,
        grid=(bsz, n // tm),
        in_specs=[_tok_spec(tm), _mod_spec(True),
                  pl.BlockSpec((None, None, DIFF_HEADS, LANES, ATT_TQ), lambda b, t: (b, t, 0, 0, 0)),
                  _const_spec(w.shape)],
        out_specs=_tok_spec(tm),
        out_shape=jax.ShapeDtypeStruct(h.shape, F32),
        compiler_params=_params(2),
        name="l1_out",
    )(h, mod, ot, w)


def _prep_l0(w_in, qa_g, wqb, kva_g, wkvb, q_g, k_g, sq_g, sk_g, w_out):
    mla = _mla_lane_map()
    swa = _swa_lane_map()
    kr_cols = np.where(mla >= MLA_NOPE, 384 + mla - MLA_NOPE, -1)
    w0 = jnp.concatenate([
        w_in[:, 0:384],
        _take_cols(w_in, kr_cols),
        _take_cols(w_in, _per_head(swa, SWA_HEADS, SWA_HEAD_DIM, base=416)),
        _take_cols(w_in, _per_head(swa, SWA_KV_HEADS, SWA_HEAD_DIM, base=928)),
    ], axis=1)
    assert w0.shape[1] == L0_WIDTH
    nope_map = np.where((mla >= 0) & (mla < MLA_NOPE), mla, -1)
    v_cols = np.concatenate([h * (MLA_NOPE + MLA_V) + MLA_NOPE + np.arange(MLA_V) for h in range(MLA_HEADS)])
    row = lambda v: v.reshape(1, -1).astype(F32)
    return {
        "w0": w0.astype(BF16),
        "wsv": w_in[:, 1056:1184].T.astype(BF16),
        "qa_g": row(qa_g),
        "wqb": _take_cols(wqb, _per_head(mla, MLA_HEADS, MLA_NOPE + MLA_ROPE)).astype(BF16),
        "kva_g": row(kva_g),
        "wkn": _take_cols(wkvb, _per_head(nope_map, MLA_HEADS, MLA_NOPE + MLA_V)).astype(BF16),
        "wvt": jnp.take(wkvb, jnp.asarray(v_cols, jnp.int32), axis=1).T.astype(BF16),
        "mq_g": row(_take_cols(q_g, mla)),
        "mk_g": row(_take_cols(k_g, mla)),
        "sq_g": row(_take_cols(sq_g, swa)),
        "sk_g": row(_take_cols(sk_g, swa)),
        "wa": w_out[:MLA_HEADS * MLA_V].astype(BF16),
        "wb": w_out[MLA_HEADS * MLA_V:].astype(BF16),
    }


def _prep_l1(w_in, q_g, k_g, w_out):
    dm = _diff_lane_map()
    width = DIFF_HEADS * LANES
    row = lambda v: v.reshape(1, -1).astype(F32)
    return {
        "wq": _take_cols(w_in, _per_head(dm, DIFF_HEADS, LANES)).astype(BF16),
        "wk": _take_cols(w_in, _per_head(dm, DIFF_HEADS, LANES, base=width)).astype(BF16),
        "wvt": w_in[:, 2 * width:].T.astype(BF16),
        "q_g": row(_take_cols(q_g, dm % DIFF_HEAD_DIM)),
        "k_g": row(_take_cols(k_g, dm % DIFF_HEAD_DIM)),
        "w_out": w_out.astype(BF16),
    }


def kernel(x, c, ctx, c_ctx, l0_ada_w, l0_ada_b, l0_norm_g, l0_ffn_wg, l0_ffn_wu, l0_ffn_wd, l0_w_in, l0_mla_qa_g, l0_mla_wqb, l0_mla_kva_g, l0_mla_wkvb, l0_mla_q_g, l0_mla_k_g, l0_swa_q_g, l0_swa_k_g, l0_swa_sink, l0_w_out, l1_ada_w, l1_ada_b, l1_norm_g, l1_ffn_wg, l1_ffn_wu, l1_ffn_wd, l1_w_in, l1_q_g, l1_k_g, l1_lambda_q1, l1_lambda_k1, l1_lambda_q2, l1_lambda_k2, l1_subln_g, l1_w_out):
    bsz, seq, _ = x.shape

    pad = (-(bsz + 1)) % 8
    cc = jnp.concatenate([c, c_ctx[None, :], jnp.zeros((pad, D_MODEL), F32)], axis=0)

    def mods(ada_w, ada_b):
        m = _ada(cc, ada_w, ada_b).reshape(cc.shape[0], N_MOD, D_MODEL)
        return m[:bsz], m[bsz:bsz + 1]

    def ffn_weights(wg, wu, wd, i):
        return wg[i].astype(BF16), wu[i].astype(BF16), wd[i].astype(BF16)

    mla_tab = _rope_tables(seq, MLA_ROPE, [(0, 16)], [(64, 16)])
    swa_tab = _rope_tables(seq, SWA_HEAD_DIM, [(0, 32)], [(64, 32)])
    diff_tab = _rope_tables(seq, DIFF_HEAD_DIM, [(0, 32), (32, 32)], [(64, 32), (96, 32)])

    mod, mod_c = mods(l0_ada_w, l0_ada_b)
    g = [l0_norm_g[i:i + 1] for i in range(3)]
    w = _prep_l0(l0_w_in, l0_mla_qa_g, l0_mla_wqb, l0_mla_kva_g, l0_mla_wkvb, l0_mla_q_g, l0_mla_k_g,
                 l0_swa_q_g, l0_swa_k_g, l0_w_out)
    f0 = ffn_weights(l0_ffn_wg, l0_ffn_wu, l0_ffn_wd, 0)
    f1 = ffn_weights(l0_ffn_wg, l0_ffn_wu, l0_ffn_wd, 1)

    h = _ffn(x, mod, True, g[0], *f0, k=0)
    hc = _ffn(ctx, mod_c, False, g[0], *f0, k=0)

    qm, km, vmt, qs, ks, vst = _l0_proj(h, mod, True, g[1], w, mla_tab + swa_tab)
    qm_c, km_c, vmt_c, qs_c, ks_c, vst_c = _l0_proj(hc, mod_c, False, g[1], w, None)
    vsc_t = vst_c.transpose(0, 1, 3, 2, 4).reshape(bsz, SWA_KV_HEADS, SWA_HEAD_DIM, -1)

    at = _mla_attn(qm, km_c, vmt_c, km, vmt)
    bt = _swa_attn(l0_swa_sink, qs, ks_c, vsc_t, ks, vst)
    at_c = _mla_ctx_attn(qm_c, km_c, vmt_c)
    bt_c = _swa_attn(l0_swa_sink, qs_c, ks_c, vsc_t)

    h = _l0_out(h, mod, True, at, bt, w["wa"], w["wb"])
    hc = _l0_out(hc, mod_c, False, at_c, bt_c, w["wa"], w["wb"])
    h = _ffn(h, mod, True, g[2], *f1, k=2)
    hc = _ffn(hc, mod_c, False, g[2], *f1, k=2)

    mod, mod_c = mods(l1_ada_w, l1_ada_b)
    g = [l1_norm_g[i:i + 1] for i in range(3)]
    w = _prep_l1(l1_w_in, l1_q_g, l1_k_g, l1_w_out)
    f0 = ffn_weights(l1_ffn_wg, l1_ffn_wu, l1_ffn_wd, 0)
    f1 = ffn_weights(l1_ffn_wg, l1_ffn_wu, l1_ffn_wd, 1)

    h = _ffn(h, mod, True, g[0], *f0, k=0)
    hc = _ffn(hc, mod_c, False, g[0], *f0, k=0)

    qd, kd, vdt = _l1_proj(h, mod, True, g[1], w, diff_tab)
    kd_c, vdt_c = _l1_proj(hc, mod_c, False, g[1], w, None)
    lamv = jnp.stack([l1_lambda_q1, l1_lambda_k1, l1_lambda_q2, l1_lambda_k2]).astype(F32)
    ot = _diff_attn(lamv, l1_subln_g.reshape(-1, 1).astype(F32), qd, kd_c, vdt_c, kd, vdt)
    h = _l1_out(h, mod, ot, w["w_out"])
    return _ffn(h, mod, True, g[2], *f1, k=2)
```

```python
import functools
import math

import numpy as np
import jax
import jax.numpy as jnp
from jax import lax
from jax.experimental import pallas as pl
from jax.experimental.pallas import tpu as pltpu

F32 = jnp.float32
BF16 = jnp.bfloat16

D_MODEL = 1024
D_FF = 2816
N_MOD = 9
EPS = 1e-6
ROPE_BASE = 10000.0
GRID_W = 64
LOG2E = 1.4426950408889634

MLA_HEADS = 8
MLA_Q_RANK = 256
MLA_KV_RANK = 128
MLA_NOPE = 64
MLA_ROPE = 32
MLA_V = 64
SWA_HEADS = 8
SWA_KV_HEADS = 2
SWA_GROUP = SWA_HEADS // SWA_KV_HEADS
SWA_HEAD_DIM = 64
SWA_WINDOW = 128
DIFF_HEADS = 8
DIFF_HEAD_DIM = 64
LAMBDA_INIT_L1 = 0.8 - 0.6 * math.exp(-0.3 * 1)

LANES = 128
SUBLANES = 8
HALF = LANES // 2
ATT_TQ = 256
SWA_BLK = 128
TOK_TILE = 512
ATT_HEADS_PER_STEP = 4
VMEM_LIMIT = 56 * 1024 * 1024
MASKED = -1e30

NT_DIMS = (((1,), (1,)), ((), ()))
TN_DIMS = (((0,), (0,)), ((), ()))


def _mla_lane_map():
    m = np.full(LANES, -1, np.int64)
    m[0:16] = 64 + np.arange(16)
    m[16:48] = np.arange(32)
    m[64:80] = 80 + np.arange(16)
    m[80:112] = 32 + np.arange(32)
    return m


def _swa_lane_map():
    m = np.full(LANES, -1, np.int64)
    m[0:32] = np.arange(32)
    m[64:96] = 32 + np.arange(32)
    return m


def _diff_lane_map():
    m = np.zeros(LANES, np.int64)
    m[0:32] = np.arange(32)
    m[32:64] = 64 + np.arange(32)
    m[64:96] = 32 + np.arange(32)
    m[96:128] = 96 + np.arange(32)
    return m


MLA_ROPE_ROWS = MLA_ROPE // 2
SWA_ROPE_ROWS = SWA_HEAD_DIM // 2
DIFF_ROPE_ROWS = HALF


def _take_cols(w, idx):
    idx = np.asarray(idx)
    out = jnp.take(w, jnp.asarray(np.where(idx < 0, 0, idx), jnp.int32), axis=-1)
    return jnp.where(jnp.asarray(idx >= 0), out, jnp.zeros((), w.dtype))


def _per_head(lane_map, n_heads, stride, base=0):
    cols = [np.where(lane_map >= 0, base + h * stride + lane_map, -1) for h in range(n_heads)]
    return np.concatenate(cols)


def _adaln(x, g, shift, scale):
    ms = jnp.mean(x * x, axis=-1, keepdims=True)
    return x * lax.rsqrt(ms + EPS) * g * (1.0 + scale) + shift


def _rms(t, g, inv_dim):
    ms = jnp.sum(t * t, axis=-1, keepdims=True) * inv_dim
    return t * lax.rsqrt(ms + EPS) * g


def _rms_t(t, g, inv_dim):
    ms = jnp.sum(t * t, axis=0, keepdims=True) * inv_dim
    return t * lax.rsqrt(ms + EPS) * g


def _rope_t(t, cos, sin, rows):
    a, b = t[0:rows], t[HALF:HALF + rows]
    a2 = a * cos[0:rows] + b * sin[0:rows]
    b2 = b * cos[HALF:HALF + rows] + a * sin[HALF:HALF + rows]
    if rows == HALF:
        return jnp.concatenate([a2, b2], axis=0)
    return jnp.concatenate([a2, t[rows:HALF], b2, t[HALF + rows:]], axis=0)


def _dot(a, b):
    return jnp.dot(a, b, preferred_element_type=F32)


def _dot_nt(a, b):
    return lax.dot_general(a, b, NT_DIMS, preferred_element_type=F32)


def _dot_tn(a, b):
    return lax.dot_general(a, b, TN_DIMS, preferred_element_type=F32)


def _params(n_axes):
    return pltpu.CompilerParams(dimension_semantics=("arbitrary",) * n_axes,
                                vmem_limit_bytes=VMEM_LIMIT)


def _const_spec(shape):
    nd = len(shape)
    return pl.BlockSpec(shape, lambda *_: (0,) * nd, pipeline_mode=pl.Buffered(1))


def _mod_spec(per_batch):
    if per_batch:
        return pl.BlockSpec((None, N_MOD, D_MODEL), lambda b, t: (b, 0, 0))
    return pl.BlockSpec((None, N_MOD, D_MODEL), lambda b, t: (0, 0, 0))


def _tok_spec(tm):
    return pl.BlockSpec((None, tm, D_MODEL), lambda b, t: (b, t, 0))


def _ada_kernel(c_ref, w_ref, b_ref, o_ref):
    c = c_ref[...]
    a = (c * jax.nn.sigmoid(c)).astype(BF16)
    o_ref[...] = _dot(a, w_ref[...].astype(BF16)) + b_ref[...]


def _ada(cc, w, b):
    rows = cc.shape[0]
    n = w.shape[1]
    tn = 1152
    return pl.pallas_call(
        _ada_kernel,
        grid=(n // tn,),
        in_specs=[pl.BlockSpec((rows, D_MODEL), lambda j: (0, 0)),
                  pl.BlockSpec((D_MODEL, tn), lambda j: (0, j)),
                  pl.BlockSpec((1, tn), lambda j: (0, j))],
        out_specs=pl.BlockSpec((rows, tn), lambda j: (0, j)),
        out_shape=jax.ShapeDtypeStruct((rows, n), F32),
        compiler_params=_params(1),
        name="ada_mod",
    )(cc, w, b.reshape(1, n))


def _rope_table_kernel(inv_ref, userow_ref, sign_ref, cos_ref, sin_ref):
    t = lax.broadcasted_iota(jnp.int32, cos_ref.shape, 1)
    row = lax.shift_right_logical(t, GRID_W.bit_length() - 1).astype(F32)
    col = (t & (GRID_W - 1)).astype(F32)
    pos = jnp.where(userow_ref[...] > 0.5, row, col)
    ang = pos * inv_ref[...]
    sign = sign_ref[...]
    active = sign != 0.0
    cos_ref[...] = jnp.where(active, jnp.cos(ang), 1.0)
    sin_ref[...] = jnp.where(active, sign * jnp.sin(ang), 0.0)


def _rope_tables(seq, rot_dim, x1_slots, x2_slots):
    n_f = rot_dim // 4
    half = rot_dim // 2
    inv_f = (ROPE_BASE ** (-np.arange(n_f, dtype=np.float64) / n_f)).astype(np.float32)
    inv_half = np.concatenate([inv_f, inv_f])
    use_row_half = np.concatenate([np.ones(n_f), np.zeros(n_f)])
    inv = np.zeros(LANES, np.float32)
    use_row = np.zeros(LANES, np.float32)
    sign = np.zeros(LANES, np.float32)
    for slots, sgn in ((x1_slots, -1.0), (x2_slots, 1.0)):
        for start, n in slots:
            assert n == half
            inv[start:start + n] = inv_half
            use_row[start:start + n] = use_row_half
            sign[start:start + n] = sgn
    vec_shape = (LANES, 1)
    tab_shape = (LANES, seq)
    vec = pl.BlockSpec(vec_shape, lambda: (0, 0))
    tab = pl.BlockSpec(tab_shape, lambda: (0, 0))
    return pl.pallas_call(
        _rope_table_kernel,
        in_specs=[vec, vec, vec],
        out_specs=[tab, tab],
        out_shape=[jax.ShapeDtypeStruct(tab_shape, F32)] * 2,
        name="rope_table",
    )(*[jnp.asarray(v.reshape(vec_shape)) for v in (inv, use_row, sign)])


def _ffn_body(x, mod_ref, g, wg_ref, wu_ref, wd_ref, k):
    shift = mod_ref[3 * k:3 * k + 1, :]
    scale = mod_ref[3 * k + 1:3 * k + 2, :]
    gate = mod_ref[3 * k + 2:3 * k + 3, :]
    hn = _adaln(x, g, shift, scale).astype(BF16)
    gg = _dot(hn, wg_ref[...])
    u = _dot(hn, wu_ref[...])
    a = (gg * jax.nn.sigmoid(gg) * u).astype(BF16)
    return x + (0.5 * gate) * _dot(a, wd_ref[...])


def _ffn_kernel(x_ref, mod_ref, g_ref, wg_ref, wu_ref, wd_ref, o_ref, *, k):
    o_ref[...] = _ffn_body(x_ref[...], mod_ref, g_ref[...], wg_ref, wu_ref, wd_ref, k)


def _ffn(x, mod, per_batch, g, wg, wu, wd, k):
    bsz, n, _ = x.shape
    tm = min(TOK_TILE, n)
    return pl.pallas_call(
        functools.partial(_ffn_kernel, k=k),
        grid=(bsz, n // tm),
        in_specs=[_tok_spec(tm), _mod_spec(per_batch), _const_spec((1, D_MODEL)),
                  _const_spec((D_MODEL, D_FF)), _const_spec((D_MODEL, D_FF)),
                  _const_spec((D_FF, D_MODEL))],
        out_specs=_tok_spec(tm),
        out_shape=jax.ShapeDtypeStruct(x.shape, F32),
        compiler_params=_params(2),
        name="ffn",
    )(x, mod, g, wg, wu, wd)


def _out_ffn_kernel(*refs, n_attn):
    h_ref, mod_ref, g_ref = refs[:3]
    attn = refs[3:3 + 2 * n_attn]
    wg_ref, wu_ref, wd_ref, o_ref = refs[3 + 2 * n_attn:]
    y = None
    for a in range(n_attn):
        ot_ref, w_ref = attn[2 * a], attn[2 * a + 1]
        n_t, nh, dv, tq = ot_ref.shape
        rows = [_dot_tn(ot_ref[j].reshape(nh * dv, tq), w_ref[...]) for j in range(n_t)]
        ya = rows[0] if n_t == 1 else jnp.concatenate(rows, axis=0)
        y = ya if y is None else y + ya
    x = h_ref[...] + mod_ref[5:6, :] * y
    o_ref[...] = _ffn_body(x, mod_ref, g_ref[...], wg_ref, wu_ref, wd_ref, 2)


def _out_ffn(h, mod, per_batch, g, attn, wg, wu, wd):
    bsz, n, _ = h.shape
    tm = min(TOK_TILE, n)
    in_specs = [_tok_spec(tm), _mod_spec(per_batch), _const_spec((1, D_MODEL))]
    args = [h, mod, g]
    for o_t, w in attn:
        _, _, nh, dv, tq = o_t.shape
        in_specs += [pl.BlockSpec((None, tm // tq, nh, dv, tq), lambda b, t: (b, t, 0, 0, 0)),
                     _const_spec(w.shape)]
        args += [o_t, w]
    in_specs += [_const_spec((D_MODEL, D_FF)), _const_spec((D_MODEL, D_FF)), _const_spec((D_FF, D_MODEL))]
    args += [wg, wu, wd]
    return pl.pallas_call(
        functools.partial(_out_ffn_kernel, n_attn=len(attn)),
        grid=(bsz, n // tm),
        in_specs=in_specs,
        out_specs=_tok_spec(tm),
        out_shape=jax.ShapeDtypeStruct(h.shape, F32),
        compiler_params=_params(2),
        name="out_ffn",
    )(*args)


L0_ST_SQ = 0
L0_ST_SV = L0_ST_SQ + SWA_HEADS * LANES
L0_ST_KR = L0_ST_SV + SWA_KV_HEADS * SWA_HEAD_DIM
L0_ST_SK = L0_ST_KR + LANES
L0_ST_ROWS = L0_ST_SK + SWA_KV_HEADS * LANES


def _l0_proj_kernel(*refs, rope):
    (h_ref, mod_ref, g_ref, w0_ref, wst_ref, qag_ref, wqbt_ref, kvag_ref, wkvt_ref,
     mqg_ref, mkg_ref, sqg_ref, skg_ref) = refs[:13]
    if rope:
        cmt, smt, cst, sst = [r[...] for r in refs[13:17]]
        outs = refs[17:]
    else:
        outs = refs[13:]
    qm_ref, km_ref, vmt_ref, qs_ref, ks_ref, vst_ref = outs

    hn = _adaln(h_ref[...], g_ref[...], mod_ref[3:4, :], mod_ref[4:5, :]).astype(BF16)
    tm = hn.shape[0]
    p = _dot(hn, w0_ref[...])
    st = _dot_nt(wst_ref[...], hn)

    mla_inv = 1.0 / (MLA_NOPE + MLA_ROPE)
    swa_inv = 1.0 / SWA_HEAD_DIM
    bcast = lambda v: jnp.broadcast_to(v, (LANES, tm))
    mqg = bcast(mqg_ref[...] * (mla_inv ** 0.5 * LOG2E))
    sqg = bcast(sqg_ref[...] * (swa_inv ** 0.5 * LOG2E))
    mkg = bcast(mkg_ref[...])
    skg = bcast(skg_ref[...])

    def mla_rope(t):
        return _rope_t(t, cmt, smt, MLA_ROPE_ROWS) if rope else t

    def swa_rope(t):
        return _rope_t(t, cst, sst, SWA_ROPE_ROWS) if rope else t

    qa = _rms(p[:, 0:MLA_Q_RANK], qag_ref[...], 1.0 / MLA_Q_RANK).astype(BF16)
    qt = _dot_nt(wqbt_ref[...], qa)
    for h in range(MLA_HEADS):
        t = mla_rope(_rms_t(qt[h * LANES:(h + 1) * LANES], mqg, mla_inv))
        for j in range(tm // ATT_TQ):
            qm_ref[h, j] = t[:, j * ATT_TQ:(j + 1) * ATT_TQ].astype(BF16)

    kva = _rms(p[:, MLA_Q_RANK:MLA_Q_RANK + MLA_KV_RANK], kvag_ref[...], 1.0 / MLA_KV_RANK).astype(BF16)
    kvt = _dot_nt(wkvt_ref[...], kva)
    kr = st[L0_ST_KR:L0_ST_KR + LANES]
    for h in range(MLA_HEADS):
        t = mla_rope(_rms_t(kvt[h * LANES:(h + 1) * LANES] + kr, mkg, mla_inv))
        km_ref[h] = t.T.astype(BF16)
    v0 = MLA_HEADS * LANES
    for h in range(MLA_HEADS):
        vmt_ref[h] = kvt[v0 + h * MLA_V:v0 + (h + 1) * MLA_V, :].astype(BF16)

    for h in range(SWA_HEADS):
        t = swa_rope(_rms_t(st[L0_ST_SQ + h * LANES:L0_ST_SQ + (h + 1) * LANES], sqg, swa_inv))
        for i in range(tm // SWA_BLK):
            qs_ref[h, i] = t[:, i * SWA_BLK:(i + 1) * SWA_BLK].astype(BF16)
    for j in range(SWA_KV_HEADS):
        t = swa_rope(_rms_t(st[L0_ST_SK + j * LANES:L0_ST_SK + (j + 1) * LANES], skg, swa_inv))
        ks_ref[j] = t.T.astype(BF16)
        for i in range(tm // SWA_BLK):
            vst_ref[j, i] = st[L0_ST_SV + j * SWA_HEAD_DIM:L0_ST_SV + (j + 1) * SWA_HEAD_DIM,
                               i * SWA_BLK:(i + 1) * SWA_BLK].astype(BF16)


def _l0_proj(h, mod, per_batch, g, w, tables):
    bsz, n, _ = h.shape
    tm = min(TOK_TILE, n)
    rope = tables is not None
    weights = [g, w["w0"], w["wst"], w["qa_g"], w["wqbt"], w["kva_g"], w["wkvt"],
               w["mq_g"], w["mk_g"], w["sq_g"], w["sk_g"]]
    in_specs = [_tok_spec(tm), _mod_spec(per_batch)] + [_const_spec(a.shape) for a in weights]
    args = [h, mod] + weights
    if rope:
        in_specs += [pl.BlockSpec((LANES, tm), lambda b, t: (0, t))] * 4
        args += list(tables)
    k_spec = lambda nh: pl.BlockSpec((None, nh, tm, LANES), lambda b, t: (b, 0, t, 0))
    qt_spec = lambda tq: pl.BlockSpec((None, MLA_HEADS, tm // tq, LANES, tq), lambda b, t: (b, 0, t, 0, 0))
    out_specs = [qt_spec(ATT_TQ), k_spec(MLA_HEADS),
                 pl.BlockSpec((None, MLA_HEADS, MLA_V, tm), lambda b, t: (b, 0, 0, t)),
                 qt_spec(SWA_BLK), k_spec(SWA_KV_HEADS),
                 pl.BlockSpec((None, SWA_KV_HEADS, tm // SWA_BLK, SWA_HEAD_DIM, SWA_BLK),
                              lambda b, t: (b, 0, t, 0, 0))]
    out_shape = [jax.ShapeDtypeStruct((bsz, MLA_HEADS, n // ATT_TQ, LANES, ATT_TQ), BF16),
                 jax.ShapeDtypeStruct((bsz, MLA_HEADS, n, LANES), BF16),
                 jax.ShapeDtypeStruct((bsz, MLA_HEADS, MLA_V, n), BF16),
                 jax.ShapeDtypeStruct((bsz, SWA_HEADS, n // SWA_BLK, LANES, SWA_BLK), BF16),
                 jax.ShapeDtypeStruct((bsz, SWA_KV_HEADS, n, LANES), BF16),
                 jax.ShapeDtypeStruct((bsz, SWA_KV_HEADS, n // SWA_BLK, SWA_HEAD_DIM, SWA_BLK), BF16)]
    return pl.pallas_call(
        functools.partial(_l0_proj_kernel, rope=rope),
        grid=(bsz, n // tm),
        in_specs=in_specs, out_specs=out_specs, out_shape=out_shape,
        compiler_params=_params(2),
        name="l0_proj",
    )(*args)


def _two_stage_loop(n_items, produce, consume, slot0, slot1):
    produce(0, slot0)

    def pair(j, carry):
        i = 2 * j
        produce(i + 1, slot1)
        consume(i, slot0)
        produce(jnp.minimum(i + 2, n_items - 1), slot0)
        consume(i + 1, slot1)
        return carry

    lax.fori_loop(0, n_items // 2, pair, 0)


def _chunk_fold(x, op):
    return op(x.reshape(x.shape[0] // SUBLANES, SUBLANES, x.shape[-1]), axis=0)


def _column_max(s_ref, chunk):
    mx = None
    for k0 in range(0, s_ref.shape[0], chunk):
        c = _chunk_fold(s_ref[k0:k0 + chunk, :], jnp.max)
        mx = c if mx is None else jnp.maximum(mx, c)
    return jnp.max(mx, axis=0, keepdims=True)


def _exp_pv(s_ref, m, value_t, chunk):
    l8 = None
    o = None
    for k0 in range(0, s_ref.shape[0], chunk):
        e = jnp.exp2(s_ref[k0:k0 + chunk, :] - m)
        part = _chunk_fold(e, jnp.sum)
        l8 = part if l8 is None else l8 + part
        d = _dot(value_t(k0), e.astype(BF16))
        o = d if o is None else o + d
    return o, jnp.sum(l8, axis=0, keepdims=True)


KEY_CHUNK = 256


def _ctx_then_latent(vct_ref, vlt_ref, h, n_ctx):
    def value_t(k0):
        if k0 < n_ctx:
            return vct_ref[h, :, k0:k0 + KEY_CHUNK]
        return vlt_ref[h, :, k0 - n_ctx:k0 - n_ctx + KEY_CHUNK]
    return value_t


def _mla_attn_kernel(q_ref, kc_ref, vct_ref, kl_ref, vlt_ref, o_ref, s0_ref, m0_ref, s1_ref, m1_ref):
    n_tiles, n_heads = o_ref.shape[:2]
    n_ctx = kc_ref.shape[1]

    def split(i):
        return lax.div(i, n_tiles), lax.rem(i, n_tiles)

    def scores(i, slot):
        s_ref, m_ref = slot
        h, t = split(i)
        qt = q_ref[h, t]
        s_ref[0:n_ctx, :] = _dot(kc_ref[h], qt)
        s_ref[n_ctx:, :] = _dot(kl_ref[h], qt)
        m_ref[...] = _column_max(s_ref, KEY_CHUNK)

    def softmax_pv(i, slot):
        s_ref, m_ref = slot
        h, t = split(i)
        o, l = _exp_pv(s_ref, m_ref[...], _ctx_then_latent(vct_ref, vlt_ref, h, n_ctx), KEY_CHUNK)
        o_ref[t, h] = (o * (1.0 / l)).astype(BF16)

    _two_stage_loop(n_tiles * n_heads, scores, softmax_pv, (s0_ref, m0_ref), (s1_ref, m1_ref))


def _mla_attn(q, kc, vct, kl, vlt):
    bsz, nh, n_tiles = q.shape[:3]
    hs = ATT_HEADS_PER_STEP
    n_keys = kc.shape[2] + kl.shape[2]
    grp = lambda shape: pl.BlockSpec((None, hs) + tuple(shape), lambda b, g: (b, g) + (0,) * len(shape))
    slot = [pltpu.VMEM((n_keys, ATT_TQ), F32), pltpu.VMEM((1, ATT_TQ), F32)]
    return pl.pallas_call(
        _mla_attn_kernel,
        grid=(bsz, nh // hs),
        in_specs=[grp(q.shape[2:]), grp(kc.shape[2:]), grp(vct.shape[2:]), grp(kl.shape[2:]), grp(vlt.shape[2:])],
        out_specs=pl.BlockSpec((None, n_tiles, hs, MLA_V, ATT_TQ), lambda b, g: (b, 0, g, 0, 0)),
        out_shape=jax.ShapeDtypeStruct((bsz, n_tiles, nh, MLA_V, ATT_TQ), BF16),
        scratch_shapes=slot + slot,
        compiler_params=_params(2),
        name="mla_attn",
    )(q, kc, vct, kl, vlt)


def _mla_ctx_attn_kernel(q_ref, kc_ref, vct_ref, o_ref):
    for h in range(o_ref.shape[0]):
        sc = _dot(kc_ref[h], q_ref[h])
        pc = jnp.exp2(sc - jnp.max(sc, axis=0, keepdims=True))
        l = jnp.sum(pc, axis=0, keepdims=True)
        o = _dot(vct_ref[h], pc.astype(BF16))
        o_ref[h] = (o * (1.0 / l)).astype(BF16)


def _mla_ctx_attn(q, kc, vct):
    bsz, nh, n_tiles = q.shape[:3]
    assert n_tiles == 1
    per_b = lambda shape: pl.BlockSpec((None,) + tuple(shape), lambda b: (b,) + (0,) * len(shape))
    return pl.pallas_call(
        _mla_ctx_attn_kernel,
        grid=(bsz,),
        in_specs=[pl.BlockSpec((None, nh, None, LANES, ATT_TQ), lambda b: (b, 0, 0, 0, 0)),
                  per_b(kc.shape[1:]), per_b(vct.shape[1:])],
        out_specs=pl.BlockSpec((None, None, nh, MLA_V, ATT_TQ), lambda b: (b, 0, 0, 0, 0)),
        out_shape=jax.ShapeDtypeStruct((bsz, 1, nh, MLA_V, ATT_TQ), BF16),
        compiler_params=_params(1),
        name="mla_ctx_attn",
    )(q, kc, vct)


SWA_BAND = 3 * SWA_BLK
SWA_NQ = SWA_GROUP * SWA_BLK


def _sink_row(sink_ref, j):
    return jnp.concatenate(
        [jnp.full((1, SWA_BLK), sink_ref[j * SWA_GROUP + g] * LOG2E, F32) for g in range(SWA_GROUP)], axis=1)


def _swa_attn_kernel(sink_ref, q_ref, kc_ref, vct_ref, kl_ref, vlt_ref, o_ref,
                     bias_ref, s0_ref, m0_ref, s1_ref, m1_ref):
    n_blk = o_ref.shape[0]
    n_ctx = kc_ref.shape[1]

    @pl.when(pl.program_id(0) == 0)
    def _():
        r = lax.broadcasted_iota(jnp.int32, (SWA_BAND, SWA_NQ), 0)
        c = lax.broadcasted_iota(jnp.int32, (SWA_BAND, SWA_NQ), 1) & (SWA_BLK - 1)
        for case in range(3):
            dist = r - c - case * SWA_BLK
            bias_ref[case] = jnp.where(jnp.abs(dist) <= SWA_WINDOW, 0.0, MASKED)

    def split(i):
        return lax.div(i, n_blk), lax.rem(i, n_blk)

    def band_start(n):
        return jnp.clip(n - 1, 0, n_blk - 3)

    def scores(i, slot):
        s_ref, m_ref = slot
        j, n = split(i)
        qt = jnp.concatenate([q_ref[j * SWA_GROUP + g, n] for g in range(SWA_GROUP)], axis=1)
        s_ref[0:n_ctx, :] = _dot(kc_ref[j], qt)
        k0 = pl.multiple_of(band_start(n) * SWA_BLK, SWA_BLK)
        case = (n > 0).astype(jnp.int32) + (n == n_blk - 1).astype(jnp.int32)
        s_ref[n_ctx:, :] = _dot(kl_ref[j, pl.ds(k0, SWA_BAND), :], qt) + bias_ref[case]
        m_ref[...] = jnp.maximum(_column_max(s_ref, SWA_BLK), _sink_row(sink_ref, j))

    def softmax_pv(i, slot):
        s_ref, m_ref = slot
        j, n = split(i)
        b0 = band_start(n)
        m = m_ref[...]

        def value_t(k0):
            if k0 < n_ctx:
                return vct_ref[j, :, k0:k0 + SWA_BLK]
            return vlt_ref[j, b0 + (k0 - n_ctx) // SWA_BLK]

        o, l = _exp_pv(s_ref, m, value_t, SWA_BLK)
        o = o * (1.0 / (l + jnp.exp2(_sink_row(sink_ref, j) - m)))
        for g in range(SWA_GROUP):
            o_ref[n, j * SWA_GROUP + g] = o[:, g * SWA_BLK:(g + 1) * SWA_BLK].astype(BF16)

    _two_stage_loop(SWA_KV_HEADS * n_blk, scores, softmax_pv, (s0_ref, m0_ref), (s1_ref, m1_ref))


def _swa_attn(sink, q, kc, vct, kl, vlt):
    bsz, nh, n_blk = q.shape[:3]
    n_keys = kc.shape[2] + SWA_BAND
    full = lambda a: pl.BlockSpec((None,) + a.shape[1:], lambda b: (b,) + (0,) * (a.ndim - 1))
    slot = [pltpu.VMEM((n_keys, SWA_NQ), F32), pltpu.VMEM((1, SWA_NQ), F32)]
    return pl.pallas_call(
        _swa_attn_kernel,
        grid=(bsz,),
        in_specs=[pl.BlockSpec(memory_space=pltpu.SMEM), full(q), full(kc), full(vct), full(kl), full(vlt)],
        out_specs=pl.BlockSpec((None, n_blk, nh, SWA_HEAD_DIM, SWA_BLK), lambda b: (b, 0, 0, 0, 0)),
        out_shape=jax.ShapeDtypeStruct((bsz, n_blk, nh, SWA_HEAD_DIM, SWA_BLK), BF16),
        scratch_shapes=[pltpu.VMEM((3, SWA_BAND, SWA_NQ), F32)] + slot + slot,
        compiler_params=_params(1),
        name="swa_attn",
    )(sink, q, kc, vct, kl, vlt)


def _swa_ctx_attn_kernel(sink_ref, q_ref, kc_ref, vct_ref, o_ref):
    sink = _sink_row(sink_ref, pl.program_id(1))
    for n in range(o_ref.shape[0]):
        qt = jnp.concatenate([q_ref[g, n] for g in range(SWA_GROUP)], axis=1)
        s = _dot(kc_ref[...], qt)
        m = jnp.maximum(jnp.max(s, axis=0, keepdims=True), sink)
        p = jnp.exp2(s - m)
        l = jnp.sum(p, axis=0, keepdims=True) + jnp.exp2(sink - m)
        o = _dot(vct_ref[...], p.astype(BF16)) * (1.0 / l)
        for g in range(SWA_GROUP):
            o_ref[n, g] = o[:, g * SWA_BLK:(g + 1) * SWA_BLK].astype(BF16)


def _swa_ctx_attn(sink, q, kc, vct):
    bsz, nh, n_blk = q.shape[:3]
    return pl.pallas_call(
        _swa_ctx_attn_kernel,
        grid=(bsz, SWA_KV_HEADS),
        in_specs=[pl.BlockSpec(memory_space=pltpu.SMEM),
                  pl.BlockSpec((None, SWA_GROUP, n_blk, LANES, SWA_BLK), lambda b, j: (b, j, 0, 0, 0)),
                  pl.BlockSpec((None, None) + kc.shape[2:], lambda b, j: (b, j, 0, 0)),
                  pl.BlockSpec((None, None) + vct.shape[2:], lambda b, j: (b, j, 0, 0))],
        out_specs=pl.BlockSpec((None, n_blk, SWA_GROUP, SWA_HEAD_DIM, SWA_BLK), lambda b, j: (b, 0, j, 0, 0)),
        out_shape=jax.ShapeDtypeStruct((bsz, n_blk, nh, SWA_HEAD_DIM, SWA_BLK), BF16),
        compiler_params=_params(2),
        name="swa_ctx_attn",
    )(sink, q, kc, vct)


DIFF_Q = HALF // 2


def _diff_norm_t(t, g):
    sq = t * t
    grp = [jnp.sum(sq[i * DIFF_Q:(i + 1) * DIFF_Q], axis=0, keepdims=True) for i in range(4)]
    inv = 1.0 / DIFF_HEAD_DIM
    r0 = lax.rsqrt((grp[0] + grp[2]) * inv + EPS)
    r1 = lax.rsqrt((grp[1] + grp[3]) * inv + EPS)
    parts = [t[i * DIFF_Q:(i + 1) * DIFF_Q] * (r0 if i % 2 == 0 else r1) for i in range(4)]
    return jnp.concatenate(parts, axis=0) * g


def _l1_proj_kernel(*refs, latent):
    if latent:
        h_ref, mod_ref, g_ref, wt_ref, qg_ref, kg_ref, cdt_ref, sdt_ref, q_ref, k_ref, vt_ref = refs
        cdt, sdt = cdt_ref[...], sdt_ref[...]
    else:
        h_ref, mod_ref, g_ref, wt_ref, kg_ref, k_ref, vt_ref = refs
    hn = _adaln(h_ref[...], g_ref[...], mod_ref[3:4, :], mod_ref[4:5, :]).astype(BF16)
    tm = hn.shape[0]
    width = DIFF_HEADS * LANES
    ft = _dot_nt(wt_ref[...], hn)
    bcast = lambda v: jnp.broadcast_to(v, (LANES, tm))
    k0 = 0
    if latent:
        qg = bcast(qg_ref[...] * (DIFF_HEAD_DIM ** -0.5 * LOG2E))
        for h in range(DIFF_HEADS):
            t = _rope_t(_diff_norm_t(ft[h * LANES:(h + 1) * LANES], qg), cdt, sdt, DIFF_ROPE_ROWS)
            for j in range(tm // ATT_TQ):
                q_ref[h, j] = t[:, j * ATT_TQ:(j + 1) * ATT_TQ].astype(BF16)
        k0 = width

    lane = lax.broadcasted_iota(jnp.int32, (tm, LANES), 1)
    mask0 = (lane & (HALF - 1)) < DIFF_Q
    kg = bcast(kg_ref[...])
    for h in range(DIFF_HEADS):
        t = _diff_norm_t(ft[k0 + h * LANES:k0 + (h + 1) * LANES], kg)
        if latent:
            t = _rope_t(t, cdt, sdt, DIFF_ROPE_ROWS)
        t = t.T
        k_ref[h, 0] = jnp.where(mask0, t, 0.0).astype(BF16)
        k_ref[h, 1] = jnp.where(mask0, 0.0, t).astype(BF16)
    v0 = k0 + width
    for h in range(DIFF_HEADS):
        vt_ref[h] = ft[v0 + h * LANES:v0 + (h + 1) * LANES, :].astype(BF16)


def _l1_proj(h, mod, per_batch, g, w, tables):
    bsz, n, _ = h.shape
    tm = min(TOK_TILE, n)
    latent = tables is not None
    k_spec = pl.BlockSpec((None, DIFF_HEADS, 2, tm, LANES), lambda b, t: (b, 0, 0, t, 0))
    vt_spec = pl.BlockSpec((None, DIFF_HEADS, LANES, tm), lambda b, t: (b, 0, 0, t))
    k_shape = jax.ShapeDtypeStruct((bsz, DIFF_HEADS, 2, n, LANES), BF16)
    vt_shape = jax.ShapeDtypeStruct((bsz, DIFF_HEADS, LANES, n), BF16)
    if latent:
        weights = [g, w["wqkvt"], w["q_g"], w["k_g"]]
        in_specs = ([_tok_spec(tm), _mod_spec(per_batch)] + [_const_spec(a.shape) for a in weights]
                    + [pl.BlockSpec((LANES, tm), lambda b, t: (0, t))] * 2)
        args = [h, mod] + weights + list(tables)
        out_specs = [pl.BlockSpec((None, DIFF_HEADS, tm // ATT_TQ, LANES, ATT_TQ), lambda b, t: (b, 0, t, 0, 0)),
                     k_spec, vt_spec]
        out_shape = [jax.ShapeDtypeStruct((bsz, DIFF_HEADS, n // ATT_TQ, LANES, ATT_TQ), BF16), k_shape, vt_shape]
    else:
        weights = [g, w["wkvt"], w["k_g"]]
        in_specs = [_tok_spec(tm), _mod_spec(per_batch)] + [_const_spec(a.shape) for a in weights]
        args = [h, mod] + weights
        out_specs = [k_spec, vt_spec]
        out_shape = [k_shape, vt_shape]
    return pl.pallas_call(
        functools.partial(_l1_proj_kernel, latent=latent),
        grid=(bsz, n // tm),
        in_specs=in_specs, out_specs=out_specs, out_shape=out_shape,
        compiler_params=_params(2),
        name="l1_proj",
    )(*args)


def _diff_attn_kernel(lam_ref, subg_ref, q_ref, kc_ref, vct_ref, kl_ref, vlt_ref, o_ref,
                      s0_ref, m0_ref, s1_ref, m1_ref):
    n_tiles, n_heads = o_ref.shape[:2]
    n_ctx = kc_ref.shape[2]
    lv = lam_ref[...]
    lam = (jnp.exp(jnp.sum(lv[0:1] * lv[1:2], axis=-1, keepdims=True))
           - jnp.exp(jnp.sum(lv[2:3] * lv[3:4], axis=-1, keepdims=True)) + LAMBDA_INIT_L1)
    subg = subg_ref[...] * (1.0 - LAMBDA_INIT_L1)

    def split(i):
        return lax.div(i, n_tiles), lax.rem(i, n_tiles)

    def scores(i, slot):
        s_ref, m_ref = slot
        h, t = split(i)
        qt = q_ref[h, t]
        for s in range(2):
            s_ref[s, 0:n_ctx, :] = _dot(kc_ref[h, s], qt)
            s_ref[s, n_ctx:, :] = _dot(kl_ref[h, s], qt)
            m_ref[s] = _column_max(s_ref.at[s], KEY_CHUNK)

    def softmax_pv(i, slot):
        s_ref, m_ref = slot
        h, t = split(i)
        value_t = _ctx_then_latent(vct_ref, vlt_ref, h, n_ctx)
        part = []
        for s in range(2):
            o, l = _exp_pv(s_ref.at[s], m_ref[s], value_t, KEY_CHUNK)
            part.append((o, 1.0 / l))
        o = part[0][0] * part[0][1] - part[1][0] * (lam * part[1][1])
        ms = jnp.mean(o * o, axis=0, keepdims=True)
        o_ref[t, h] = (o * lax.rsqrt(ms + EPS) * subg).astype(BF16)

    _two_stage_loop(n_tiles * n_heads, scores, softmax_pv, (s0_ref, m0_ref), (s1_ref, m1_ref))


def _diff_attn(lamv, subg, q, kc, vct, kl, vlt):
    bsz, nh, n_tiles = q.shape[:3]
    hs = ATT_HEADS_PER_STEP
    n_keys = kc.shape[3] + kl.shape[3]
    grp = lambda shape: pl.BlockSpec((None, hs) + tuple(shape), lambda b, g: (b, g) + (0,) * len(shape))
    slot = [pltpu.VMEM((2, n_keys, ATT_TQ), F32), pltpu.VMEM((2, 1, ATT_TQ), F32)]
    return pl.pallas_call(
        _diff_attn_kernel,
        grid=(bsz, nh // hs),
        in_specs=[_const_spec(lamv.shape), _const_spec(subg.shape),
                  grp(q.shape[2:]), grp(kc.shape[2:]), grp(vct.shape[2:]), grp(kl.shape[2:]), grp(vlt.shape[2:])],
        out_specs=pl.BlockSpec((None, n_tiles, hs, LANES, ATT_TQ), lambda b, g: (b, 0, g, 0, 0)),
        out_shape=jax.ShapeDtypeStruct((bsz, n_tiles, nh, LANES, ATT_TQ), BF16),
        scratch_shapes=slot + slot,
        compiler_params=_params(2),
        name="diff_attn",
    )(lamv, subg, q, kc, vct, kl, vlt)


def _prep_l0(w_in, qa_g, wqb, kva_g, wkvb, q_g, k_g, sq_g, sk_g, w_out):
    mla = _mla_lane_map()
    swa = _swa_lane_map()
    kr_cols = np.where(mla >= MLA_NOPE, 384 + mla - MLA_NOPE, -1)
    wst = jnp.concatenate([
        _take_cols(w_in, _per_head(swa, SWA_HEADS, SWA_HEAD_DIM, base=416)),
        w_in[:, 1056:1184],
        _take_cols(w_in, kr_cols),
        _take_cols(w_in, _per_head(swa, SWA_KV_HEADS, SWA_HEAD_DIM, base=928)),
    ], axis=1).T
    assert wst.shape[0] == L0_ST_ROWS
    nope_map = np.where((mla >= 0) & (mla < MLA_NOPE), mla, -1)
    v_cols = np.concatenate([h * (MLA_NOPE + MLA_V) + MLA_NOPE + np.arange(MLA_V) for h in range(MLA_HEADS)])
    wkvt = jnp.concatenate([
        _take_cols(wkvb, _per_head(nope_map, MLA_HEADS, MLA_NOPE + MLA_V)),
        jnp.take(wkvb, jnp.asarray(v_cols, jnp.int32), axis=1),
    ], axis=1).T
    row = lambda v: v.reshape(1, -1).astype(F32)
    col = lambda v: v.reshape(-1, 1).astype(F32)
    return {
        "w0": w_in[:, 0:MLA_Q_RANK + MLA_KV_RANK].astype(BF16),
        "wst": wst.astype(BF16),
        "qa_g": row(qa_g),
        "wqbt": _take_cols(wqb, _per_head(mla, MLA_HEADS, MLA_NOPE + MLA_ROPE)).T.astype(BF16),
        "kva_g": row(kva_g),
        "wkvt": wkvt.astype(BF16),
        "mq_g": col(_take_cols(q_g, mla)),
        "mk_g": col(_take_cols(k_g, mla)),
        "sq_g": col(_take_cols(sq_g, swa)),
        "sk_g": col(_take_cols(sk_g, swa)),
        "wa": w_out[:MLA_HEADS * MLA_V].astype(BF16),
        "wb": w_out[MLA_HEADS * MLA_V:].astype(BF16),
    }


def _prep_l1(w_in, q_g, k_g, w_out):
    dm = _diff_lane_map()
    width = DIFF_HEADS * LANES
    wqt = _take_cols(w_in, _per_head(dm, DIFF_HEADS, LANES)).T
    wkt = _take_cols(w_in, _per_head(dm, DIFF_HEADS, LANES, base=width)).T
    wvt = w_in[:, 2 * width:].T
    return {
        "wqkvt": jnp.concatenate([wqt, wkt, wvt], axis=0).astype(BF16),
        "wkvt": jnp.concatenate([wkt, wvt], axis=0).astype(BF16),
        "q_g": _take_cols(q_g, dm % DIFF_HEAD_DIM).reshape(-1, 1).astype(F32),
        "k_g": _take_cols(k_g, dm % DIFF_HEAD_DIM).reshape(-1, 1).astype(F32),
        "w_out": w_out.astype(BF16),
    }


def kernel(x, c, ctx, c_ctx, l0_ada_w, l0_ada_b, l0_norm_g, l0_ffn_wg, l0_ffn_wu, l0_ffn_wd, l0_w_in, l0_mla_qa_g, l0_mla_wqb, l0_mla_kva_g, l0_mla_wkvb, l0_mla_q_g, l0_mla_k_g, l0_swa_q_g, l0_swa_k_g, l0_swa_sink, l0_w_out, l1_ada_w, l1_ada_b, l1_norm_g, l1_ffn_wg, l1_ffn_wu, l1_ffn_wd, l1_w_in, l1_q_g, l1_k_g, l1_lambda_q1, l1_lambda_k1, l1_lambda_q2, l1_lambda_k2, l1_subln_g, l1_w_out):
    bsz, seq, _ = x.shape

    pad = (-(bsz + 1)) % SUBLANES
    cc = jnp.concatenate([c, c_ctx[None, :], jnp.zeros((pad, D_MODEL), F32)], axis=0)

    def mods(ada_w, ada_b):
        m = _ada(cc, ada_w, ada_b).reshape(cc.shape[0], N_MOD, D_MODEL)
        return m[:bsz], m[bsz:bsz + 1]

    def ffn_weights(wg, wu, wd, i):
        return wg[i].astype(BF16), wu[i].astype(BF16), wd[i].astype(BF16)

    mla_tab = _rope_tables(seq, MLA_ROPE, [(0, 16)], [(64, 16)])
    swa_tab = _rope_tables(seq, SWA_HEAD_DIM, [(0, 32)], [(64, 32)])
    diff_tab = _rope_tables(seq, DIFF_HEAD_DIM, [(0, 32), (32, 32)], [(64, 32), (96, 32)])

    mod, mod_c = mods(l0_ada_w, l0_ada_b)
    g = [l0_norm_g[i:i + 1] for i in range(3)]
    w = _prep_l0(l0_w_in, l0_mla_qa_g, l0_mla_wqb, l0_mla_kva_g, l0_mla_wkvb, l0_mla_q_g, l0_mla_k_g,
                 l0_swa_q_g, l0_swa_k_g, l0_w_out)
    f0 = ffn_weights(l0_ffn_wg, l0_ffn_wu, l0_ffn_wd, 0)
    f1 = ffn_weights(l0_ffn_wg, l0_ffn_wu, l0_ffn_wd, 1)

    h = _ffn(x, mod, True, g[0], *f0, k=0)
    hc = _ffn(ctx, mod_c, False, g[0], *f0, k=0)

    qm, km, vmt, qs, ks, vst = _l0_proj(h, mod, True, g[1], w, mla_tab + swa_tab)
    qm_c, km_c, vmt_c, qs_c, ks_c, vst_c = _l0_proj(hc, mod_c, False, g[1], w, None)
    vsc_t = vst_c.transpose(0, 1, 3, 2, 4).reshape(bsz, SWA_KV_HEADS, SWA_HEAD_DIM, -1)

    at = _mla_attn(qm, km_c, vmt_c, km, vmt)
    bt = _swa_attn(l0_swa_sink, qs, ks_c, vsc_t, ks, vst)
    at_c = _mla_ctx_attn(qm_c, km_c, vmt_c)
    bt_c = _swa_ctx_attn(l0_swa_sink, qs_c, ks_c, vsc_t)

    h = _out_ffn(h, mod, True, g[2], [(at, w["wa"]), (bt, w["wb"])], *f1)
    hc = _out_ffn(hc, mod_c, False, g[2], [(at_c, w["wa"]), (bt_c, w["wb"])], *f1)

    mod, mod_c = mods(l1_ada_w, l1_ada_b)
    g = [l1_norm_g[i:i + 1] for i in range(3)]
    w = _prep_l1(l1_w_in, l1_q_g, l1_k_g, l1_w_out)
    f0 = ffn_weights(l1_ffn_wg, l1_ffn_wu, l1_ffn_wd, 0)
    f1 = ffn_weights(l1_ffn_wg, l1_ffn_wu, l1_ffn_wd, 1)

    h = _ffn(h, mod, True, g[0], *f0, k=0)
    hc = _ffn(hc, mod_c, False, g[0], *f0, k=0)

    qd, kd, vdt = _l1_proj(h, mod, True, g[1], w, diff_tab)
    kd_c, vdt_c = _l1_proj(hc, mod_c, False, g[1], w, None)
    lamv = jnp.stack([l1_lambda_q1, l1_lambda_k1, l1_lambda_q2, l1_lambda_k2]).astype(F32)
    ot = _diff_attn(lamv, l1_subln_g.reshape(-1, 1).astype(F32), qd, kd_c, vdt_c, kd, vdt)
    return _out_ffn(h, mod, True, g[2], [(ot, w["w_out"])], *f1)
```

```python
import functools
import math

import numpy as np
import jax
import jax.numpy as jnp
from jax import lax
from jax.experimental import pallas as pl
from jax.experimental.pallas import tpu as pltpu

F32 = jnp.float32
BF16 = jnp.bfloat16

D_MODEL = 1024
D_FF = 2816
N_MOD = 9
EPS = 1e-6
ROPE_BASE = 10000.0
GRID_W = 64
LOG2E = 1.4426950408889634

MLA_HEADS = 8
MLA_Q_RANK = 256
MLA_KV_RANK = 128
MLA_NOPE = 64
MLA_ROPE = 32
MLA_V = 64
SWA_HEADS = 8
SWA_KV_HEADS = 2
SWA_GROUP = SWA_HEADS // SWA_KV_HEADS
SWA_HEAD_DIM = 64
SWA_WINDOW = 128
DIFF_HEADS = 8
DIFF_HEAD_DIM = 64
LAMBDA_INIT_L1 = 0.8 - 0.6 * math.exp(-0.3 * 1)

LANES = 128
SUBLANES = 8
HALF = LANES // 2
MLA_TQ = 512
DIFF_TQ = 256
SWA_BLK = 128
TOK_TILE = 512
ATT_HEADS_PER_STEP = 4
VMEM_LIMIT = 56 * 1024 * 1024
MASKED = -1e30

NT_DIMS = (((1,), (1,)), ((), ()))
TN_DIMS = (((0,), (0,)), ((), ()))


def _mla_lane_map():
    m = np.full(LANES, -1, np.int64)
    m[0:16] = 64 + np.arange(16)
    m[16:48] = np.arange(32)
    m[64:80] = 80 + np.arange(16)
    m[80:112] = 32 + np.arange(32)
    return m


def _swa_lane_map():
    m = np.full(LANES, -1, np.int64)
    m[0:32] = np.arange(32)
    m[64:96] = 32 + np.arange(32)
    return m


def _diff_lane_map():
    m = np.zeros(LANES, np.int64)
    m[0:32] = np.arange(32)
    m[32:64] = 64 + np.arange(32)
    m[64:96] = 32 + np.arange(32)
    m[96:128] = 96 + np.arange(32)
    return m


MLA_ROPE_ROWS = MLA_ROPE // 2
SWA_ROPE_ROWS = SWA_HEAD_DIM // 2
DIFF_ROPE_ROWS = HALF


def _take_cols(w, idx):
    idx = np.asarray(idx)
    out = jnp.take(w, jnp.asarray(np.where(idx < 0, 0, idx), jnp.int32), axis=-1)
    return jnp.where(jnp.asarray(idx >= 0), out, jnp.zeros((), w.dtype))


def _per_head(lane_map, n_heads, stride, base=0):
    cols = [np.where(lane_map >= 0, base + h * stride + lane_map, -1) for h in range(n_heads)]
    return np.concatenate(cols)


def _adaln(x, g, shift, scale):
    ms = jnp.mean(x * x, axis=-1, keepdims=True)
    return x * lax.rsqrt(ms + EPS) * g * (1.0 + scale) + shift


def _rms(t, g, inv_dim):
    ms = jnp.sum(t * t, axis=-1, keepdims=True) * inv_dim
    return t * lax.rsqrt(ms + EPS) * g


def _rms_t(t, g, inv_dim):
    ms = jnp.sum(t * t, axis=0, keepdims=True) * inv_dim
    return t * lax.rsqrt(ms + EPS) * g


def _rope_t(t, cos, sin, rows):
    a, b = t[0:rows], t[HALF:HALF + rows]
    a2 = a * cos[0:rows] + b * sin[0:rows]
    b2 = b * cos[HALF:HALF + rows] + a * sin[HALF:HALF + rows]
    if rows == HALF:
        return jnp.concatenate([a2, b2], axis=0)
    return jnp.concatenate([a2, t[rows:HALF], b2, t[HALF + rows:]], axis=0)


def _dot(a, b):
    return jnp.dot(a, b, preferred_element_type=F32)


def _dot_nt(a, b):
    return lax.dot_general(a, b, NT_DIMS, preferred_element_type=F32)


def _dot_tn(a, b):
    return lax.dot_general(a, b, TN_DIMS, preferred_element_type=F32)


NT_GROUP_ROWS = 512


def _grouped_nt(w_ref, x):
    done = {}

    def rows(r0, r1):
        g = r0 // NT_GROUP_ROWS
        assert (r1 - 1) // NT_GROUP_ROWS == g
        if g not in done:
            done[g] = _dot_nt(w_ref[g * NT_GROUP_ROWS:(g + 1) * NT_GROUP_ROWS, :], x)
        return done[g][r0 - g * NT_GROUP_ROWS:r1 - g * NT_GROUP_ROWS]

    return rows


def _params(n_axes):
    return pltpu.CompilerParams(dimension_semantics=("arbitrary",) * n_axes,
                                vmem_limit_bytes=VMEM_LIMIT)


def _const_spec(shape):
    nd = len(shape)
    return pl.BlockSpec(shape, lambda *_: (0,) * nd, pipeline_mode=pl.Buffered(1))


def _mod_spec(per_batch):
    if per_batch:
        return pl.BlockSpec((None, N_MOD, D_MODEL), lambda b, t: (b, 0, 0))
    return pl.BlockSpec((None, N_MOD, D_MODEL), lambda b, t: (0, 0, 0))


def _tok_spec(tm):
    return pl.BlockSpec((None, tm, D_MODEL), lambda b, t: (b, t, 0))


def _ada_kernel(c_ref, w_ref, b_ref, o_ref):
    c = c_ref[...]
    a = (c * jax.nn.sigmoid(c)).astype(BF16)
    o_ref[...] = _dot(a, w_ref[...].astype(BF16)) + b_ref[...]


def _ada(cc, w, b):
    rows = cc.shape[0]
    n = w.shape[1]
    tn = 1152
    return pl.pallas_call(
        _ada_kernel,
        grid=(n // tn,),
        in_specs=[pl.BlockSpec((rows, D_MODEL), lambda j: (0, 0)),
                  pl.BlockSpec((D_MODEL, tn), lambda j: (0, j)),
                  pl.BlockSpec((1, tn), lambda j: (0, j))],
        out_specs=pl.BlockSpec((rows, tn), lambda j: (0, j)),
        out_shape=jax.ShapeDtypeStruct((rows, n), F32),
        compiler_params=_params(1),
        name="ada_mod",
    )(cc, w, b.reshape(1, n))


def _rope_table_kernel(inv_ref, userow_ref, sign_ref, cos_ref, sin_ref):
    t = lax.broadcasted_iota(jnp.int32, cos_ref.shape, 1)
    row = lax.shift_right_logical(t, GRID_W.bit_length() - 1).astype(F32)
    col = (t & (GRID_W - 1)).astype(F32)
    pos = jnp.where(userow_ref[...] > 0.5, row, col)
    ang = pos * inv_ref[...]
    sign = sign_ref[...]
    active = sign != 0.0
    cos_ref[...] = jnp.where(active, jnp.cos(ang), 1.0)
    sin_ref[...] = jnp.where(active, sign * jnp.sin(ang), 0.0)


def _rope_tables(seq, rot_dim, x1_slots, x2_slots):
    n_f = rot_dim // 4
    half = rot_dim // 2
    inv_f = (ROPE_BASE ** (-np.arange(n_f, dtype=np.float64) / n_f)).astype(np.float32)
    inv_half = np.concatenate([inv_f, inv_f])
    use_row_half = np.concatenate([np.ones(n_f), np.zeros(n_f)])
    inv = np.zeros(LANES, np.float32)
    use_row = np.zeros(LANES, np.float32)
    sign = np.zeros(LANES, np.float32)
    for slots, sgn in ((x1_slots, -1.0), (x2_slots, 1.0)):
        for start, n in slots:
            assert n == half
            inv[start:start + n] = inv_half
            use_row[start:start + n] = use_row_half
            sign[start:start + n] = sgn
    vec_shape = (LANES, 1)
    tab_shape = (LANES, seq)
    vec = pl.BlockSpec(vec_shape, lambda: (0, 0))
    tab = pl.BlockSpec(tab_shape, lambda: (0, 0))
    return pl.pallas_call(
        _rope_table_kernel,
        in_specs=[vec, vec, vec],
        out_specs=[tab, tab],
        out_shape=[jax.ShapeDtypeStruct(tab_shape, F32)] * 2,
        name="rope_table",
    )(*[jnp.asarray(v.reshape(vec_shape)) for v in (inv, use_row, sign)])


def _ffn_body(x, mod_ref, g, wg_ref, wu_ref, wd_ref, k):
    shift = mod_ref[3 * k:3 * k + 1, :]
    scale = mod_ref[3 * k + 1:3 * k + 2, :]
    gate = mod_ref[3 * k + 2:3 * k + 3, :]
    hn = _adaln(x, g, shift, scale).astype(BF16)
    gg = _dot(hn, wg_ref[...])
    u = _dot(hn, wu_ref[...])
    a = (gg * jax.nn.sigmoid(gg) * u).astype(BF16)
    return x + (0.5 * gate) * _dot(a, wd_ref[...])


FFN_SUB_ROWS = 256


def _ffn_kernel(x_ref, mod_ref, g_ref, wg_ref, wu_ref, wd_ref, o_ref, *, k):
    for r0 in range(0, x_ref.shape[0], FFN_SUB_ROWS):
        rows = slice(r0, r0 + FFN_SUB_ROWS)
        o_ref[rows, :] = _ffn_body(x_ref[rows, :], mod_ref, g_ref[...], wg_ref, wu_ref, wd_ref, k)


def _ffn(x, mod, per_batch, g, wg, wu, wd, k):
    bsz, n, _ = x.shape
    tm = min(TOK_TILE, n)
    return pl.pallas_call(
        functools.partial(_ffn_kernel, k=k),
        grid=(bsz, n // tm),
        in_specs=[_tok_spec(tm), _mod_spec(per_batch), _const_spec((1, D_MODEL)),
                  _const_spec((D_MODEL, D_FF)), _const_spec((D_MODEL, D_FF)),
                  _const_spec((D_FF, D_MODEL))],
        out_specs=_tok_spec(tm),
        out_shape=jax.ShapeDtypeStruct(x.shape, F32),
        compiler_params=_params(2),
        name="ffn",
    )(x, mod, g, wg, wu, wd)


def _out_ffn_kernel(*refs, n_attn):
    h_ref, mod_ref, g_ref = refs[:3]
    attn = refs[3:3 + 2 * n_attn]
    wg_ref, wu_ref, wd_ref, o_ref = refs[3 + 2 * n_attn:]
    y = None
    for a in range(n_attn):
        ot_ref, w_ref = attn[2 * a], attn[2 * a + 1]
        n_t, nh, dv, tq = ot_ref.shape
        rows = [_dot_tn(ot_ref[j].reshape(nh * dv, tq), w_ref[...]) for j in range(n_t)]
        ya = rows[0] if n_t == 1 else jnp.concatenate(rows, axis=0)
        y = ya if y is None else y + ya
    x = h_ref[...] + mod_ref[5:6, :] * y
    o_ref[...] = _ffn_body(x, mod_ref, g_ref[...], wg_ref, wu_ref, wd_ref, 2)


def _out_ffn(h, mod, per_batch, g, attn, wg, wu, wd):
    bsz, n, _ = h.shape
    tm = min(TOK_TILE, n)
    in_specs = [_tok_spec(tm), _mod_spec(per_batch), _const_spec((1, D_MODEL))]
    args = [h, mod, g]
    for o_t, w in attn:
        _, _, nh, dv, tq = o_t.shape
        in_specs += [pl.BlockSpec((None, tm // tq, nh, dv, tq), lambda b, t: (b, t, 0, 0, 0)),
                     _const_spec(w.shape)]
        args += [o_t, w]
    in_specs += [_const_spec((D_MODEL, D_FF)), _const_spec((D_MODEL, D_FF)), _const_spec((D_FF, D_MODEL))]
    args += [wg, wu, wd]
    return pl.pallas_call(
        functools.partial(_out_ffn_kernel, n_attn=len(attn)),
        grid=(bsz, n // tm),
        in_specs=in_specs,
        out_specs=_tok_spec(tm),
        out_shape=jax.ShapeDtypeStruct(h.shape, F32),
        compiler_params=_params(2),
        name="out_ffn",
    )(*args)


L0_ST_SQ = 0
L0_ST_SV = L0_ST_SQ + SWA_HEADS * LANES
L0_ST_KR = L0_ST_SV + SWA_KV_HEADS * SWA_HEAD_DIM
L0_ST_SK = L0_ST_KR + LANES
L0_ST_ROWS = L0_ST_SK + SWA_KV_HEADS * LANES


def _l0_proj_kernel(*refs, rope):
    (h_ref, mod_ref, g_ref, w0_ref, wst_ref, qag_ref, wqbt_ref, kvag_ref, wkvt_ref,
     mqg_ref, mkg_ref, sqg_ref, skg_ref) = refs[:13]
    if rope:
        cmt, smt, cst, sst = [r[...] for r in refs[13:17]]
        outs = refs[17:]
    else:
        outs = refs[13:]
    qm_ref, km_ref, vmt_ref, qs_ref, ks_ref, vst_ref = outs

    hn = _adaln(h_ref[...], g_ref[...], mod_ref[3:4, :], mod_ref[4:5, :]).astype(BF16)
    tm = hn.shape[0]
    p = _dot(hn, w0_ref[...])
    st = _grouped_nt(wst_ref, hn)

    mla_inv = 1.0 / (MLA_NOPE + MLA_ROPE)
    swa_inv = 1.0 / SWA_HEAD_DIM
    bcast = lambda v: jnp.broadcast_to(v, (LANES, tm))
    mqg = bcast(mqg_ref[...] * (mla_inv ** 0.5 * LOG2E))
    sqg = bcast(sqg_ref[...] * (swa_inv ** 0.5 * LOG2E))
    mkg = bcast(mkg_ref[...])
    skg = bcast(skg_ref[...])

    def mla_rope(t):
        return _rope_t(t, cmt, smt, MLA_ROPE_ROWS) if rope else t

    def swa_rope(t):
        return _rope_t(t, cst, sst, SWA_ROPE_ROWS) if rope else t

    for h in range(SWA_HEADS):
        t = swa_rope(_rms_t(st(L0_ST_SQ + h * LANES, L0_ST_SQ + (h + 1) * LANES), sqg, swa_inv))
        for i in range(tm // SWA_BLK):
            qs_ref[h, i] = t[:, i * SWA_BLK:(i + 1) * SWA_BLK].astype(BF16)
    for j in range(SWA_KV_HEADS):
        t = swa_rope(_rms_t(st(L0_ST_SK + j * LANES, L0_ST_SK + (j + 1) * LANES), skg, swa_inv))
        ks_ref[j] = t.T.astype(BF16)
        vt = st(L0_ST_SV + j * SWA_HEAD_DIM, L0_ST_SV + (j + 1) * SWA_HEAD_DIM)
        for i in range(tm // SWA_BLK):
            vst_ref[j, i] = vt[:, i * SWA_BLK:(i + 1) * SWA_BLK].astype(BF16)

    qa = _rms(p[:, 0:MLA_Q_RANK], qag_ref[...], 1.0 / MLA_Q_RANK).astype(BF16)
    qt = _grouped_nt(wqbt_ref, qa)
    for h in range(MLA_HEADS):
        t = mla_rope(_rms_t(qt(h * LANES, (h + 1) * LANES), mqg, mla_inv))
        tq = qm_ref.shape[-1]
        for j in range(tm // tq):
            qm_ref[h, j] = t[:, j * tq:(j + 1) * tq].astype(BF16)

    kva = _rms(p[:, MLA_Q_RANK:MLA_Q_RANK + MLA_KV_RANK], kvag_ref[...], 1.0 / MLA_KV_RANK).astype(BF16)
    kvt = _grouped_nt(wkvt_ref, kva)
    kr = st(L0_ST_KR, L0_ST_KR + LANES)
    for h in range(MLA_HEADS):
        t = mla_rope(_rms_t(kvt(h * LANES, (h + 1) * LANES) + kr, mkg, mla_inv))
        km_ref[h] = t.T.astype(BF16)
    v0 = MLA_HEADS * LANES
    for h in range(MLA_HEADS):
        vmt_ref[h] = kvt(v0 + h * MLA_V, v0 + (h + 1) * MLA_V).astype(BF16)


def _l0_proj(h, mod, per_batch, g, w, tables):
    bsz, n, _ = h.shape
    tm = min(TOK_TILE, n)
    rope = tables is not None
    weights = [g, w["w0"], w["wst"], w["qa_g"], w["wqbt"], w["kva_g"], w["wkvt"],
               w["mq_g"], w["mk_g"], w["sq_g"], w["sk_g"]]
    in_specs = [_tok_spec(tm), _mod_spec(per_batch)] + [_const_spec(a.shape) for a in weights]
    args = [h, mod] + weights
    if rope:
        in_specs += [pl.BlockSpec((LANES, tm), lambda b, t: (0, t))] * 4
        args += list(tables)
    k_spec = lambda nh: pl.BlockSpec((None, nh, tm, LANES), lambda b, t: (b, 0, t, 0))
    qt_spec = lambda tq: pl.BlockSpec((None, MLA_HEADS, tm // tq, LANES, tq), lambda b, t: (b, 0, t, 0, 0))
    tq = min(MLA_TQ, n)
    out_specs = [qt_spec(tq), k_spec(MLA_HEADS),
                 pl.BlockSpec((None, MLA_HEADS, MLA_V, tm), lambda b, t: (b, 0, 0, t)),
                 qt_spec(SWA_BLK), k_spec(SWA_KV_HEADS),
                 pl.BlockSpec((None, SWA_KV_HEADS, tm // SWA_BLK, SWA_HEAD_DIM, SWA_BLK),
                              lambda b, t: (b, 0, t, 0, 0))]
    out_shape = [jax.ShapeDtypeStruct((bsz, MLA_HEADS, n // tq, LANES, tq), BF16),
                 jax.ShapeDtypeStruct((bsz, MLA_HEADS, n, LANES), BF16),
                 jax.ShapeDtypeStruct((bsz, MLA_HEADS, MLA_V, n), BF16),
                 jax.ShapeDtypeStruct((bsz, SWA_HEADS, n // SWA_BLK, LANES, SWA_BLK), BF16),
                 jax.ShapeDtypeStruct((bsz, SWA_KV_HEADS, n, LANES), BF16),
                 jax.ShapeDtypeStruct((bsz, SWA_KV_HEADS, n // SWA_BLK, SWA_HEAD_DIM, SWA_BLK), BF16)]
    return pl.pallas_call(
        functools.partial(_l0_proj_kernel, rope=rope),
        grid=(bsz, n // tm),
        in_specs=in_specs, out_specs=out_specs, out_shape=out_shape,
        compiler_params=_params(2),
        name="l0_proj",
    )(*args)


def _interleave(*stages):
    live = list(stages)
    while live:
        for st in list(live):
            try:
                next(st)
            except StopIteration:
                live.remove(st)


def _two_stage_loop(n_items, produce, consume, slot0, slot1):
    _interleave(produce(0, slot0))

    def pair(j, carry):
        i = 2 * j
        _interleave(produce(i + 1, slot1), consume(i, slot0))
        _interleave(produce(jnp.minimum(i + 2, n_items - 1), slot0), consume(i + 1, slot1))
        return carry

    lax.fori_loop(0, n_items // 2, pair, 0)


def _chunk_fold(x, op):
    return op(x.reshape(x.shape[0] // SUBLANES, SUBLANES, x.shape[-1]), axis=0)


def _score_chunks(s_ref, m_ref, key_chunk, qt, chunks, bias=None, floor=None):
    mx = None
    for k0, size in chunks:
        c = _dot(key_chunk(k0, size), qt)
        b = None if bias is None else bias(k0, size)
        if b is not None:
            c = c + b
        s_ref[k0:k0 + size, :] = c
        part = _chunk_fold(c, jnp.max)
        mx = part if mx is None else jnp.maximum(mx, part)
        yield
    m = jnp.max(mx, axis=0, keepdims=True)
    m_ref[...] = m if floor is None else jnp.maximum(m, floor)


def _exp_pv_chunks(s_ref, m_ref, value_t, chunk, out):
    m = m_ref[...]
    l8 = None
    o = None
    for k0 in range(0, s_ref.shape[0], chunk):
        e = jnp.exp2(s_ref[k0:k0 + chunk, :] - m)
        part = _chunk_fold(e, jnp.sum)
        l8 = part if l8 is None else l8 + part
        d = _dot(value_t(k0), e.astype(BF16))
        o = d if o is None else o + d
        yield
    out.append((o, jnp.sum(l8, axis=0, keepdims=True)))


KEY_CHUNK = 256


def _ctx_then_latent(c_ref, l_ref, n_ctx, token_major):
    def chunk(k0, size=KEY_CHUNK):
        ref, k = (c_ref, k0) if k0 < n_ctx else (l_ref, k0 - n_ctx)
        return ref[k:k + size, :] if token_major else ref[:, k:k + size]
    return chunk


def _key_chunks(n_keys):
    return [(k0, KEY_CHUNK) for k0 in range(0, n_keys, KEY_CHUNK)]


def _mla_attn_kernel(q_ref, kc_ref, vct_ref, kl_ref, vlt_ref, o_ref, s0_ref, m0_ref, s1_ref, m1_ref):
    n_tiles, n_heads = o_ref.shape[:2]
    n_ctx = kc_ref.shape[1]

    def split(i):
        return lax.div(i, n_tiles), lax.rem(i, n_tiles)

    def scores(i, slot):
        s_ref, m_ref = slot
        h, t = split(i)
        keys = _ctx_then_latent(kc_ref.at[h], kl_ref.at[h], n_ctx, True)
        yield from _score_chunks(s_ref, m_ref, keys, q_ref[h, t], _key_chunks(s_ref.shape[0]))

    def softmax_pv(i, slot):
        s_ref, m_ref = slot
        h, t = split(i)
        values = _ctx_then_latent(vct_ref.at[h], vlt_ref.at[h], n_ctx, False)
        res = []
        yield from _exp_pv_chunks(s_ref, m_ref, values, KEY_CHUNK, res)
        o, l = res[0]
        o_ref[t, h] = (o * (1.0 / l)).astype(BF16)

    _two_stage_loop(n_tiles * n_heads, scores, softmax_pv, (s0_ref, m0_ref), (s1_ref, m1_ref))


def _mla_attn(q, kc, vct, kl, vlt):
    bsz, nh, n_tiles = q.shape[:3]
    hs = ATT_HEADS_PER_STEP
    n_keys = kc.shape[2] + kl.shape[2]
    grp = lambda shape: pl.BlockSpec((None, hs) + tuple(shape), lambda b, g: (b, g) + (0,) * len(shape))
    tq = q.shape[-1]
    slot = [pltpu.VMEM((n_keys, tq), F32), pltpu.VMEM((1, tq), F32)]
    return pl.pallas_call(
        _mla_attn_kernel,
        grid=(bsz, nh // hs),
        in_specs=[grp(q.shape[2:]), grp(kc.shape[2:]), grp(vct.shape[2:]), grp(kl.shape[2:]), grp(vlt.shape[2:])],
        out_specs=pl.BlockSpec((None, n_tiles, hs, MLA_V, tq), lambda b, g: (b, 0, g, 0, 0)),
        out_shape=jax.ShapeDtypeStruct((bsz, n_tiles, nh, MLA_V, tq), BF16),
        scratch_shapes=slot + slot,
        compiler_params=_params(2),
        name="mla_attn",
    )(q, kc, vct, kl, vlt)


def _mla_ctx_attn_kernel(q_ref, kc_ref, vct_ref, o_ref):
    for h in range(o_ref.shape[0]):
        sc = _dot(kc_ref[h], q_ref[h])
        pc = jnp.exp2(sc - jnp.max(sc, axis=0, keepdims=True))
        l = jnp.sum(pc, axis=0, keepdims=True)
        o = _dot(vct_ref[h], pc.astype(BF16))
        o_ref[h] = (o * (1.0 / l)).astype(BF16)


def _mla_ctx_attn(q, kc, vct):
    bsz, nh, n_tiles = q.shape[:3]
    assert n_tiles == 1
    tq = q.shape[-1]
    per_b = lambda shape: pl.BlockSpec((None,) + tuple(shape), lambda b: (b,) + (0,) * len(shape))
    return pl.pallas_call(
        _mla_ctx_attn_kernel,
        grid=(bsz,),
        in_specs=[pl.BlockSpec((None, nh, None, LANES, tq), lambda b: (b, 0, 0, 0, 0)),
                  per_b(kc.shape[1:]), per_b(vct.shape[1:])],
        out_specs=pl.BlockSpec((None, None, nh, MLA_V, tq), lambda b: (b, 0, 0, 0, 0)),
        out_shape=jax.ShapeDtypeStruct((bsz, 1, nh, MLA_V, tq), BF16),
        compiler_params=_params(1),
        name="mla_ctx_attn",
    )(q, kc, vct)


SWA_BAND = 3 * SWA_BLK
SWA_NQ = SWA_GROUP * SWA_BLK


def _sink_row(sink_ref, j):
    return jnp.concatenate(
        [jnp.full((1, SWA_BLK), sink_ref[j * SWA_GROUP + g] * LOG2E, F32) for g in range(SWA_GROUP)], axis=1)


def _swa_attn_kernel(sink_ref, q_ref, kc_ref, vct_ref, kl_ref, vlt_ref, o_ref,
                     bias_ref, s0_ref, m0_ref, s1_ref, m1_ref):
    n_blk = o_ref.shape[0]
    n_ctx = kc_ref.shape[1]

    @pl.when(pl.program_id(0) == 0)
    def _():
        r = lax.broadcasted_iota(jnp.int32, (SWA_BAND, SWA_NQ), 0)
        c = lax.broadcasted_iota(jnp.int32, (SWA_BAND, SWA_NQ), 1) & (SWA_BLK - 1)
        for case in range(3):
            dist = r - c - case * SWA_BLK
            bias_ref[case] = jnp.where(jnp.abs(dist) <= SWA_WINDOW, 0.0, MASKED)

    def split(i):
        return lax.div(i, n_blk), lax.rem(i, n_blk)

    def band_start(n):
        return jnp.clip(n - 1, 0, n_blk - 3)

    def scores(i, slot):
        s_ref, m_ref = slot
        j, n = split(i)
        qt = jnp.concatenate([q_ref[j * SWA_GROUP + g, n] for g in range(SWA_GROUP)], axis=1)
        band0 = band_start(n) * SWA_BLK
        case = (n > 0).astype(jnp.int32) + (n == n_blk - 1).astype(jnp.int32)

        def keys(k0, size):
            if k0 < n_ctx:
                return kc_ref[j, k0:k0 + size, :]
            return kl_ref[j, pl.ds(pl.multiple_of(band0 + (k0 - n_ctx), SWA_BLK), size), :]

        def bias(k0, size):
            return None if k0 < n_ctx else bias_ref[case, k0 - n_ctx:k0 - n_ctx + size, :]

        chunks = [(0, n_ctx), (n_ctx, SWA_BAND)]
        yield from _score_chunks(s_ref, m_ref, keys, qt, chunks, bias, _sink_row(sink_ref, j))

    def softmax_pv(i, slot):
        s_ref, m_ref = slot
        j, n = split(i)
        b0 = band_start(n)

        def value_t(k0):
            if k0 < n_ctx:
                return vct_ref[j, :, k0:k0 + SWA_BLK]
            return vlt_ref[j, b0 + (k0 - n_ctx) // SWA_BLK]

        res = []
        yield from _exp_pv_chunks(s_ref, m_ref, value_t, SWA_BLK, res)
        o, l = res[0]
        o = o * (1.0 / (l + jnp.exp2(_sink_row(sink_ref, j) - m_ref[...])))
        for g in range(SWA_GROUP):
            o_ref[n, j * SWA_GROUP + g] = o[:, g * SWA_BLK:(g + 1) * SWA_BLK].astype(BF16)

    _two_stage_loop(SWA_KV_HEADS * n_blk, scores, softmax_pv, (s0_ref, m0_ref), (s1_ref, m1_ref))


def _swa_attn(sink, q, kc, vct, kl, vlt):
    bsz, nh, n_blk = q.shape[:3]
    n_keys = kc.shape[2] + SWA_BAND
    full = lambda a: pl.BlockSpec((None,) + a.shape[1:], lambda b: (b,) + (0,) * (a.ndim - 1))
    slot = [pltpu.VMEM((n_keys, SWA_NQ), F32), pltpu.VMEM((1, SWA_NQ), F32)]
    return pl.pallas_call(
        _swa_attn_kernel,
        grid=(bsz,),
        in_specs=[pl.BlockSpec(memory_space=pltpu.SMEM), full(q), full(kc), full(vct), full(kl), full(vlt)],
        out_specs=pl.BlockSpec((None, n_blk, nh, SWA_HEAD_DIM, SWA_BLK), lambda b: (b, 0, 0, 0, 0)),
        out_shape=jax.ShapeDtypeStruct((bsz, n_blk, nh, SWA_HEAD_DIM, SWA_BLK), BF16),
        scratch_shapes=[pltpu.VMEM((3, SWA_BAND, SWA_NQ), F32)] + slot + slot,
        compiler_params=_params(1),
        name="swa_attn",
    )(sink, q, kc, vct, kl, vlt)


def _swa_ctx_attn_kernel(sink_ref, q_ref, kc_ref, vct_ref, o_ref):
    sink = _sink_row(sink_ref, pl.program_id(1))
    for n in range(o_ref.shape[0]):
        qt = jnp.concatenate([q_ref[g, n] for g in range(SWA_GROUP)], axis=1)
        s = _dot(kc_ref[...], qt)
        m = jnp.maximum(jnp.max(s, axis=0, keepdims=True), sink)
        p = jnp.exp2(s - m)
        l = jnp.sum(p, axis=0, keepdims=True) + jnp.exp2(sink - m)
        o = _dot(vct_ref[...], p.astype(BF16)) * (1.0 / l)
        for g in range(SWA_GROUP):
            o_ref[n, g] = o[:, g * SWA_BLK:(g + 1) * SWA_BLK].astype(BF16)


def _swa_ctx_attn(sink, q, kc, vct):
    bsz, nh, n_blk = q.shape[:3]
    return pl.pallas_call(
        _swa_ctx_attn_kernel,
        grid=(bsz, SWA_KV_HEADS),
        in_specs=[pl.BlockSpec(memory_space=pltpu.SMEM),
                  pl.BlockSpec((None, SWA_GROUP, n_blk, LANES, SWA_BLK), lambda b, j: (b, j, 0, 0, 0)),
                  pl.BlockSpec((None, None) + kc.shape[2:], lambda b, j: (b, j, 0, 0)),
                  pl.BlockSpec((None, None) + vct.shape[2:], lambda b, j: (b, j, 0, 0))],
        out_specs=pl.BlockSpec((None, n_blk, SWA_GROUP, SWA_HEAD_DIM, SWA_BLK), lambda b, j: (b, 0, j, 0, 0)),
        out_shape=jax.ShapeDtypeStruct((bsz, n_blk, nh, SWA_HEAD_DIM, SWA_BLK), BF16),
        compiler_params=_params(2),
        name="swa_ctx_attn",
    )(sink, q, kc, vct)


DIFF_Q = HALF // 2


def _diff_norm_t(t, g):
    sq = t * t
    grp = [jnp.sum(sq[i * DIFF_Q:(i + 1) * DIFF_Q], axis=0, keepdims=True) for i in range(4)]
    inv = 1.0 / DIFF_HEAD_DIM
    r0 = lax.rsqrt((grp[0] + grp[2]) * inv + EPS)
    r1 = lax.rsqrt((grp[1] + grp[3]) * inv + EPS)
    parts = [t[i * DIFF_Q:(i + 1) * DIFF_Q] * (r0 if i % 2 == 0 else r1) for i in range(4)]
    return jnp.concatenate(parts, axis=0) * g


def _l1_proj_kernel(*refs, latent):
    if latent:
        h_ref, mod_ref, g_ref, wt_ref, qg_ref, kg_ref, cdt_ref, sdt_ref, q_ref, k_ref, vt_ref = refs
        cdt, sdt = cdt_ref[...], sdt_ref[...]
    else:
        h_ref, mod_ref, g_ref, wt_ref, kg_ref, k_ref, vt_ref = refs
    hn = _adaln(h_ref[...], g_ref[...], mod_ref[3:4, :], mod_ref[4:5, :]).astype(BF16)
    tm = hn.shape[0]
    width = DIFF_HEADS * LANES
    ft = _grouped_nt(wt_ref, hn)
    bcast = lambda v: jnp.broadcast_to(v, (LANES, tm))
    k0 = 0
    if latent:
        qg = bcast(qg_ref[...] * (DIFF_HEAD_DIM ** -0.5 * LOG2E))
        for h in range(DIFF_HEADS):
            t = _rope_t(_diff_norm_t(ft(h * LANES, (h + 1) * LANES), qg), cdt, sdt, DIFF_ROPE_ROWS)
            tq = q_ref.shape[-1]
            for j in range(tm // tq):
                q_ref[h, j] = t[:, j * tq:(j + 1) * tq].astype(BF16)
        k0 = width

    lane = lax.broadcasted_iota(jnp.int32, (tm, LANES), 1)
    mask0 = (lane & (HALF - 1)) < DIFF_Q
    kg = bcast(kg_ref[...])
    for h in range(DIFF_HEADS):
        t = _diff_norm_t(ft(k0 + h * LANES, k0 + (h + 1) * LANES), kg)
        if latent:
            t = _rope_t(t, cdt, sdt, DIFF_ROPE_ROWS)
        t = t.T
        k_ref[h, 0] = jnp.where(mask0, t, 0.0).astype(BF16)
        k_ref[h, 1] = jnp.where(mask0, 0.0, t).astype(BF16)
    v0 = k0 + width
    for h in range(DIFF_HEADS):
        vt_ref[h] = ft(v0 + h * LANES, v0 + (h + 1) * LANES).astype(BF16)


def _l1_proj(h, mod, per_batch, g, w, tables):
    bsz, n, _ = h.shape
    tm = min(TOK_TILE, n)
    latent = tables is not None
    k_spec = pl.BlockSpec((None, DIFF_HEADS, 2, tm, LANES), lambda b, t: (b, 0, 0, t, 0))
    vt_spec = pl.BlockSpec((None, DIFF_HEADS, LANES, tm), lambda b, t: (b, 0, 0, t))
    k_shape = jax.ShapeDtypeStruct((bsz, DIFF_HEADS, 2, n, LANES), BF16)
    vt_shape = jax.ShapeDtypeStruct((bsz, DIFF_HEADS, LANES, n), BF16)
    if latent:
        weights = [g, w["wqkvt"], w["q_g"], w["k_g"]]
        in_specs = ([_tok_spec(tm), _mod_spec(per_batch)] + [_const_spec(a.shape) for a in weights]
                    + [pl.BlockSpec((LANES, tm), lambda b, t: (0, t))] * 2)
        args = [h, mod] + weights + list(tables)
        tq = min(DIFF_TQ, n)
        out_specs = [pl.BlockSpec((None, DIFF_HEADS, tm // tq, LANES, tq), lambda b, t: (b, 0, t, 0, 0)),
                     k_spec, vt_spec]
        out_shape = [jax.ShapeDtypeStruct((bsz, DIFF_HEADS, n // tq, LANES, tq), BF16), k_shape, vt_shape]
    else:
        weights = [g, w["wkvt"], w["k_g"]]
        in_specs = [_tok_spec(tm), _mod_spec(per_batch)] + [_const_spec(a.shape) for a in weights]
        args = [h, mod] + weights
        out_specs = [k_spec, vt_spec]
        out_shape = [k_shape, vt_shape]
    return pl.pallas_call(
        functools.partial(_l1_proj_kernel, latent=latent),
        grid=(bsz, n // tm),
        in_specs=in_specs, out_specs=out_specs, out_shape=out_shape,
        compiler_params=_params(2),
        name="l1_proj",
    )(*args)


def _diff_attn_kernel(lam_ref, subg_ref, q_ref, kc_ref, vct_ref, kl_ref, vlt_ref, o_ref,
                      s0_ref, m0_ref, s1_ref, m1_ref):
    n_tiles, n_heads = o_ref.shape[:2]
    n_ctx = kc_ref.shape[2]
    lv = lam_ref[...]
    lam = (jnp.exp(jnp.sum(lv[0:1] * lv[1:2], axis=-1, keepdims=True))
           - jnp.exp(jnp.sum(lv[2:3] * lv[3:4], axis=-1, keepdims=True)) + LAMBDA_INIT_L1)
    subg = subg_ref[...] * (1.0 - LAMBDA_INIT_L1)

    def split(i):
        return lax.div(i, n_tiles), lax.rem(i, n_tiles)

    def scores(i, slot):
        s_ref, m_ref = slot
        h, t = split(i)
        qt = q_ref[h, t]
        for s in range(2):
            keys = _ctx_then_latent(kc_ref.at[h, s], kl_ref.at[h, s], n_ctx, True)
            yield from _score_chunks(s_ref.at[s], m_ref.at[s], keys, qt, _key_chunks(s_ref.shape[1]))

    def softmax_pv(i, slot):
        s_ref, m_ref = slot
        h, t = split(i)
        values = _ctx_then_latent(vct_ref.at[h], vlt_ref.at[h], n_ctx, False)
        res = []
        for s in range(2):
            yield from _exp_pv_chunks(s_ref.at[s], m_ref.at[s], values, KEY_CHUNK, res)
        (o0, l0), (o1, l1) = res
        o = o0 * (1.0 / l0) - o1 * (lam * (1.0 / l1))
        ms = jnp.mean(o * o, axis=0, keepdims=True)
        o_ref[t, h] = (o * lax.rsqrt(ms + EPS) * subg).astype(BF16)

    _two_stage_loop(n_tiles * n_heads, scores, softmax_pv, (s0_ref, m0_ref), (s1_ref, m1_ref))


def _diff_attn(lamv, subg, q, kc, vct, kl, vlt):
    bsz, nh, n_tiles = q.shape[:3]
    hs = ATT_HEADS_PER_STEP
    n_keys = kc.shape[3] + kl.shape[3]
    grp = lambda shape: pl.BlockSpec((None, hs) + tuple(shape), lambda b, g: (b, g) + (0,) * len(shape))
    tq = q.shape[-1]
    slot = [pltpu.VMEM((2, n_keys, tq), F32), pltpu.VMEM((2, 1, tq), F32)]
    return pl.pallas_call(
        _diff_attn_kernel,
        grid=(bsz, nh // hs),
        in_specs=[_const_spec(lamv.shape), _const_spec(subg.shape),
                  grp(q.shape[2:]), grp(kc.shape[2:]), grp(vct.shape[2:]), grp(kl.shape[2:]), grp(vlt.shape[2:])],
        out_specs=pl.BlockSpec((None, n_tiles, hs, LANES, tq), lambda b, g: (b, 0, g, 0, 0)),
        out_shape=jax.ShapeDtypeStruct((bsz, n_tiles, nh, LANES, tq), BF16),
        scratch_shapes=slot + slot,
        compiler_params=_params(2),
        name="diff_attn",
    )(lamv, subg, q, kc, vct, kl, vlt)


def _prep_l0(w_in, qa_g, wqb, kva_g, wkvb, q_g, k_g, sq_g, sk_g, w_out):
    mla = _mla_lane_map()
    swa = _swa_lane_map()
    kr_cols = np.where(mla >= MLA_NOPE, 384 + mla - MLA_NOPE, -1)
    wst = jnp.concatenate([
        _take_cols(w_in, _per_head(swa, SWA_HEADS, SWA_HEAD_DIM, base=416)),
        w_in[:, 1056:1184],
        _take_cols(w_in, kr_cols),
        _take_cols(w_in, _per_head(swa, SWA_KV_HEADS, SWA_HEAD_DIM, base=928)),
    ], axis=1).T
    assert wst.shape[0] == L0_ST_ROWS
    nope_map = np.where((mla >= 0) & (mla < MLA_NOPE), mla, -1)
    v_cols = np.concatenate([h * (MLA_NOPE + MLA_V) + MLA_NOPE + np.arange(MLA_V) for h in range(MLA_HEADS)])
    wkvt = jnp.concatenate([
        _take_cols(wkvb, _per_head(nope_map, MLA_HEADS, MLA_NOPE + MLA_V)),
        jnp.take(wkvb, jnp.asarray(v_cols, jnp.int32), axis=1),
    ], axis=1).T
    row = lambda v: v.reshape(1, -1).astype(F32)
    col = lambda v: v.reshape(-1, 1).astype(F32)
    return {
        "w0": w_in[:, 0:MLA_Q_RANK + MLA_KV_RANK].astype(BF16),
        "wst": wst.astype(BF16),
        "qa_g": row(qa_g),
        "wqbt": _take_cols(wqb, _per_head(mla, MLA_HEADS, MLA_NOPE + MLA_ROPE)).T.astype(BF16),
        "kva_g": row(kva_g),
        "wkvt": wkvt.astype(BF16),
        "mq_g": col(_take_cols(q_g, mla)),
        "mk_g": col(_take_cols(k_g, mla)),
        "sq_g": col(_take_cols(sq_g, swa)),
        "sk_g": col(_take_cols(sk_g, swa)),
        "wa": w_out[:MLA_HEADS * MLA_V].astype(BF16),
        "wb": w_out[MLA_HEADS * MLA_V:].astype(BF16),
    }


def _prep_l1(w_in, q_g, k_g, w_out):
    dm = _diff_lane_map()
    width = DIFF_HEADS * LANES
    wqt = _take_cols(w_in, _per_head(dm, DIFF_HEADS, LANES)).T
    wkt = _take_cols(w_in, _per_head(dm, DIFF_HEADS, LANES, base=width)).T
    wvt = w_in[:, 2 * width:].T
    return {
        "wqkvt": jnp.concatenate([wqt, wkt, wvt], axis=0).astype(BF16),
        "wkvt": jnp.concatenate([wkt, wvt], axis=0).astype(BF16),
        "q_g": _take_cols(q_g, dm % DIFF_HEAD_DIM).reshape(-1, 1).astype(F32),
        "k_g": _take_cols(k_g, dm % DIFF_HEAD_DIM).reshape(-1, 1).astype(F32),
        "w_out": w_out.astype(BF16),
    }


def kernel(x, c, ctx, c_ctx, l0_ada_w, l0_ada_b, l0_norm_g, l0_ffn_wg, l0_ffn_wu, l0_ffn_wd, l0_w_in, l0_mla_qa_g, l0_mla_wqb, l0_mla_kva_g, l0_mla_wkvb, l0_mla_q_g, l0_mla_k_g, l0_swa_q_g, l0_swa_k_g, l0_swa_sink, l0_w_out, l1_ada_w, l1_ada_b, l1_norm_g, l1_ffn_wg, l1_ffn_wu, l1_ffn_wd, l1_w_in, l1_q_g, l1_k_g, l1_lambda_q1, l1_lambda_k1, l1_lambda_q2, l1_lambda_k2, l1_subln_g, l1_w_out):
    bsz, seq, _ = x.shape

    pad = (-(bsz + 1)) % SUBLANES
    cc = jnp.concatenate([c, c_ctx[None, :], jnp.zeros((pad, D_MODEL), F32)], axis=0)

    def mods(ada_w, ada_b):
        m = _ada(cc, ada_w, ada_b).reshape(cc.shape[0], N_MOD, D_MODEL)
        return m[:bsz], m[bsz:bsz + 1]

    def ffn_weights(wg, wu, wd, i):
        return wg[i].astype(BF16), wu[i].astype(BF16), wd[i].astype(BF16)

    mla_tab = _rope_tables(seq, MLA_ROPE, [(0, 16)], [(64, 16)])
    swa_tab = _rope_tables(seq, SWA_HEAD_DIM, [(0, 32)], [(64, 32)])
    diff_tab = _rope_tables(seq, DIFF_HEAD_DIM, [(0, 32), (32, 32)], [(64, 32), (96, 32)])

    mod, mod_c = mods(l0_ada_w, l0_ada_b)
    g = [l0_norm_g[i:i + 1] for i in range(3)]
    w = _prep_l0(l0_w_in, l0_mla_qa_g, l0_mla_wqb, l0_mla_kva_g, l0_mla_wkvb, l0_mla_q_g, l0_mla_k_g,
                 l0_swa_q_g, l0_swa_k_g, l0_w_out)
    f0 = ffn_weights(l0_ffn_wg, l0_ffn_wu, l0_ffn_wd, 0)
    f1 = ffn_weights(l0_ffn_wg, l0_ffn_wu, l0_ffn_wd, 1)

    h = _ffn(x, mod, True, g[0], *f0, k=0)
    hc = _ffn(ctx, mod_c, False, g[0], *f0, k=0)

    qm, km, vmt, qs, ks, vst = _l0_proj(h, mod, True, g[1], w, mla_tab + swa_tab)
    qm_c, km_c, vmt_c, qs_c, ks_c, vst_c = _l0_proj(hc, mod_c, False, g[1], w, None)
    vsc_t = vst_c.transpose(0, 1, 3, 2, 4).reshape(bsz, SWA_KV_HEADS, SWA_HEAD_DIM, -1)

    at = _mla_attn(qm, km_c, vmt_c, km, vmt)
    bt = _swa_attn(l0_swa_sink, qs, ks_c, vsc_t, ks, vst)
    at_c = _mla_ctx_attn(qm_c, km_c, vmt_c)
    bt_c = _swa_ctx_attn(l0_swa_sink, qs_c, ks_c, vsc_t)

    h = _out_ffn(h, mod, True, g[2], [(at, w["wa"]), (bt, w["wb"])], *f1)
    hc = _out_ffn(hc, mod_c, False, g[2], [(at_c, w["wa"]), (bt_c, w["wb"])], *f1)

    mod, mod_c = mods(l1_ada_w, l1_ada_b)
    g = [l1_norm_g[i:i + 1] for i in range(3)]
    w = _prep_l1(l1_w_in, l1_q_g, l1_k_g, l1_w_out)
    f0 = ffn_weights(l1_ffn_wg, l1_ffn_wu, l1_ffn_wd, 0)
    f1 = ffn_weights(l1_ffn_wg, l1_ffn_wu, l1_ffn_wd, 1)

    h = _ffn(h, mod, True, g[0], *f0, k=0)
    hc = _ffn(hc, mod_c, False, g[0], *f0, k=0)

    qd, kd, vdt = _l1_proj(h, mod, True, g[1], w, diff_tab)
    kd_c, vdt_c = _l1_proj(hc, mod_c, False, g[1], w, None)
    lamv = jnp.stack([l1_lambda_q1, l1_lambda_k1, l1_lambda_q2, l1_lambda_k2]).astype(F32)
    ot = _diff_attn(lamv, l1_subln_g.reshape(-1, 1).astype(F32), qd, kd_c, vdt_c, kd, vdt)
    return _out_ffn(h, mod, True, g[2], [(ot, w["w_out"])], *f1)
```

```python
import functools
import math

import numpy as np
import jax
import jax.numpy as jnp
from jax import lax
from jax.experimental import pallas as pl
from jax.experimental.pallas import tpu as pltpu

F32 = jnp.float32
BF16 = jnp.bfloat16

D_MODEL = 1024
D_FF = 2816
N_MOD = 9
EPS = 1e-6
ROPE_BASE = 10000.0
GRID_W = 64
LOG2E = 1.4426950408889634

MLA_HEADS = 8
MLA_Q_RANK = 256
MLA_KV_RANK = 128
MLA_NOPE = 64
MLA_ROPE = 32
MLA_V = 64
SWA_HEADS = 8
SWA_KV_HEADS = 2
SWA_GROUP = SWA_HEADS // SWA_KV_HEADS
SWA_HEAD_DIM = 64
SWA_WINDOW = 128
DIFF_HEADS = 8
DIFF_HEAD_DIM = 64
LAMBDA_INIT_L1 = 0.8 - 0.6 * math.exp(-0.3 * 1)

LANES = 128
SUBLANES = 8
HALF = LANES // 2
MLA_TQ = 512
DIFF_TQ = 256
SWA_BLK = 128
TOK_TILE = 512
FFN_TILE = 1024
ATT_HEADS_PER_STEP = 8
VMEM_LIMIT = 56 * 1024 * 1024
MASKED = -1e30

NT_DIMS = (((1,), (1,)), ((), ()))
TN_DIMS = (((0,), (0,)), ((), ()))


def _mla_lane_map():
    m = np.full(LANES, -1, np.int64)
    m[0:16] = 64 + np.arange(16)
    m[16:48] = np.arange(32)
    m[64:80] = 80 + np.arange(16)
    m[80:112] = 32 + np.arange(32)
    return m


def _swa_lane_map():
    m = np.full(LANES, -1, np.int64)
    m[0:32] = np.arange(32)
    m[64:96] = 32 + np.arange(32)
    return m


def _diff_lane_map():
    m = np.zeros(LANES, np.int64)
    m[0:32] = np.arange(32)
    m[32:64] = 64 + np.arange(32)
    m[64:96] = 32 + np.arange(32)
    m[96:128] = 96 + np.arange(32)
    return m


MLA_ROPE_ROWS = MLA_ROPE // 2
SWA_ROPE_ROWS = SWA_HEAD_DIM // 2
DIFF_ROPE_ROWS = HALF


def _take_cols(w, idx):
    idx = np.asarray(idx)
    out = jnp.take(w, jnp.asarray(np.where(idx < 0, 0, idx), jnp.int32), axis=-1)
    return jnp.where(jnp.asarray(idx >= 0), out, jnp.zeros((), w.dtype))


def _per_head(lane_map, n_heads, stride, base=0):
    cols = [np.where(lane_map >= 0, base + h * stride + lane_map, -1) for h in range(n_heads)]
    return np.concatenate(cols)


def _adaln(x, g, shift, scale):
    ms = jnp.mean(x * x, axis=-1, keepdims=True)
    return x * lax.rsqrt(ms + EPS) * g * (1.0 + scale) + shift


def _rms(t, g, inv_dim):
    ms = jnp.sum(t * t, axis=-1, keepdims=True) * inv_dim
    return t * lax.rsqrt(ms + EPS) * g


def _rms_t(t, g, inv_dim):
    ms = jnp.sum(t * t, axis=0, keepdims=True) * inv_dim
    return t * lax.rsqrt(ms + EPS) * g


def _rope_t(t, cos, sin, rows):
    a, b = t[0:rows], t[HALF:HALF + rows]
    a2 = a * cos[0:rows] + b * sin[0:rows]
    b2 = b * cos[HALF:HALF + rows] + a * sin[HALF:HALF + rows]
    if rows == HALF:
        return jnp.concatenate([a2, b2], axis=0)
    return jnp.concatenate([a2, t[rows:HALF], b2, t[HALF + rows:]], axis=0)


def _dot(a, b):
    return jnp.dot(a, b, preferred_element_type=F32)


def _dot_nt(a, b):
    return lax.dot_general(a, b, NT_DIMS, preferred_element_type=F32)


def _dot_tn(a, b):
    return lax.dot_general(a, b, TN_DIMS, preferred_element_type=F32)


NT_GROUP_ROWS = 512


def _grouped_nt(w_ref, x):
    done = {}

    def rows(r0, r1):
        g = r0 // NT_GROUP_ROWS
        assert (r1 - 1) // NT_GROUP_ROWS == g
        if g not in done:
            done[g] = _dot_nt(w_ref[g * NT_GROUP_ROWS:(g + 1) * NT_GROUP_ROWS, :], x)
        return done[g][r0 - g * NT_GROUP_ROWS:r1 - g * NT_GROUP_ROWS]

    return rows


def _params(n_axes):
    return pltpu.CompilerParams(dimension_semantics=("arbitrary",) * n_axes,
                                vmem_limit_bytes=VMEM_LIMIT)


def _const_spec(shape):
    nd = len(shape)
    return pl.BlockSpec(shape, lambda *_: (0,) * nd, pipeline_mode=pl.Buffered(1))


def _mod_spec(per_batch):
    if per_batch:
        return pl.BlockSpec((None, N_MOD, D_MODEL), lambda b, t: (b, 0, 0))
    return pl.BlockSpec((None, N_MOD, D_MODEL), lambda b, t: (0, 0, 0))


def _tok_spec(tm):
    return pl.BlockSpec((None, tm, D_MODEL), lambda b, t: (b, t, 0))


def _ada_kernel(c_ref, w_ref, b_ref, o_ref):
    c = c_ref[...]
    a = (c * jax.nn.sigmoid(c)).astype(BF16)
    o_ref[...] = _dot(a, w_ref[...].astype(BF16)) + b_ref[...]


def _ada(cc, w, b):
    rows = cc.shape[0]
    n = w.shape[1]
    tn = 1152
    return pl.pallas_call(
        _ada_kernel,
        grid=(n // tn,),
        in_specs=[pl.BlockSpec((rows, D_MODEL), lambda j: (0, 0)),
                  pl.BlockSpec((D_MODEL, tn), lambda j: (0, j)),
                  pl.BlockSpec((1, tn), lambda j: (0, j))],
        out_specs=pl.BlockSpec((rows, tn), lambda j: (0, j)),
        out_shape=jax.ShapeDtypeStruct((rows, n), F32),
        compiler_params=_params(1),
        name="ada_mod",
    )(cc, w, b.reshape(1, n))


def _rope_table_kernel(inv_ref, userow_ref, sign_ref, cos_ref, sin_ref):
    t = lax.broadcasted_iota(jnp.int32, cos_ref.shape, 1)
    row = lax.shift_right_logical(t, GRID_W.bit_length() - 1).astype(F32)
    col = (t & (GRID_W - 1)).astype(F32)
    pos = jnp.where(userow_ref[...] > 0.5, row, col)
    ang = pos * inv_ref[...]
    sign = sign_ref[...]
    active = sign != 0.0
    cos_ref[...] = jnp.where(active, jnp.cos(ang), 1.0)
    sin_ref[...] = jnp.where(active, sign * jnp.sin(ang), 0.0)


def _rope_tables(seq, rot_dim, x1_slots, x2_slots):
    n_f = rot_dim // 4
    half = rot_dim // 2
    inv_f = (ROPE_BASE ** (-np.arange(n_f, dtype=np.float64) / n_f)).astype(np.float32)
    inv_half = np.concatenate([inv_f, inv_f])
    use_row_half = np.concatenate([np.ones(n_f), np.zeros(n_f)])
    inv = np.zeros(LANES, np.float32)
    use_row = np.zeros(LANES, np.float32)
    sign = np.zeros(LANES, np.float32)
    for slots, sgn in ((x1_slots, -1.0), (x2_slots, 1.0)):
        for start, n in slots:
            assert n == half
            inv[start:start + n] = inv_half
            use_row[start:start + n] = use_row_half
            sign[start:start + n] = sgn
    vec_shape = (LANES, 1)
    tab_shape = (LANES, seq)
    vec = pl.BlockSpec(vec_shape, lambda: (0, 0))
    tab = pl.BlockSpec(tab_shape, lambda: (0, 0))
    return pl.pallas_call(
        _rope_table_kernel,
        in_specs=[vec, vec, vec],
        out_specs=[tab, tab],
        out_shape=[jax.ShapeDtypeStruct(tab_shape, F32)] * 2,
        name="rope_table",
    )(*[jnp.asarray(v.reshape(vec_shape)) for v in (inv, use_row, sign)])


def _ffn_body(x, mod_ref, g, wg_ref, wu_ref, wd_ref, k):
    shift = mod_ref[3 * k:3 * k + 1, :]
    scale = mod_ref[3 * k + 1:3 * k + 2, :]
    gate = mod_ref[3 * k + 2:3 * k + 3, :]
    hn = _adaln(x, g, shift, scale).astype(BF16)
    gg = _dot(hn, wg_ref[...])
    u = _dot(hn, wu_ref[...])
    a = (gg * jax.nn.sigmoid(gg) * u).astype(BF16)
    return x + (0.5 * gate) * _dot(a, wd_ref[...])


FFN_SUB_ROWS = 256


def _ffn_kernel(x_ref, mod_ref, g_ref, wg_ref, wu_ref, wd_ref, o_ref, *, k):
    for r0 in range(0, x_ref.shape[0], FFN_SUB_ROWS):
        rows = slice(r0, r0 + FFN_SUB_ROWS)
        o_ref[rows, :] = _ffn_body(x_ref[rows, :], mod_ref, g_ref[...], wg_ref, wu_ref, wd_ref, k)


def _ffn(x, mod, per_batch, g, wg, wu, wd, k):
    bsz, n, _ = x.shape
    tm = min(FFN_TILE, n)
    return pl.pallas_call(
        functools.partial(_ffn_kernel, k=k),
        grid=(bsz, n // tm),
        in_specs=[_tok_spec(tm), _mod_spec(per_batch), _const_spec((1, D_MODEL)),
                  _const_spec((D_MODEL, D_FF)), _const_spec((D_MODEL, D_FF)),
                  _const_spec((D_FF, D_MODEL))],
        out_specs=_tok_spec(tm),
        out_shape=jax.ShapeDtypeStruct(x.shape, F32),
        compiler_params=_params(2),
        name="ffn",
    )(x, mod, g, wg, wu, wd)


def _attn_out_rows(ot_ref, w_ref, r0, n_rows):
    _, nh, dv, tq = ot_ref.shape
    if tq >= n_rows:
        c0 = r0 % tq
        a = ot_ref[r0 // tq, :, :, c0:c0 + n_rows].reshape(nh * dv, n_rows)
        return _dot_tn(a, w_ref[...])
    parts = [_dot_tn(ot_ref[j].reshape(nh * dv, tq), w_ref[...])
             for j in range(r0 // tq, (r0 + n_rows) // tq)]
    return jnp.concatenate(parts, axis=0)


def _out_ffn_kernel(*refs, n_attn):
    h_ref, mod_ref, g_ref = refs[:3]
    attn = refs[3:3 + 2 * n_attn]
    wg_ref, wu_ref, wd_ref, o_ref = refs[3 + 2 * n_attn:]
    for r0 in range(0, h_ref.shape[0], FFN_SUB_ROWS):
        rows = slice(r0, r0 + FFN_SUB_ROWS)
        y = None
        for a in range(n_attn):
            ya = _attn_out_rows(attn[2 * a], attn[2 * a + 1], r0, FFN_SUB_ROWS)
            y = ya if y is None else y + ya
        x = h_ref[rows, :] + mod_ref[5:6, :] * y
        o_ref[rows, :] = _ffn_body(x, mod_ref, g_ref[...], wg_ref, wu_ref, wd_ref, 2)


def _out_ffn(h, mod, per_batch, g, attn, wg, wu, wd):
    bsz, n, _ = h.shape
    tm = min(FFN_TILE, n)
    in_specs = [_tok_spec(tm), _mod_spec(per_batch), _const_spec((1, D_MODEL))]
    args = [h, mod, g]
    for o_t, w in attn:
        _, _, nh, dv, tq = o_t.shape
        assert tm % tq == 0
        in_specs += [pl.BlockSpec((None, tm // tq, nh, dv, tq), lambda b, t: (b, t, 0, 0, 0)),
                     _const_spec(w.shape)]
        args += [o_t, w]
    in_specs += [_const_spec((D_MODEL, D_FF)), _const_spec((D_MODEL, D_FF)), _const_spec((D_FF, D_MODEL))]
    args += [wg, wu, wd]
    return pl.pallas_call(
        functools.partial(_out_ffn_kernel, n_attn=len(attn)),
        grid=(bsz, n // tm),
        in_specs=in_specs,
        out_specs=_tok_spec(tm),
        out_shape=jax.ShapeDtypeStruct(h.shape, F32),
        compiler_params=_params(2),
        name="out_ffn",
    )(*args)


L0_ST_SQ = 0
L0_ST_SV = L0_ST_SQ + SWA_HEADS * LANES
L0_ST_KR = L0_ST_SV + SWA_KV_HEADS * SWA_HEAD_DIM
L0_ST_SK = L0_ST_KR + LANES
L0_ST_ROWS = L0_ST_SK + SWA_KV_HEADS * LANES


def _l0_proj_kernel(*refs, rope):
    (h_ref, mod_ref, g_ref, w0_ref, wst_ref, qag_ref, wqbt_ref, kvag_ref, wkvt_ref,
     mqg_ref, mkg_ref, sqg_ref, skg_ref) = refs[:13]
    if rope:
        cmt, smt, cst, sst = [r[...] for r in refs[13:17]]
        outs = refs[17:]
    else:
        outs = refs[13:]
    qm_ref, km_ref, vmt_ref, qs_ref, ks_ref, vst_ref = outs

    hn = _adaln(h_ref[...], g_ref[...], mod_ref[3:4, :], mod_ref[4:5, :]).astype(BF16)
    tm = hn.shape[0]
    p = _dot(hn, w0_ref[...])
    st = _grouped_nt(wst_ref, hn)

    mla_inv = 1.0 / (MLA_NOPE + MLA_ROPE)
    swa_inv = 1.0 / SWA_HEAD_DIM
    bcast = lambda v: jnp.broadcast_to(v, (LANES, tm))
    mqg = bcast(mqg_ref[...] * (mla_inv ** 0.5 * LOG2E))
    sqg = bcast(sqg_ref[...] * (swa_inv ** 0.5 * LOG2E))
    mkg = bcast(mkg_ref[...])
    skg = bcast(skg_ref[...])

    def mla_rope(t):
        return _rope_t(t, cmt, smt, MLA_ROPE_ROWS) if rope else t

    def swa_rope(t):
        return _rope_t(t, cst, sst, SWA_ROPE_ROWS) if rope else t

    for h in range(SWA_HEADS):
        t = swa_rope(_rms_t(st(L0_ST_SQ + h * LANES, L0_ST_SQ + (h + 1) * LANES), sqg, swa_inv))
        for i in range(tm // SWA_BLK):
            qs_ref[h, i] = t[:, i * SWA_BLK:(i + 1) * SWA_BLK].astype(BF16)
    for j in range(SWA_KV_HEADS):
        t = swa_rope(_rms_t(st(L0_ST_SK + j * LANES, L0_ST_SK + (j + 1) * LANES), skg, swa_inv))
        ks_ref[j] = t.T.astype(BF16)
        vt = st(L0_ST_SV + j * SWA_HEAD_DIM, L0_ST_SV + (j + 1) * SWA_HEAD_DIM)
        for i in range(tm // SWA_BLK):
            vst_ref[j, i] = vt[:, i * SWA_BLK:(i + 1) * SWA_BLK].astype(BF16)

    qa = _rms(p[:, 0:MLA_Q_RANK], qag_ref[...], 1.0 / MLA_Q_RANK).astype(BF16)
    qt = _grouped_nt(wqbt_ref, qa)
    for h in range(MLA_HEADS):
        t = mla_rope(_rms_t(qt(h * LANES, (h + 1) * LANES), mqg, mla_inv))
        tq = qm_ref.shape[-1]
        for j in range(tm // tq):
            qm_ref[h, j] = t[:, j * tq:(j + 1) * tq].astype(BF16)

    kva = _rms(p[:, MLA_Q_RANK:MLA_Q_RANK + MLA_KV_RANK], kvag_ref[...], 1.0 / MLA_KV_RANK).astype(BF16)
    kvt = _grouped_nt(wkvt_ref, kva)
    kr = st(L0_ST_KR, L0_ST_KR + LANES)
    for h in range(MLA_HEADS):
        t = mla_rope(_rms_t(kvt(h * LANES, (h + 1) * LANES) + kr, mkg, mla_inv))
        km_ref[h] = t.T.astype(BF16)
    v0 = MLA_HEADS * LANES
    for h in range(MLA_HEADS):
        vmt_ref[h] = kvt(v0 + h * MLA_V, v0 + (h + 1) * MLA_V).astype(BF16)


def _l0_proj(h, mod, per_batch, g, w, tables):
    bsz, n, _ = h.shape
    tm = min(TOK_TILE, n)
    rope = tables is not None
    weights = [g, w["w0"], w["wst"], w["qa_g"], w["wqbt"], w["kva_g"], w["wkvt"],
               w["mq_g"], w["mk_g"], w["sq_g"], w["sk_g"]]
    in_specs = [_tok_spec(tm), _mod_spec(per_batch)] + [_const_spec(a.shape) for a in weights]
    args = [h, mod] + weights
    if rope:
        in_specs += [pl.BlockSpec((LANES, tm), lambda b, t: (0, t))] * 4
        args += list(tables)
    k_spec = lambda nh: pl.BlockSpec((None, nh, tm, LANES), lambda b, t: (b, 0, t, 0))
    qt_spec = lambda tq: pl.BlockSpec((None, MLA_HEADS, tm // tq, LANES, tq), lambda b, t: (b, 0, t, 0, 0))
    tq = min(MLA_TQ, n)
    out_specs = [qt_spec(tq), k_spec(MLA_HEADS),
                 pl.BlockSpec((None, MLA_HEADS, MLA_V, tm), lambda b, t: (b, 0, 0, t)),
                 qt_spec(SWA_BLK), k_spec(SWA_KV_HEADS),
                 pl.BlockSpec((None, SWA_KV_HEADS, tm // SWA_BLK, SWA_HEAD_DIM, SWA_BLK),
                              lambda b, t: (b, 0, t, 0, 0))]
    out_shape = [jax.ShapeDtypeStruct((bsz, MLA_HEADS, n // tq, LANES, tq), BF16),
                 jax.ShapeDtypeStruct((bsz, MLA_HEADS, n, LANES), BF16),
                 jax.ShapeDtypeStruct((bsz, MLA_HEADS, MLA_V, n), BF16),
                 jax.ShapeDtypeStruct((bsz, SWA_HEADS, n // SWA_BLK, LANES, SWA_BLK), BF16),
                 jax.ShapeDtypeStruct((bsz, SWA_KV_HEADS, n, LANES), BF16),
                 jax.ShapeDtypeStruct((bsz, SWA_KV_HEADS, n // SWA_BLK, SWA_HEAD_DIM, SWA_BLK), BF16)]
    return pl.pallas_call(
        functools.partial(_l0_proj_kernel, rope=rope),
        grid=(bsz, n // tm),
        in_specs=in_specs, out_specs=out_specs, out_shape=out_shape,
        compiler_params=_params(2),
        name="l0_proj",
    )(*args)


def _interleave(*stages):
    live = list(stages)
    while live:
        for st in list(live):
            try:
                next(st)
            except StopIteration:
                live.remove(st)


def _two_stage_loop(n_items, produce, consume, slot0, slot1):
    _interleave(produce(0, slot0))

    def pair(j, carry):
        i = 2 * j
        _interleave(produce(i + 1, slot1), consume(i, slot0))
        _interleave(produce(jnp.minimum(i + 2, n_items - 1), slot0), consume(i + 1, slot1))
        return carry

    lax.fori_loop(0, n_items // 2, pair, 0)


def _chunk_fold(x, op):
    return op(x.reshape(x.shape[0] // SUBLANES, SUBLANES, x.shape[-1]), axis=0)


def _score_chunks(s_ref, m_ref, key_chunk, qt, chunks, bias=None, floor=None):
    mx = None
    for k0, size in chunks:
        c = _dot(key_chunk(k0, size), qt)
        b = None if bias is None else bias(k0, size)
        if b is not None:
            c = c + b
        s_ref[k0:k0 + size, :] = c
        part = _chunk_fold(c, jnp.max)
        mx = part if mx is None else jnp.maximum(mx, part)
        yield
    m = jnp.max(mx, axis=0, keepdims=True)
    m_ref[...] = m if floor is None else jnp.maximum(m, floor)


def _exp_pv_chunks(s_ref, m_ref, value_t, chunk, out):
    m = m_ref[...]
    l8 = None
    o = None
    for k0 in range(0, s_ref.shape[0], chunk):
        e = jnp.exp2(s_ref[k0:k0 + chunk, :] - m)
        part = _chunk_fold(e, jnp.sum)
        l8 = part if l8 is None else l8 + part
        d = _dot(value_t(k0), e.astype(BF16))
        o = d if o is None else o + d
        yield
    out.append((o, jnp.sum(l8, axis=0, keepdims=True)))


KEY_CHUNK = 256


def _ctx_then_latent(c_ref, l_ref, n_ctx, token_major):
    def chunk(k0, size=KEY_CHUNK):
        ref, k = (c_ref, k0) if k0 < n_ctx else (l_ref, k0 - n_ctx)
        return ref[k:k + size, :] if token_major else ref[:, k:k + size]
    return chunk


SCORE_CHUNK = KEY_CHUNK


def _score_key_chunks(n_ctx, n_keys):
    assert n_ctx <= SCORE_CHUNK
    return [(0, n_ctx)] + [(k0, min(SCORE_CHUNK, n_keys - k0)) for k0 in range(n_ctx, n_keys, SCORE_CHUNK)]


def _mla_attn_kernel(q_ref, kc_ref, vct_ref, kl_ref, vlt_ref, o_ref, s0_ref, m0_ref, s1_ref, m1_ref):
    n_tiles, n_heads = o_ref.shape[:2]
    n_ctx = kc_ref.shape[1]
    n_keys = s0_ref.shape[0]
    score_chunks = _score_key_chunks(n_ctx, n_keys)

    def split(i):
        return lax.div(i, n_tiles), lax.rem(i, n_tiles)

    def scores(i, slot):
        s_ref, m_ref = slot
        h, t = split(i)
        keys = _ctx_then_latent(kc_ref.at[h], kl_ref.at[h], n_ctx, True)
        yield from _score_chunks(s_ref, m_ref, keys, q_ref[h, t], score_chunks)

    def softmax_pv(i, slot):
        s_ref, m_ref = slot
        h, t = split(i)
        values = _ctx_then_latent(vct_ref.at[h], vlt_ref.at[h], n_ctx, False)
        res = []
        yield from _exp_pv_chunks(s_ref, m_ref, values, KEY_CHUNK, res)
        o, l = res[0]
        o_ref[t, h] = (o * (1.0 / l)).astype(BF16)

    _two_stage_loop(n_tiles * n_heads, scores, softmax_pv, (s0_ref, m0_ref), (s1_ref, m1_ref))


def _mla_attn(q, kc, vct, kl, vlt):
    bsz, nh, n_tiles = q.shape[:3]
    hs = ATT_HEADS_PER_STEP
    n_keys = kc.shape[2] + kl.shape[2]
    grp = lambda shape: pl.BlockSpec((None, hs) + tuple(shape), lambda b, g: (b, g) + (0,) * len(shape))
    tq = q.shape[-1]
    slot = [pltpu.VMEM((n_keys, tq), F32), pltpu.VMEM((1, tq), F32)]
    return pl.pallas_call(
        _mla_attn_kernel,
        grid=(bsz, nh // hs),
        in_specs=[grp(q.shape[2:]), grp(kc.shape[2:]), grp(vct.shape[2:]), grp(kl.shape[2:]), grp(vlt.shape[2:])],
        out_specs=pl.BlockSpec((None, n_tiles, hs, MLA_V, tq), lambda b, g: (b, 0, g, 0, 0)),
        out_shape=jax.ShapeDtypeStruct((bsz, n_tiles, nh, MLA_V, tq), BF16),
        scratch_shapes=slot + slot,
        compiler_params=_params(2),
        name="mla_attn",
    )(q, kc, vct, kl, vlt)


def _mla_ctx_attn_kernel(q_ref, kc_ref, vct_ref, o_ref):
    for h in range(o_ref.shape[0]):
        sc = _dot(kc_ref[h], q_ref[h])
        pc = jnp.exp2(sc - jnp.max(sc, axis=0, keepdims=True))
        l = jnp.sum(pc, axis=0, keepdims=True)
        o = _dot(vct_ref[h], pc.astype(BF16))
        o_ref[h] = (o * (1.0 / l)).astype(BF16)


def _mla_ctx_attn(q, kc, vct):
    bsz, nh, n_tiles = q.shape[:3]
    assert n_tiles == 1
    tq = q.shape[-1]
    per_b = lambda shape: pl.BlockSpec((None,) + tuple(shape), lambda b: (b,) + (0,) * len(shape))
    return pl.pallas_call(
        _mla_ctx_attn_kernel,
        grid=(bsz,),
        in_specs=[pl.BlockSpec((None, nh, None, LANES, tq), lambda b: (b, 0, 0, 0, 0)),
                  per_b(kc.shape[1:]), per_b(vct.shape[1:])],
        out_specs=pl.BlockSpec((None, None, nh, MLA_V, tq), lambda b: (b, 0, 0, 0, 0)),
        out_shape=jax.ShapeDtypeStruct((bsz, 1, nh, MLA_V, tq), BF16),
        compiler_params=_params(1),
        name="mla_ctx_attn",
    )(q, kc, vct)


SWA_BAND = 3 * SWA_BLK
SWA_NQ = SWA_GROUP * SWA_BLK


def _sink_row(sink_ref, j):
    return jnp.concatenate(
        [jnp.full((1, SWA_BLK), sink_ref[j * SWA_GROUP + g] * LOG2E, F32) for g in range(SWA_GROUP)], axis=1)


def _swa_attn_kernel(sink_ref, q_ref, kc_ref, vct_ref, kl_ref, vlt_ref, o_ref,
                     bias_ref, s0_ref, m0_ref, s1_ref, m1_ref):
    n_blk = o_ref.shape[0]
    n_ctx = kc_ref.shape[1]

    @pl.when(pl.program_id(0) == 0)
    def _():
        r = lax.broadcasted_iota(jnp.int32, (SWA_BAND, SWA_NQ), 0)
        c = lax.broadcasted_iota(jnp.int32, (SWA_BAND, SWA_NQ), 1) & (SWA_BLK - 1)
        for case in range(3):
            dist = r - c - case * SWA_BLK
            bias_ref[case] = jnp.where(jnp.abs(dist) <= SWA_WINDOW, 0.0, MASKED)

    def split(i):
        return lax.div(i, n_blk), lax.rem(i, n_blk)

    def band_start(n):
        return jnp.clip(n - 1, 0, n_blk - 3)

    def scores(i, slot):
        s_ref, m_ref = slot
        j, n = split(i)
        qt = jnp.concatenate([q_ref[j * SWA_GROUP + g, n] for g in range(SWA_GROUP)], axis=1)
        band0 = band_start(n) * SWA_BLK
        case = (n > 0).astype(jnp.int32) + (n == n_blk - 1).astype(jnp.int32)

        def keys(k0, size):
            if k0 < n_ctx:
                return kc_ref[j, k0:k0 + size, :]
            return kl_ref[j, pl.ds(pl.multiple_of(band0 + (k0 - n_ctx), SWA_BLK), size), :]

        def bias(k0, size):
            return None if k0 < n_ctx else bias_ref[case, k0 - n_ctx:k0 - n_ctx + size, :]

        chunks = [(0, n_ctx), (n_ctx, SWA_BAND)]
        yield from _score_chunks(s_ref, m_ref, keys, qt, chunks, bias, _sink_row(sink_ref, j))

    def softmax_pv(i, slot):
        s_ref, m_ref = slot
        j, n = split(i)
        b0 = band_start(n)

        def value_t(k0):
            if k0 < n_ctx:
                return vct_ref[j, :, k0:k0 + SWA_BLK]
            return vlt_ref[j, b0 + (k0 - n_ctx) // SWA_BLK]

        res = []
        yield from _exp_pv_chunks(s_ref, m_ref, value_t, SWA_BLK, res)
        o, l = res[0]
        o = o * (1.0 / (l + jnp.exp2(_sink_row(sink_ref, j) - m_ref[...])))
        for g in range(SWA_GROUP):
            o_ref[n, j * SWA_GROUP + g] = o[:, g * SWA_BLK:(g + 1) * SWA_BLK].astype(BF16)

    _two_stage_loop(SWA_KV_HEADS * n_blk, scores, softmax_pv, (s0_ref, m0_ref), (s1_ref, m1_ref))


def _swa_attn(sink, q, kc, vct, kl, vlt):
    bsz, nh, n_blk = q.shape[:3]
    n_keys = kc.shape[2] + SWA_BAND
    full = lambda a: pl.BlockSpec((None,) + a.shape[1:], lambda b: (b,) + (0,) * (a.ndim - 1))
    slot = [pltpu.VMEM((n_keys, SWA_NQ), F32), pltpu.VMEM((1, SWA_NQ), F32)]
    return pl.pallas_call(
        _swa_attn_kernel,
        grid=(bsz,),
        in_specs=[pl.BlockSpec(memory_space=pltpu.SMEM), full(q), full(kc), full(vct), full(kl), full(vlt)],
        out_specs=pl.BlockSpec((None, n_blk, nh, SWA_HEAD_DIM, SWA_BLK), lambda b: (b, 0, 0, 0, 0)),
        out_shape=jax.ShapeDtypeStruct((bsz, n_blk, nh, SWA_HEAD_DIM, SWA_BLK), BF16),
        scratch_shapes=[pltpu.VMEM((3, SWA_BAND, SWA_NQ), F32)] + slot + slot,
        compiler_params=_params(1),
        name="swa_attn",
    )(sink, q, kc, vct, kl, vlt)


def _swa_ctx_attn_kernel(sink_ref, q_ref, kc_ref, vct_ref, o_ref):
    sink = _sink_row(sink_ref, pl.program_id(1))
    for n in range(o_ref.shape[0]):
        qt = jnp.concatenate([q_ref[g, n] for g in range(SWA_GROUP)], axis=1)
        s = _dot(kc_ref[...], qt)
        m = jnp.maximum(jnp.max(s, axis=0, keepdims=True), sink)
        p = jnp.exp2(s - m)
        l = jnp.sum(p, axis=0, keepdims=True) + jnp.exp2(sink - m)
        o = _dot(vct_ref[...], p.astype(BF16)) * (1.0 / l)
        for g in range(SWA_GROUP):
            o_ref[n, g] = o[:, g * SWA_BLK:(g + 1) * SWA_BLK].astype(BF16)


def _swa_ctx_attn(sink, q, kc, vct):
    bsz, nh, n_blk = q.shape[:3]
    return pl.pallas_call(
        _swa_ctx_attn_kernel,
        grid=(bsz, SWA_KV_HEADS),
        in_specs=[pl.BlockSpec(memory_space=pltpu.SMEM),
                  pl.BlockSpec((None, SWA_GROUP, n_blk, LANES, SWA_BLK), lambda b, j: (b, j, 0, 0, 0)),
                  pl.BlockSpec((None, None) + kc.shape[2:], lambda b, j: (b, j, 0, 0)),
                  pl.BlockSpec((None, None) + vct.shape[2:], lambda b, j: (b, j, 0, 0))],
        out_specs=pl.BlockSpec((None, n_blk, SWA_GROUP, SWA_HEAD_DIM, SWA_BLK), lambda b, j: (b, 0, j, 0, 0)),
        out_shape=jax.ShapeDtypeStruct((bsz, n_blk, nh, SWA_HEAD_DIM, SWA_BLK), BF16),
        compiler_params=_params(2),
        name="swa_ctx_attn",
    )(sink, q, kc, vct)


DIFF_Q = HALF // 2


def _diff_norm_t(t, g):
    sq = t * t
    grp = [jnp.sum(sq[i * DIFF_Q:(i + 1) * DIFF_Q], axis=0, keepdims=True) for i in range(4)]
    inv = 1.0 / DIFF_HEAD_DIM
    r0 = lax.rsqrt((grp[0] + grp[2]) * inv + EPS)
    r1 = lax.rsqrt((grp[1] + grp[3]) * inv + EPS)
    parts = [t[i * DIFF_Q:(i + 1) * DIFF_Q] * (r0 if i % 2 == 0 else r1) for i in range(4)]
    return jnp.concatenate(parts, axis=0) * g


def _l1_proj_kernel(*refs, latent):
    if latent:
        h_ref, mod_ref, g_ref, wt_ref, qg_ref, kg_ref, cdt_ref, sdt_ref, q_ref, k_ref, vt_ref = refs
        cdt, sdt = cdt_ref[...], sdt_ref[...]
    else:
        h_ref, mod_ref, g_ref, wt_ref, kg_ref, k_ref, vt_ref = refs
    hn = _adaln(h_ref[...], g_ref[...], mod_ref[3:4, :], mod_ref[4:5, :]).astype(BF16)
    tm = hn.shape[0]
    width = DIFF_HEADS * LANES
    ft = _grouped_nt(wt_ref, hn)
    bcast = lambda v: jnp.broadcast_to(v, (LANES, tm))
    k0 = 0
    if latent:
        qg = bcast(qg_ref[...] * (DIFF_HEAD_DIM ** -0.5 * LOG2E))
        for h in range(DIFF_HEADS):
            t = _rope_t(_diff_norm_t(ft(h * LANES, (h + 1) * LANES), qg), cdt, sdt, DIFF_ROPE_ROWS)
            tq = q_ref.shape[-1]
            for j in range(tm // tq):
                q_ref[h, j] = t[:, j * tq:(j + 1) * tq].astype(BF16)
        k0 = width

    kg = bcast(kg_ref[...])
    for h in range(DIFF_HEADS):
        t = _diff_norm_t(ft(k0 + h * LANES, k0 + (h + 1) * LANES), kg)
        if latent:
            t = _rope_t(t, cdt, sdt, DIFF_ROPE_ROWS)
        k_ref[h] = t.T.astype(BF16)
    v0 = k0 + width
    for h in range(DIFF_HEADS):
        vt_ref[h] = ft(v0 + h * LANES, v0 + (h + 1) * LANES).astype(BF16)


def _l1_proj(h, mod, per_batch, g, w, tables):
    bsz, n, _ = h.shape
    tm = min(TOK_TILE, n)
    latent = tables is not None
    k_spec = pl.BlockSpec((None, DIFF_HEADS, tm, LANES), lambda b, t: (b, 0, t, 0))
    vt_spec = pl.BlockSpec((None, DIFF_HEADS, LANES, tm), lambda b, t: (b, 0, 0, t))
    k_shape = jax.ShapeDtypeStruct((bsz, DIFF_HEADS, n, LANES), BF16)
    vt_shape = jax.ShapeDtypeStruct((bsz, DIFF_HEADS, LANES, n), BF16)
    if latent:
        weights = [g, w["wqkvt"], w["q_g"], w["k_g"]]
        in_specs = ([_tok_spec(tm), _mod_spec(per_batch)] + [_const_spec(a.shape) for a in weights]
                    + [pl.BlockSpec((LANES, tm), lambda b, t: (0, t))] * 2)
        args = [h, mod] + weights + list(tables)
        tq = min(DIFF_TQ, n)
        out_specs = [pl.BlockSpec((None, DIFF_HEADS, tm // tq, LANES, tq), lambda b, t: (b, 0, t, 0, 0)),
                     k_spec, vt_spec]
        out_shape = [jax.ShapeDtypeStruct((bsz, DIFF_HEADS, n // tq, LANES, tq), BF16), k_shape, vt_shape]
    else:
        weights = [g, w["wkvt"], w["k_g"]]
        in_specs = [_tok_spec(tm), _mod_spec(per_batch)] + [_const_spec(a.shape) for a in weights]
        args = [h, mod] + weights
        out_specs = [k_spec, vt_spec]
        out_shape = [k_shape, vt_shape]
    return pl.pallas_call(
        functools.partial(_l1_proj_kernel, latent=latent),
        grid=(bsz, n // tm),
        in_specs=in_specs, out_specs=out_specs, out_shape=out_shape,
        compiler_params=_params(2),
        name="l1_proj",
    )(*args)


def _diff_attn_kernel(lam_ref, subg_ref, q_ref, kc_ref, vct_ref, kl_ref, vlt_ref, o_ref,
                      s0_ref, m0_ref, s1_ref, m1_ref):
    n_tiles, n_heads = o_ref.shape[:2]
    n_ctx = kc_ref.shape[1]
    n_keys = s0_ref.shape[1]
    score_chunks = _score_key_chunks(n_ctx, n_keys)
    feature = lax.broadcasted_iota(jnp.int32, q_ref.shape[2:], 0)
    sub0 = (feature & (HALF - 1)) < DIFF_Q
    lv = lam_ref[...]
    lam = (jnp.exp(jnp.sum(lv[0:1] * lv[1:2], axis=-1, keepdims=True))
           - jnp.exp(jnp.sum(lv[2:3] * lv[3:4], axis=-1, keepdims=True)) + LAMBDA_INIT_L1)
    subg = subg_ref[...] * (1.0 - LAMBDA_INIT_L1)

    def split(i):
        return lax.div(i, n_tiles), lax.rem(i, n_tiles)

    def scores(i, slot):
        s_ref, m_ref = slot
        h, t = split(i)
        qt = q_ref[h, t]
        zero = jnp.zeros_like(qt)
        keys = _ctx_then_latent(kc_ref.at[h], kl_ref.at[h], n_ctx, True)
        for s, qs in enumerate((jnp.where(sub0, qt, zero), jnp.where(sub0, zero, qt))):
            yield from _score_chunks(s_ref.at[s], m_ref.at[s], keys, qs, score_chunks)

    def softmax_pv(i, slot):
        s_ref, m_ref = slot
        h, t = split(i)
        values = _ctx_then_latent(vct_ref.at[h], vlt_ref.at[h], n_ctx, False)
        res = []
        for s in range(2):
            yield from _exp_pv_chunks(s_ref.at[s], m_ref.at[s], values, KEY_CHUNK, res)
        (o0, l0), (o1, l1) = res
        o = o0 * (1.0 / l0) - o1 * (lam * (1.0 / l1))
        ms = jnp.mean(o * o, axis=0, keepdims=True)
        o_ref[t, h] = (o * lax.rsqrt(ms + EPS) * subg).astype(BF16)

    _two_stage_loop(n_tiles * n_heads, scores, softmax_pv, (s0_ref, m0_ref), (s1_ref, m1_ref))


def _diff_attn(lamv, subg, q, kc, vct, kl, vlt):
    bsz, nh, n_tiles = q.shape[:3]
    hs = ATT_HEADS_PER_STEP
    n_keys = kc.shape[2] + kl.shape[2]
    grp = lambda shape: pl.BlockSpec((None, hs) + tuple(shape), lambda b, g: (b, g) + (0,) * len(shape))
    tq = q.shape[-1]
    slot = [pltpu.VMEM((2, n_keys, tq), F32), pltpu.VMEM((2, 1, tq), F32)]
    return pl.pallas_call(
        _diff_attn_kernel,
        grid=(bsz, nh // hs),
        in_specs=[_const_spec(lamv.shape), _const_spec(subg.shape),
                  grp(q.shape[2:]), grp(kc.shape[2:]), grp(vct.shape[2:]), grp(kl.shape[2:]), grp(vlt.shape[2:])],
        out_specs=pl.BlockSpec((None, n_tiles, hs, LANES, tq), lambda b, g: (b, 0, g, 0, 0)),
        out_shape=jax.ShapeDtypeStruct((bsz, n_tiles, nh, LANES, tq), BF16),
        scratch_shapes=slot + slot,
        compiler_params=_params(2),
        name="diff_attn",
    )(lamv, subg, q, kc, vct, kl, vlt)


def _prep_l0(w_in, qa_g, wqb, kva_g, wkvb, q_g, k_g, sq_g, sk_g, w_out):
    mla = _mla_lane_map()
    swa = _swa_lane_map()
    kr_cols = np.where(mla >= MLA_NOPE, 384 + mla - MLA_NOPE, -1)
    wst = jnp.concatenate([
        _take_cols(w_in, _per_head(swa, SWA_HEADS, SWA_HEAD_DIM, base=416)),
        w_in[:, 1056:1184],
        _take_cols(w_in, kr_cols),
        _take_cols(w_in, _per_head(swa, SWA_KV_HEADS, SWA_HEAD_DIM, base=928)),
    ], axis=1).T
    assert wst.shape[0] == L0_ST_ROWS
    nope_map = np.where((mla >= 0) & (mla < MLA_NOPE), mla, -1)
    v_cols = np.concatenate([h * (MLA_NOPE + MLA_V) + MLA_NOPE + np.arange(MLA_V) for h in range(MLA_HEADS)])
    wkvt = jnp.concatenate([
        _take_cols(wkvb, _per_head(nope_map, MLA_HEADS, MLA_NOPE + MLA_V)),
        jnp.take(wkvb, jnp.asarray(v_cols, jnp.int32), axis=1),
    ], axis=1).T
    row = lambda v: v.reshape(1, -1).astype(F32)
    col = lambda v: v.reshape(-1, 1).astype(F32)
    return {
        "w0": w_in[:, 0:MLA_Q_RANK + MLA_KV_RANK].astype(BF16),
        "wst": wst.astype(BF16),
        "qa_g": row(qa_g),
        "wqbt": _take_cols(wqb, _per_head(mla, MLA_HEADS, MLA_NOPE + MLA_ROPE)).T.astype(BF16),
        "kva_g": row(kva_g),
        "wkvt": wkvt.astype(BF16),
        "mq_g": col(_take_cols(q_g, mla)),
        "mk_g": col(_take_cols(k_g, mla)),
        "sq_g": col(_take_cols(sq_g, swa)),
        "sk_g": col(_take_cols(sk_g, swa)),
        "wa": w_out[:MLA_HEADS * MLA_V].astype(BF16),
        "wb": w_out[MLA_HEADS * MLA_V:].astype(BF16),
    }


def _prep_l1(w_in, q_g, k_g, w_out):
    dm = _diff_lane_map()
    width = DIFF_HEADS * LANES
    wqt = _take_cols(w_in, _per_head(dm, DIFF_HEADS, LANES)).T
    wkt = _take_cols(w_in, _per_head(dm, DIFF_HEADS, LANES, base=width)).T
    wvt = w_in[:, 2 * width:].T
    return {
        "wqkvt": jnp.concatenate([wqt, wkt, wvt], axis=0).astype(BF16),
        "wkvt": jnp.concatenate([wkt, wvt], axis=0).astype(BF16),
        "q_g": _take_cols(q_g, dm % DIFF_HEAD_DIM).reshape(-1, 1).astype(F32),
        "k_g": _take_cols(k_g, dm % DIFF_HEAD_DIM).reshape(-1, 1).astype(F32),
        "w_out": w_out.astype(BF16),
    }


def kernel(x, c, ctx, c_ctx, l0_ada_w, l0_ada_b, l0_norm_g, l0_ffn_wg, l0_ffn_wu, l0_ffn_wd, l0_w_in, l0_mla_qa_g, l0_mla_wqb, l0_mla_kva_g, l0_mla_wkvb, l0_mla_q_g, l0_mla_k_g, l0_swa_q_g, l0_swa_k_g, l0_swa_sink, l0_w_out, l1_ada_w, l1_ada_b, l1_norm_g, l1_ffn_wg, l1_ffn_wu, l1_ffn_wd, l1_w_in, l1_q_g, l1_k_g, l1_lambda_q1, l1_lambda_k1, l1_lambda_q2, l1_lambda_k2, l1_subln_g, l1_w_out):
    bsz, seq, _ = x.shape

    pad = (-(bsz + 1)) % SUBLANES
    cc = jnp.concatenate([c, c_ctx[None, :], jnp.zeros((pad, D_MODEL), F32)], axis=0)

    def mods(ada_w, ada_b):
        m = _ada(cc, ada_w, ada_b).reshape(cc.shape[0], N_MOD, D_MODEL)
        return m[:bsz], m[bsz:bsz + 1]

    def ffn_weights(wg, wu, wd, i):
        return wg[i].astype(BF16), wu[i].astype(BF16), wd[i].astype(BF16)

    mla_tab = _rope_tables(seq, MLA_ROPE, [(0, 16)], [(64, 16)])
    swa_tab = _rope_tables(seq, SWA_HEAD_DIM, [(0, 32)], [(64, 32)])
    diff_tab = _rope_tables(seq, DIFF_HEAD_DIM, [(0, 32), (32, 32)], [(64, 32), (96, 32)])

    mod, mod_c = mods(l0_ada_w, l0_ada_b)
    g = [l0_norm_g[i:i + 1] for i in range(3)]
    w = _prep_l0(l0_w_in, l0_mla_qa_g, l0_mla_wqb, l0_mla_kva_g, l0_mla_wkvb, l0_mla_q_g, l0_mla_k_g,
                 l0_swa_q_g, l0_swa_k_g, l0_w_out)
    f0 = ffn_weights(l0_ffn_wg, l0_ffn_wu, l0_ffn_wd, 0)
    f1 = ffn_weights(l0_ffn_wg, l0_ffn_wu, l0_ffn_wd, 1)

    h = _ffn(x, mod, True, g[0], *f0, k=0)
    hc = _ffn(ctx, mod_c, False, g[0], *f0, k=0)

    qm, km, vmt, qs, ks, vst = _l0_proj(h, mod, True, g[1], w, mla_tab + swa_tab)
    qm_c, km_c, vmt_c, qs_c, ks_c, vst_c = _l0_proj(hc, mod_c, False, g[1], w, None)
    vsc_t = vst_c.transpose(0, 1, 3, 2, 4).reshape(bsz, SWA_KV_HEADS, SWA_HEAD_DIM, -1)

    at = _mla_attn(qm, km_c, vmt_c, km, vmt)
    bt = _swa_attn(l0_swa_sink, qs, ks_c, vsc_t, ks, vst)
    at_c = _mla_ctx_attn(qm_c, km_c, vmt_c)
    bt_c = _swa_ctx_attn(l0_swa_sink, qs_c, ks_c, vsc_t)

    h = _out_ffn(h, mod, True, g[2], [(at, w["wa"]), (bt, w["wb"])], *f1)
    hc = _out_ffn(hc, mod_c, False, g[2], [(at_c, w["wa"]), (bt_c, w["wb"])], *f1)

    mod, mod_c = mods(l1_ada_w, l1_ada_b)
    g = [l1_norm_g[i:i + 1] for i in range(3)]
    w = _prep_l1(l1_w_in, l1_q_g, l1_k_g, l1_w_out)
    f0 = ffn_weights(l1_ffn_wg, l1_ffn_wu, l1_ffn_wd, 0)
    f1 = ffn_weights(l1_ffn_wg, l1_ffn_wu, l1_ffn_wd, 1)

    h = _ffn(h, mod, True, g[0], *f0, k=0)
    hc = _ffn(hc, mod_c, False, g[0], *f0, k=0)

    qd, kd, vdt = _l1_proj(h, mod, True, g[1], w, diff_tab)
    kd_c, vdt_c = _l1_proj(hc, mod_c, False, g[1], w, None)
    lamv = jnp.stack([l1_lambda_q1, l1_lambda_k1, l1_lambda_q2, l1_lambda_k2]).astype(F32)
    ot = _diff_attn(lamv, l1_subln_g.reshape(-1, 1).astype(F32), qd, kd_c, vdt_c, kd, vdt)
    return _out_ffn(h, mod, True, g[2], [(ot, w["w_out"])], *f1)
```

```python
import functools
import math

import numpy as np
import jax
import jax.numpy as jnp
from jax import lax
from jax.experimental import pallas as pl
from jax.experimental.pallas import tpu as pltpu

F32 = jnp.float32
BF16 = jnp.bfloat16

D_MODEL = 1024
D_FF = 2816
N_MOD = 9
EPS = 1e-6
ROPE_BASE = 10000.0
GRID_W = 64
LOG2E = 1.4426950408889634

MLA_HEADS = 8
MLA_Q_RANK = 256
MLA_KV_RANK = 128
MLA_NOPE = 64
MLA_ROPE = 32
MLA_V = 64
SWA_HEADS = 8
SWA_KV_HEADS = 2
SWA_GROUP = SWA_HEADS // SWA_KV_HEADS
SWA_HEAD_DIM = 64
SWA_WINDOW = 128
DIFF_HEADS = 8
DIFF_HEAD_DIM = 64
LAMBDA_INIT_L1 = 0.8 - 0.6 * math.exp(-0.3 * 1)

LANES = 128
SUBLANES = 8
HALF = LANES // 2
MLA_TQ = 512
DIFF_TQ = 256
SWA_BLK = 128
TOK_TILE = 512
FFN_TILE = 1024
ATT_HEADS_PER_STEP = 8
VMEM_LIMIT = 56 * 1024 * 1024
MASKED = -1e30

NT_DIMS = (((1,), (1,)), ((), ()))
TN_DIMS = (((0,), (0,)), ((), ()))


def _mla_lane_map():
    m = np.full(LANES, -1, np.int64)
    m[0:16] = 64 + np.arange(16)
    m[16:48] = np.arange(32)
    m[64:80] = 80 + np.arange(16)
    m[80:112] = 32 + np.arange(32)
    return m


def _swa_lane_map():
    m = np.full(LANES, -1, np.int64)
    m[0:SWA_HEAD_DIM] = np.arange(SWA_HEAD_DIM)
    return m


def _diff_lane_map():
    m = np.zeros(LANES, np.int64)
    m[0:32] = np.arange(32)
    m[32:64] = 64 + np.arange(32)
    m[64:96] = 32 + np.arange(32)
    m[96:128] = 96 + np.arange(32)
    return m


MLA_ROPE_ROWS = MLA_ROPE // 2
SWA_ROPE_ROWS = SWA_HEAD_DIM // 2
DIFF_ROPE_ROWS = HALF


def _take_cols(w, idx):
    idx = np.asarray(idx)
    out = jnp.take(w, jnp.asarray(np.where(idx < 0, 0, idx), jnp.int32), axis=-1)
    return jnp.where(jnp.asarray(idx >= 0), out, jnp.zeros((), w.dtype))


def _cols_as_rows(w, idx):
    idx = np.asarray(idx)
    wt = jnp.concatenate([w.astype(BF16).T, jnp.zeros((1, w.shape[0]), BF16)], axis=0)
    return jnp.take(wt, jnp.asarray(np.where(idx < 0, w.shape[1], idx), jnp.int32), axis=0)


def _per_head(lane_map, n_heads, stride, base=0):
    cols = [np.where(lane_map >= 0, base + h * stride + lane_map, -1) for h in range(n_heads)]
    return np.concatenate(cols)


def _adaln(x, g, shift, scale):
    ms = jnp.mean(x * x, axis=-1, keepdims=True)
    return x * lax.rsqrt(ms + EPS) * g * (1.0 + scale) + shift


def _rms(t, g, inv_dim):
    ms = jnp.sum(t * t, axis=-1, keepdims=True) * inv_dim
    return t * lax.rsqrt(ms + EPS) * g


def _rms_t(t, g, inv_dim):
    ms = jnp.sum(t * t, axis=0, keepdims=True) * inv_dim
    return t * lax.rsqrt(ms + EPS) * g


def _rope_t(t, cos, sin, rows, partner=HALF):
    a, b = t[0:rows], t[partner:partner + rows]
    a2 = a * cos[0:rows] + b * sin[0:rows]
    b2 = b * cos[partner:partner + rows] + a * sin[partner:partner + rows]
    parts = [a2, t[rows:partner], b2, t[partner + rows:]]
    return jnp.concatenate([p for p in parts if p.shape[0] > 0], axis=0)


def _dot(a, b):
    return jnp.dot(a, b, preferred_element_type=F32)


def _dot_nt(a, b):
    return lax.dot_general(a, b, NT_DIMS, preferred_element_type=F32)


def _dot_tn(a, b):
    return lax.dot_general(a, b, TN_DIMS, preferred_element_type=F32)


NT_GROUP_ROWS = 512


def _grouped_nt(w_ref, x):
    done = {}

    def rows(r0, r1):
        g = r0 // NT_GROUP_ROWS
        assert (r1 - 1) // NT_GROUP_ROWS == g
        if g not in done:
            done[g] = _dot_nt(w_ref[g * NT_GROUP_ROWS:(g + 1) * NT_GROUP_ROWS, :], x)
        return done[g][r0 - g * NT_GROUP_ROWS:r1 - g * NT_GROUP_ROWS]

    return rows


def _params(n_axes):
    return pltpu.CompilerParams(dimension_semantics=("arbitrary",) * n_axes,
                                vmem_limit_bytes=VMEM_LIMIT)


def _const_spec(shape):
    nd = len(shape)
    return pl.BlockSpec(shape, lambda *_: (0,) * nd, pipeline_mode=pl.Buffered(1))


def _mod_spec(per_batch):
    if per_batch:
        return pl.BlockSpec((None, N_MOD, D_MODEL), lambda b, t: (b, 0, 0))
    return pl.BlockSpec((None, N_MOD, D_MODEL), lambda b, t: (0, 0, 0))


def _tok_spec(tm):
    return pl.BlockSpec((None, tm, D_MODEL), lambda b, t: (b, t, 0))


def _ada_kernel(c_ref, w_ref, b_ref, o_ref):
    c = c_ref[...]
    a = (c * jax.nn.sigmoid(c)).astype(BF16)
    o_ref[...] = _dot(a, w_ref[...].astype(BF16)) + b_ref[...]


def _ada(cc, w, b):
    rows = cc.shape[0]
    n = w.shape[1]
    tn = 1152
    return pl.pallas_call(
        _ada_kernel,
        grid=(n // tn,),
        in_specs=[pl.BlockSpec((rows, D_MODEL), lambda j: (0, 0)),
                  pl.BlockSpec((D_MODEL, tn), lambda j: (0, j)),
                  pl.BlockSpec((1, tn), lambda j: (0, j))],
        out_specs=pl.BlockSpec((rows, tn), lambda j: (0, j)),
        out_shape=jax.ShapeDtypeStruct((rows, n), F32),
        compiler_params=_params(1),
        name="ada_mod",
    )(cc, w, b.reshape(1, n))


def _rope_table_kernel(inv_ref, userow_ref, sign_ref, cos_ref, sin_ref):
    t = lax.broadcasted_iota(jnp.int32, cos_ref.shape, 1)
    row = lax.shift_right_logical(t, GRID_W.bit_length() - 1).astype(F32)
    col = (t & (GRID_W - 1)).astype(F32)
    pos = jnp.where(userow_ref[...] > 0.5, row, col)
    ang = pos * inv_ref[...]
    sign = sign_ref[...]
    active = sign != 0.0
    cos_ref[...] = jnp.where(active, jnp.cos(ang), 1.0)
    sin_ref[...] = jnp.where(active, sign * jnp.sin(ang), 0.0)


def _rope_tables(seq, rot_dim, x1_slots, x2_slots):
    n_f = rot_dim // 4
    half = rot_dim // 2
    inv_f = (ROPE_BASE ** (-np.arange(n_f, dtype=np.float64) / n_f)).astype(np.float32)
    inv_half = np.concatenate([inv_f, inv_f])
    use_row_half = np.concatenate([np.ones(n_f), np.zeros(n_f)])
    inv = np.zeros(LANES, np.float32)
    use_row = np.zeros(LANES, np.float32)
    sign = np.zeros(LANES, np.float32)
    for slots, sgn in ((x1_slots, -1.0), (x2_slots, 1.0)):
        for start, n in slots:
            assert n == half
            inv[start:start + n] = inv_half
            use_row[start:start + n] = use_row_half
            sign[start:start + n] = sgn
    vec_shape = (LANES, 1)
    tab_shape = (LANES, seq)
    vec = pl.BlockSpec(vec_shape, lambda: (0, 0))
    tab = pl.BlockSpec(tab_shape, lambda: (0, 0))
    return pl.pallas_call(
        _rope_table_kernel,
        in_specs=[vec, vec, vec],
        out_specs=[tab, tab],
        out_shape=[jax.ShapeDtypeStruct(tab_shape, F32)] * 2,
        name="rope_table",
    )(*[jnp.asarray(v.reshape(vec_shape)) for v in (inv, use_row, sign)])


def _ffn_body(x, mod_ref, g, wg_ref, wu_ref, wd_ref, k):
    shift = mod_ref[3 * k:3 * k + 1, :]
    scale = mod_ref[3 * k + 1:3 * k + 2, :]
    gate = mod_ref[3 * k + 2:3 * k + 3, :]
    hn = _adaln(x, g, shift, scale).astype(BF16)
    gg = _dot(hn, wg_ref[...])
    u = _dot(hn, wu_ref[...])
    a = (gg * jax.nn.sigmoid(gg) * u).astype(BF16)
    return x + (0.5 * gate) * _dot(a, wd_ref[...])


FFN_SUB_ROWS = 256


def _ffn_kernel(x_ref, mod_ref, g_ref, wg_ref, wu_ref, wd_ref, o_ref, *, k):
    for r0 in range(0, x_ref.shape[0], FFN_SUB_ROWS):
        rows = slice(r0, r0 + FFN_SUB_ROWS)
        o_ref[rows, :] = _ffn_body(x_ref[rows, :], mod_ref, g_ref[...], wg_ref, wu_ref, wd_ref, k)


def _ffn(x, mod, per_batch, g, wg, wu, wd, k):
    bsz, n, _ = x.shape
    tm = min(FFN_TILE, n)
    return pl.pallas_call(
        functools.partial(_ffn_kernel, k=k),
        grid=(bsz, n // tm),
        in_specs=[_tok_spec(tm), _mod_spec(per_batch), _const_spec((1, D_MODEL)),
                  _const_spec((D_MODEL, D_FF)), _const_spec((D_MODEL, D_FF)),
                  _const_spec((D_FF, D_MODEL))],
        out_specs=_tok_spec(tm),
        out_shape=jax.ShapeDtypeStruct(x.shape, F32),
        compiler_params=_params(2),
        name="ffn",
    )(x, mod, g, wg, wu, wd)


def _attn_out_rows(ot_ref, w_ref, r0, n_rows):
    _, nh, dv, tq = ot_ref.shape
    if tq >= n_rows:
        c0 = r0 % tq
        a = ot_ref[r0 // tq, :, :, c0:c0 + n_rows].reshape(nh * dv, n_rows)
        return _dot_tn(a, w_ref[...])
    parts = [_dot_tn(ot_ref[j].reshape(nh * dv, tq), w_ref[...])
             for j in range(r0 // tq, (r0 + n_rows) // tq)]
    return jnp.concatenate(parts, axis=0)


def _out_ffn_kernel(*refs, n_attn):
    h_ref, mod_ref, g_ref = refs[:3]
    attn = refs[3:3 + 2 * n_attn]
    wg_ref, wu_ref, wd_ref, o_ref = refs[3 + 2 * n_attn:]
    for r0 in range(0, h_ref.shape[0], FFN_SUB_ROWS):
        rows = slice(r0, r0 + FFN_SUB_ROWS)
        y = None
        for a in range(n_attn):
            ya = _attn_out_rows(attn[2 * a], attn[2 * a + 1], r0, FFN_SUB_ROWS)
            y = ya if y is None else y + ya
        x = h_ref[rows, :] + mod_ref[5:6, :] * y
        o_ref[rows, :] = _ffn_body(x, mod_ref, g_ref[...], wg_ref, wu_ref, wd_ref, 2)


def _out_ffn(h, mod, per_batch, g, attn, wg, wu, wd):
    bsz, n, _ = h.shape
    tm = min(FFN_TILE, n)
    in_specs = [_tok_spec(tm), _mod_spec(per_batch), _const_spec((1, D_MODEL))]
    args = [h, mod, g]
    for o_t, w in attn:
        _, _, nh, dv, tq = o_t.shape
        assert tm % tq == 0
        in_specs += [pl.BlockSpec((None, tm // tq, nh, dv, tq), lambda b, t: (b, t, 0, 0, 0)),
                     _const_spec(w.shape)]
        args += [o_t, w]
    in_specs += [_const_spec((D_MODEL, D_FF)), _const_spec((D_MODEL, D_FF)), _const_spec((D_FF, D_MODEL))]
    args += [wg, wu, wd]
    return pl.pallas_call(
        functools.partial(_out_ffn_kernel, n_attn=len(attn)),
        grid=(bsz, n // tm),
        in_specs=in_specs,
        out_specs=_tok_spec(tm),
        out_shape=jax.ShapeDtypeStruct(h.shape, F32),
        compiler_params=_params(2),
        name="out_ffn",
    )(*args)


L0_ST_SQ = 0
L0_ST_SV = L0_ST_SQ + SWA_HEADS * SWA_HEAD_DIM
L0_ST_KR = L0_ST_SV + SWA_KV_HEADS * SWA_HEAD_DIM
L0_ST_SK = L0_ST_KR + LANES
L0_ST_ROWS = L0_ST_SK + SWA_KV_HEADS * LANES


def _l0_proj_kernel(*refs, rope):
    (h_ref, mod_ref, g_ref, w0_ref, wst_ref, qag_ref, wqbt_ref, kvag_ref, wkvt_ref,
     mqg_ref, mkg_ref, sqg_ref, skg_ref) = refs[:13]
    if rope:
        cmt, smt, cst, sst = [r[...] for r in refs[13:17]]
        outs = refs[17:]
    else:
        outs = refs[13:]
    qm_ref, km_ref, vmt_ref, qs_ref, ks_ref, vst_ref = outs

    hn = _adaln(h_ref[...], g_ref[...], mod_ref[3:4, :], mod_ref[4:5, :]).astype(BF16)
    tm = hn.shape[0]
    p = _dot(hn, w0_ref[...])
    st = _grouped_nt(wst_ref, hn)

    mla_inv = 1.0 / (MLA_NOPE + MLA_ROPE)
    swa_inv = 1.0 / SWA_HEAD_DIM
    bcast = lambda v: jnp.broadcast_to(v, (v.shape[0], tm))
    mqg = bcast(mqg_ref[...] * (mla_inv ** 0.5 * LOG2E))
    sqg = bcast(sqg_ref[...] * (swa_inv ** 0.5 * LOG2E))
    mkg = bcast(mkg_ref[...])
    skg = bcast(skg_ref[...])

    def mla_rope(t):
        return _rope_t(t, cmt, smt, MLA_ROPE_ROWS) if rope else t

    def swa_rope(t):
        return _rope_t(t, cst, sst, SWA_ROPE_ROWS, SWA_ROPE_ROWS) if rope else t

    for h in range(SWA_HEADS):
        t = swa_rope(_rms_t(st(L0_ST_SQ + h * SWA_HEAD_DIM, L0_ST_SQ + (h + 1) * SWA_HEAD_DIM), sqg, swa_inv))
        for i in range(tm // SWA_BLK):
            qs_ref[h, i] = t[:, i * SWA_BLK:(i + 1) * SWA_BLK].astype(BF16)
    for j in range(SWA_KV_HEADS):
        t = swa_rope(_rms_t(st(L0_ST_SK + j * LANES, L0_ST_SK + (j + 1) * LANES), skg, swa_inv))
        ks_ref[j] = t.T.astype(BF16)
        vt = st(L0_ST_SV + j * SWA_HEAD_DIM, L0_ST_SV + (j + 1) * SWA_HEAD_DIM)
        for i in range(tm // SWA_BLK):
            vst_ref[j, i] = vt[:, i * SWA_BLK:(i + 1) * SWA_BLK].astype(BF16)

    qa = _rms(p[:, 0:MLA_Q_RANK], qag_ref[...], 1.0 / MLA_Q_RANK).astype(BF16)
    qt = _grouped_nt(wqbt_ref, qa)
    for h in range(MLA_HEADS):
        t = mla_rope(_rms_t(qt(h * LANES, (h + 1) * LANES), mqg, mla_inv))
        tq = qm_ref.shape[-1]
        for j in range(tm // tq):
            qm_ref[h, j] = t[:, j * tq:(j + 1) * tq].astype(BF16)

    kva = _rms(p[:, MLA_Q_RANK:MLA_Q_RANK + MLA_KV_RANK], kvag_ref[...], 1.0 / MLA_KV_RANK).astype(BF16)
    kvt = _grouped_nt(wkvt_ref, kva)
    kr = st(L0_ST_KR, L0_ST_KR + LANES)
    for h in range(MLA_HEADS):
        t = mla_rope(_rms_t(kvt(h * LANES, (h + 1) * LANES) + kr, mkg, mla_inv))
        km_ref[h] = t.T.astype(BF16)
    v0 = MLA_HEADS * LANES
    for h in range(MLA_HEADS):
        vmt_ref[h] = kvt(v0 + h * MLA_V, v0 + (h + 1) * MLA_V).astype(BF16)


def _l0_proj(h, mod, per_batch, g, w, tables):
    bsz, n, _ = h.shape
    tm = min(TOK_TILE, n)
    rope = tables is not None
    weights = [g, w["w0"], w["wst"], w["qa_g"], w["wqbt"], w["kva_g"], w["wkvt"],
               w["mq_g"], w["mk_g"], w["sq_g"], w["sk_g"]]
    in_specs = [_tok_spec(tm), _mod_spec(per_batch)] + [_const_spec(a.shape) for a in weights]
    args = [h, mod] + weights
    if rope:
        in_specs += [pl.BlockSpec((LANES, tm), lambda b, t: (0, t))] * 4
        args += list(tables)
    k_spec = lambda nh: pl.BlockSpec((None, nh, tm, LANES), lambda b, t: (b, 0, t, 0))
    qt_spec = lambda rows, tq: pl.BlockSpec((None, MLA_HEADS, tm // tq, rows, tq), lambda b, t: (b, 0, t, 0, 0))
    tq = min(MLA_TQ, n)
    out_specs = [qt_spec(LANES, tq), k_spec(MLA_HEADS),
                 pl.BlockSpec((None, MLA_HEADS, MLA_V, tm), lambda b, t: (b, 0, 0, t)),
                 qt_spec(SWA_HEAD_DIM, SWA_BLK), k_spec(SWA_KV_HEADS),
                 pl.BlockSpec((None, SWA_KV_HEADS, tm // SWA_BLK, SWA_HEAD_DIM, SWA_BLK),
                              lambda b, t: (b, 0, t, 0, 0))]
    out_shape = [jax.ShapeDtypeStruct((bsz, MLA_HEADS, n // tq, LANES, tq), BF16),
                 jax.ShapeDtypeStruct((bsz, MLA_HEADS, n, LANES), BF16),
                 jax.ShapeDtypeStruct((bsz, MLA_HEADS, MLA_V, n), BF16),
                 jax.ShapeDtypeStruct((bsz, SWA_HEADS, n // SWA_BLK, SWA_HEAD_DIM, SWA_BLK), BF16),
                 jax.ShapeDtypeStruct((bsz, SWA_KV_HEADS, n, LANES), BF16),
                 jax.ShapeDtypeStruct((bsz, SWA_KV_HEADS, n // SWA_BLK, SWA_HEAD_DIM, SWA_BLK), BF16)]
    return pl.pallas_call(
        functools.partial(_l0_proj_kernel, rope=rope),
        grid=(bsz, n // tm),
        in_specs=in_specs, out_specs=out_specs, out_shape=out_shape,
        compiler_params=_params(2),
        name="l0_proj",
    )(*args)


def _interleave(*stages):
    live = list(stages)
    while live:
        for st in list(live):
            try:
                next(st)
            except StopIteration:
                live.remove(st)


def _two_stage_loop(n_items, produce, consume, slot0, slot1):
    _interleave(produce(0, slot0))

    def pair(j, carry):
        i = 2 * j
        _interleave(produce(i + 1, slot1), consume(i, slot0))
        _interleave(produce(jnp.minimum(i + 2, n_items - 1), slot0), consume(i + 1, slot1))
        return carry

    lax.fori_loop(0, n_items // 2, pair, 0)


def _chunk_fold(x, op):
    return op(x.reshape(x.shape[0] // SUBLANES, SUBLANES, x.shape[-1]), axis=0)


def _score_chunks(s_ref, m_ref, key_chunk, qt, chunks, bias=None, floor=None):
    mx = None
    for k0, size in chunks:
        c = _dot(key_chunk(k0, size), qt)
        b = None if bias is None else bias(k0, size)
        if b is not None:
            c = c + b
        s_ref[k0:k0 + size, :] = c
        part = _chunk_fold(c, jnp.max)
        mx = part if mx is None else jnp.maximum(mx, part)
        yield
    m = jnp.max(mx, axis=0, keepdims=True)
    m_ref[...] = m if floor is None else jnp.maximum(m, floor)


def _exp_pv_chunks(s_ref, m_ref, value_t, chunk, out):
    m = m_ref[...]
    l8 = None
    o = None
    for k0 in range(0, s_ref.shape[0], chunk):
        e = jnp.exp2(s_ref[k0:k0 + chunk, :] - m)
        part = _chunk_fold(e, jnp.sum)
        l8 = part if l8 is None else l8 + part
        d = _dot(value_t(k0), e.astype(BF16))
        o = d if o is None else o + d
        yield
    out.append((o, jnp.sum(l8, axis=0, keepdims=True)))


KEY_CHUNK = 256


def _ctx_then_latent(c_ref, l_ref, n_ctx, token_major):
    def chunk(k0, size=KEY_CHUNK):
        ref, k = (c_ref, k0) if k0 < n_ctx else (l_ref, k0 - n_ctx)
        return ref[k:k + size, :] if token_major else ref[:, k:k + size]
    return chunk


SCORE_CHUNK = KEY_CHUNK


def _score_key_chunks(n_ctx, n_keys):
    assert n_ctx <= SCORE_CHUNK
    return [(0, n_ctx)] + [(k0, min(SCORE_CHUNK, n_keys - k0)) for k0 in range(n_ctx, n_keys, SCORE_CHUNK)]


def _mla_attn_kernel(q_ref, kc_ref, vct_ref, kl_ref, vlt_ref, o_ref, s0_ref, m0_ref, s1_ref, m1_ref):
    n_tiles, n_heads = o_ref.shape[:2]
    n_ctx = kc_ref.shape[1]
    n_keys = s0_ref.shape[0]
    score_chunks = _score_key_chunks(n_ctx, n_keys)

    def split(i):
        return lax.div(i, n_tiles), lax.rem(i, n_tiles)

    def scores(i, slot):
        s_ref, m_ref = slot
        h, t = split(i)
        keys = _ctx_then_latent(kc_ref.at[h], kl_ref.at[h], n_ctx, True)
        yield from _score_chunks(s_ref, m_ref, keys, q_ref[h, t], score_chunks)

    def softmax_pv(i, slot):
        s_ref, m_ref = slot
        h, t = split(i)
        values = _ctx_then_latent(vct_ref.at[h], vlt_ref.at[h], n_ctx, False)
        res = []
        yield from _exp_pv_chunks(s_ref, m_ref, values, KEY_CHUNK, res)
        o, l = res[0]
        o_ref[t, h] = (o * (1.0 / l)).astype(BF16)

    _two_stage_loop(n_tiles * n_heads, scores, softmax_pv, (s0_ref, m0_ref), (s1_ref, m1_ref))


def _mla_attn(q, kc, vct, kl, vlt):
    bsz, nh, n_tiles = q.shape[:3]
    hs = ATT_HEADS_PER_STEP
    n_keys = kc.shape[2] + kl.shape[2]
    grp = lambda shape: pl.BlockSpec((None, hs) + tuple(shape), lambda b, g: (b, g) + (0,) * len(shape))
    tq = q.shape[-1]
    slot = [pltpu.VMEM((n_keys, tq), F32), pltpu.VMEM((1, tq), F32)]
    return pl.pallas_call(
        _mla_attn_kernel,
        grid=(bsz, nh // hs),
        in_specs=[grp(q.shape[2:]), grp(kc.shape[2:]), grp(vct.shape[2:]), grp(kl.shape[2:]), grp(vlt.shape[2:])],
        out_specs=pl.BlockSpec((None, n_tiles, hs, MLA_V, tq), lambda b, g: (b, 0, g, 0, 0)),
        out_shape=jax.ShapeDtypeStruct((bsz, n_tiles, nh, MLA_V, tq), BF16),
        scratch_shapes=slot + slot,
        compiler_params=_params(2),
        name="mla_attn",
    )(q, kc, vct, kl, vlt)


def _mla_ctx_attn_kernel(q_ref, kc_ref, vct_ref, o_ref):
    for h in range(o_ref.shape[0]):
        sc = _dot(kc_ref[h], q_ref[h])
        pc = jnp.exp2(sc - jnp.max(sc, axis=0, keepdims=True))
        l = jnp.sum(pc, axis=0, keepdims=True)
        o = _dot(vct_ref[h], pc.astype(BF16))
        o_ref[h] = (o * (1.0 / l)).astype(BF16)


def _mla_ctx_attn(q, kc, vct):
    bsz, nh, n_tiles = q.shape[:3]
    assert n_tiles == 1
    tq = q.shape[-1]
    per_b = lambda shape: pl.BlockSpec((None,) + tuple(shape), lambda b: (b,) + (0,) * len(shape))
    return pl.pallas_call(
        _mla_ctx_attn_kernel,
        grid=(bsz,),
        in_specs=[pl.BlockSpec((None, nh, None, LANES, tq), lambda b: (b, 0, 0, 0, 0)),
                  per_b(kc.shape[1:]), per_b(vct.shape[1:])],
        out_specs=pl.BlockSpec((None, None, nh, MLA_V, tq), lambda b: (b, 0, 0, 0, 0)),
        out_shape=jax.ShapeDtypeStruct((bsz, 1, nh, MLA_V, tq), BF16),
        compiler_params=_params(1),
        name="mla_ctx_attn",
    )(q, kc, vct)


SWA_BAND = 3 * SWA_BLK
SWA_NQ = SWA_GROUP * SWA_BLK


def _swa_query_block(heads):
    qt = jnp.concatenate(heads, axis=1)
    return jnp.concatenate([qt, jnp.zeros((LANES - qt.shape[0], qt.shape[1]), qt.dtype)], axis=0)


def _sink_row(sink_ref, j):
    return jnp.concatenate(
        [jnp.full((1, SWA_BLK), sink_ref[j * SWA_GROUP + g] * LOG2E, F32) for g in range(SWA_GROUP)], axis=1)


def _swa_attn_kernel(sink_ref, q_ref, kc_ref, vct_ref, kl_ref, vlt_ref, o_ref,
                     bias_ref, s0_ref, m0_ref, s1_ref, m1_ref):
    n_blk = o_ref.shape[0]
    n_ctx = kc_ref.shape[1]

    @pl.when(pl.program_id(0) == 0)
    def _():
        r = lax.broadcasted_iota(jnp.int32, (SWA_BAND, SWA_NQ), 0)
        c = lax.broadcasted_iota(jnp.int32, (SWA_BAND, SWA_NQ), 1) & (SWA_BLK - 1)
        for case in range(3):
            dist = r - c - case * SWA_BLK
            bias_ref[case] = jnp.where(jnp.abs(dist) <= SWA_WINDOW, 0.0, MASKED)

    def split(i):
        return lax.div(i, n_blk), lax.rem(i, n_blk)

    def band_start(n):
        return jnp.clip(n - 1, 0, n_blk - 3)

    def scores(i, slot):
        s_ref, m_ref = slot
        j, n = split(i)
        qt = _swa_query_block([q_ref[j * SWA_GROUP + g, n] for g in range(SWA_GROUP)])
        band0 = band_start(n) * SWA_BLK
        case = (n > 0).astype(jnp.int32) + (n == n_blk - 1).astype(jnp.int32)

        def keys(k0, size):
            if k0 < n_ctx:
                return kc_ref[j, k0:k0 + size, :]
            return kl_ref[j, pl.ds(pl.multiple_of(band0 + (k0 - n_ctx), SWA_BLK), size), :]

        def bias(k0, size):
            return None if k0 < n_ctx else bias_ref[case, k0 - n_ctx:k0 - n_ctx + size, :]

        chunks = [(0, n_ctx), (n_ctx, SWA_BAND)]
        yield from _score_chunks(s_ref, m_ref, keys, qt, chunks, bias, _sink_row(sink_ref, j))

    def softmax_pv(i, slot):
        s_ref, m_ref = slot
        j, n = split(i)
        b0 = band_start(n)

        def value_t(k0):
            if k0 < n_ctx:
                return vct_ref[j, :, k0:k0 + SWA_BLK]
            return vlt_ref[j, b0 + (k0 - n_ctx) // SWA_BLK]

        res = []
        yield from _exp_pv_chunks(s_ref, m_ref, value_t, SWA_BLK, res)
        o, l = res[0]
        o = o * (1.0 / (l + jnp.exp2(_sink_row(sink_ref, j) - m_ref[...])))
        for g in range(SWA_GROUP):
            o_ref[n, j * SWA_GROUP + g] = o[:, g * SWA_BLK:(g + 1) * SWA_BLK].astype(BF16)

    _two_stage_loop(SWA_KV_HEADS * n_blk, scores, softmax_pv, (s0_ref, m0_ref), (s1_ref, m1_ref))


def _swa_attn(sink, q, kc, vct, kl, vlt):
    bsz, nh, n_blk = q.shape[:3]
    n_keys = kc.shape[2] + SWA_BAND
    full = lambda a: pl.BlockSpec((None,) + a.shape[1:], lambda b: (b,) + (0,) * (a.ndim - 1))
    slot = [pltpu.VMEM((n_keys, SWA_NQ), F32), pltpu.VMEM((1, SWA_NQ), F32)]
    return pl.pallas_call(
        _swa_attn_kernel,
        grid=(bsz,),
        in_specs=[pl.BlockSpec(memory_space=pltpu.SMEM), full(q), full(kc), full(vct), full(kl), full(vlt)],
        out_specs=pl.BlockSpec((None, n_blk, nh, SWA_HEAD_DIM, SWA_BLK), lambda b: (b, 0, 0, 0, 0)),
        out_shape=jax.ShapeDtypeStruct((bsz, n_blk, nh, SWA_HEAD_DIM, SWA_BLK), BF16),
        scratch_shapes=[pltpu.VMEM((3, SWA_BAND, SWA_NQ), F32)] + slot + slot,
        compiler_params=_params(1),
        name="swa_attn",
    )(sink, q, kc, vct, kl, vlt)


def _swa_ctx_attn_kernel(sink_ref, q_ref, kc_ref, vct_ref, o_ref):
    sink = _sink_row(sink_ref, pl.program_id(1))
    for n in range(o_ref.shape[0]):
        qt = _swa_query_block([q_ref[g, n] for g in range(SWA_GROUP)])
        s = _dot(kc_ref[...], qt)
        m = jnp.maximum(jnp.max(s, axis=0, keepdims=True), sink)
        p = jnp.exp2(s - m)
        l = jnp.sum(p, axis=0, keepdims=True) + jnp.exp2(sink - m)
        o = _dot(vct_ref[...], p.astype(BF16)) * (1.0 / l)
        for g in range(SWA_GROUP):
            o_ref[n, g] = o[:, g * SWA_BLK:(g + 1) * SWA_BLK].astype(BF16)


def _swa_ctx_attn(sink, q, kc, vct):
    bsz, nh, n_blk = q.shape[:3]
    return pl.pallas_call(
        _swa_ctx_attn_kernel,
        grid=(bsz, SWA_KV_HEADS),
        in_specs=[pl.BlockSpec(memory_space=pltpu.SMEM),
                  pl.BlockSpec((None, SWA_GROUP, n_blk, SWA_HEAD_DIM, SWA_BLK), lambda b, j: (b, j, 0, 0, 0)),
                  pl.BlockSpec((None, None) + kc.shape[2:], lambda b, j: (b, j, 0, 0)),
                  pl.BlockSpec((None, None) + vct.shape[2:], lambda b, j: (b, j, 0, 0))],
        out_specs=pl.BlockSpec((None, n_blk, SWA_GROUP, SWA_HEAD_DIM, SWA_BLK), lambda b, j: (b, 0, j, 0, 0)),
        out_shape=jax.ShapeDtypeStruct((bsz, n_blk, nh, SWA_HEAD_DIM, SWA_BLK), BF16),
        compiler_params=_params(2),
        name="swa_ctx_attn",
    )(sink, q, kc, vct)


DIFF_Q = HALF // 2


def _diff_norm_t(t, g):
    sq = t * t
    grp = [jnp.sum(sq[i * DIFF_Q:(i + 1) * DIFF_Q], axis=0, keepdims=True) for i in range(4)]
    inv = 1.0 / DIFF_HEAD_DIM
    r0 = lax.rsqrt((grp[0] + grp[2]) * inv + EPS)
    r1 = lax.rsqrt((grp[1] + grp[3]) * inv + EPS)
    parts = [t[i * DIFF_Q:(i + 1) * DIFF_Q] * (r0 if i % 2 == 0 else r1) for i in range(4)]
    return jnp.concatenate(parts, axis=0) * g


def _l1_proj_kernel(*refs, latent):
    if latent:
        h_ref, mod_ref, g_ref, wt_ref, qg_ref, kg_ref, cdt_ref, sdt_ref, q_ref, k_ref, vt_ref = refs
        cdt, sdt = cdt_ref[...], sdt_ref[...]
    else:
        h_ref, mod_ref, g_ref, wt_ref, kg_ref, k_ref, vt_ref = refs
    hn = _adaln(h_ref[...], g_ref[...], mod_ref[3:4, :], mod_ref[4:5, :]).astype(BF16)
    tm = hn.shape[0]
    width = DIFF_HEADS * LANES
    ft = _grouped_nt(wt_ref, hn)
    bcast = lambda v: jnp.broadcast_to(v, (LANES, tm))
    k0 = 0
    if latent:
        qg = bcast(qg_ref[...] * (DIFF_HEAD_DIM ** -0.5 * LOG2E))
        for h in range(DIFF_HEADS):
            t = _rope_t(_diff_norm_t(ft(h * LANES, (h + 1) * LANES), qg), cdt, sdt, DIFF_ROPE_ROWS)
            tq = q_ref.shape[-1]
            for j in range(tm // tq):
                q_ref[h, j] = t[:, j * tq:(j + 1) * tq].astype(BF16)
        k0 = width

    kg = bcast(kg_ref[...])
    for h in range(DIFF_HEADS):
        t = _diff_norm_t(ft(k0 + h * LANES, k0 + (h + 1) * LANES), kg)
        if latent:
            t = _rope_t(t, cdt, sdt, DIFF_ROPE_ROWS)
        k_ref[h] = t.T.astype(BF16)
    v0 = k0 + width
    for h in range(DIFF_HEADS):
        vt_ref[h] = ft(v0 + h * LANES, v0 + (h + 1) * LANES).astype(BF16)


def _l1_proj(h, mod, per_batch, g, w, tables):
    bsz, n, _ = h.shape
    tm = min(TOK_TILE, n)
    latent = tables is not None
    k_spec = pl.BlockSpec((None, DIFF_HEADS, tm, LANES), lambda b, t: (b, 0, t, 0))
    vt_spec = pl.BlockSpec((None, DIFF_HEADS, LANES, tm), lambda b, t: (b, 0, 0, t))
    k_shape = jax.ShapeDtypeStruct((bsz, DIFF_HEADS, n, LANES), BF16)
    vt_shape = jax.ShapeDtypeStruct((bsz, DIFF_HEADS, LANES, n), BF16)
    if latent:
        weights = [g, w["wqkvt"], w["q_g"], w["k_g"]]
        in_specs = ([_tok_spec(tm), _mod_spec(per_batch)] + [_const_spec(a.shape) for a in weights]
                    + [pl.BlockSpec((LANES, tm), lambda b, t: (0, t))] * 2)
        args = [h, mod] + weights + list(tables)
        tq = min(DIFF_TQ, n)
        out_specs = [pl.BlockSpec((None, DIFF_HEADS, tm // tq, LANES, tq), lambda b, t: (b, 0, t, 0, 0)),
                     k_spec, vt_spec]
        out_shape = [jax.ShapeDtypeStruct((bsz, DIFF_HEADS, n // tq, LANES, tq), BF16), k_shape, vt_shape]
    else:
        weights = [g, w["wkvt"], w["k_g"]]
        in_specs = [_tok_spec(tm), _mod_spec(per_batch)] + [_const_spec(a.shape) for a in weights]
        args = [h, mod] + weights
        out_specs = [k_spec, vt_spec]
        out_shape = [k_shape, vt_shape]
    return pl.pallas_call(
        functools.partial(_l1_proj_kernel, latent=latent),
        grid=(bsz, n // tm),
        in_specs=in_specs, out_specs=out_specs, out_shape=out_shape,
        compiler_params=_params(2),
        name="l1_proj",
    )(*args)


def _diff_attn_kernel(lam_ref, subg_ref, q_ref, kc_ref, vct_ref, kl_ref, vlt_ref, o_ref,
                      s0_ref, m0_ref, s1_ref, m1_ref):
    n_tiles, n_heads = o_ref.shape[:2]
    n_ctx = kc_ref.shape[1]
    n_keys = s0_ref.shape[1]
    score_chunks = _score_key_chunks(n_ctx, n_keys)
    feature = lax.broadcasted_iota(jnp.int32, q_ref.shape[2:], 0)
    sub0 = (feature & (HALF - 1)) < DIFF_Q
    lv = lam_ref[...]
    lam = (jnp.exp(jnp.sum(lv[0:1] * lv[1:2], axis=-1, keepdims=True))
           - jnp.exp(jnp.sum(lv[2:3] * lv[3:4], axis=-1, keepdims=True)) + LAMBDA_INIT_L1)
    subg = subg_ref[...] * (1.0 - LAMBDA_INIT_L1)

    def split(i):
        return lax.div(i, n_tiles), lax.rem(i, n_tiles)

    def scores(i, slot):
        s_ref, m_ref = slot
        h, t = split(i)
        qt = q_ref[h, t]
        zero = jnp.zeros_like(qt)
        keys = _ctx_then_latent(kc_ref.at[h], kl_ref.at[h], n_ctx, True)
        for s, qs in enumerate((jnp.where(sub0, qt, zero), jnp.where(sub0, zero, qt))):
            yield from _score_chunks(s_ref.at[s], m_ref.at[s], keys, qs, score_chunks)

    def softmax_pv(i, slot):
        s_ref, m_ref = slot
        h, t = split(i)
        values = _ctx_then_latent(vct_ref.at[h], vlt_ref.at[h], n_ctx, False)
        res = []
        for s in range(2):
            yield from _exp_pv_chunks(s_ref.at[s], m_ref.at[s], values, KEY_CHUNK, res)
        (o0, l0), (o1, l1) = res
        o = o0 * (1.0 / l0) - o1 * (lam * (1.0 / l1))
        ms = jnp.mean(o * o, axis=0, keepdims=True)
        o_ref[t, h] = (o * lax.rsqrt(ms + EPS) * subg).astype(BF16)

    _two_stage_loop(n_tiles * n_heads, scores, softmax_pv, (s0_ref, m0_ref), (s1_ref, m1_ref))


def _diff_attn(lamv, subg, q, kc, vct, kl, vlt):
    bsz, nh, n_tiles = q.shape[:3]
    hs = ATT_HEADS_PER_STEP
    n_keys = kc.shape[2] + kl.shape[2]
    grp = lambda shape: pl.BlockSpec((None, hs) + tuple(shape), lambda b, g: (b, g) + (0,) * len(shape))
    tq = q.shape[-1]
    slot = [pltpu.VMEM((2, n_keys, tq), F32), pltpu.VMEM((2, 1, tq), F32)]
    return pl.pallas_call(
        _diff_attn_kernel,
        grid=(bsz, nh // hs),
        in_specs=[_const_spec(lamv.shape), _const_spec(subg.shape),
                  grp(q.shape[2:]), grp(kc.shape[2:]), grp(vct.shape[2:]), grp(kl.shape[2:]), grp(vlt.shape[2:])],
        out_specs=pl.BlockSpec((None, n_tiles, hs, LANES, tq), lambda b, g: (b, 0, g, 0, 0)),
        out_shape=jax.ShapeDtypeStruct((bsz, n_tiles, nh, LANES, tq), BF16),
        scratch_shapes=slot + slot,
        compiler_params=_params(2),
        name="diff_attn",
    )(lamv, subg, q, kc, vct, kl, vlt)


def _prep_l0(w_in, qa_g, wqb, kva_g, wkvb, q_g, k_g, sq_g, sk_g, w_out):
    mla = _mla_lane_map()
    swa = _swa_lane_map()
    kr_cols = np.where(mla >= MLA_NOPE, 384 + mla - MLA_NOPE, -1)
    wst_cols = np.concatenate([
        416 + np.arange(SWA_HEADS * SWA_HEAD_DIM),
        np.arange(1056, 1184),
        kr_cols,
        _per_head(swa, SWA_KV_HEADS, SWA_HEAD_DIM, base=928),
    ])
    assert wst_cols.shape[0] == L0_ST_ROWS
    nope_map = np.where((mla >= 0) & (mla < MLA_NOPE), mla, -1)
    v_cols = np.concatenate([h * (MLA_NOPE + MLA_V) + MLA_NOPE + np.arange(MLA_V) for h in range(MLA_HEADS)])
    wkvt_cols = np.concatenate([_per_head(nope_map, MLA_HEADS, MLA_NOPE + MLA_V), v_cols])
    row = lambda v: v.reshape(1, -1).astype(F32)
    col = lambda v: v.reshape(-1, 1).astype(F32)
    return {
        "w0": w_in[:, 0:MLA_Q_RANK + MLA_KV_RANK].astype(BF16),
        "wst": _cols_as_rows(w_in, wst_cols),
        "qa_g": row(qa_g),
        "wqbt": _cols_as_rows(wqb, _per_head(mla, MLA_HEADS, MLA_NOPE + MLA_ROPE)),
        "kva_g": row(kva_g),
        "wkvt": _cols_as_rows(wkvb, wkvt_cols),
        "mq_g": col(_take_cols(q_g, mla)),
        "mk_g": col(_take_cols(k_g, mla)),
        "sq_g": col(sq_g),
        "sk_g": col(_take_cols(sk_g, swa)),
        "wa": w_out[:MLA_HEADS * MLA_V].astype(BF16),
        "wb": w_out[MLA_HEADS * MLA_V:].astype(BF16),
    }


def _prep_l1(w_in, q_g, k_g, w_out):
    dm = _diff_lane_map()
    width = DIFF_HEADS * LANES
    cols = np.concatenate([_per_head(dm, DIFF_HEADS, LANES), _per_head(dm, DIFF_HEADS, LANES, base=width),
                           np.arange(2 * width, 3 * width)])
    wqkvt = _cols_as_rows(w_in, cols)
    return {
        "wqkvt": wqkvt,
        "wkvt": wqkvt[width:],
        "q_g": _take_cols(q_g, dm % DIFF_HEAD_DIM).reshape(-1, 1).astype(F32),
        "k_g": _take_cols(k_g, dm % DIFF_HEAD_DIM).reshape(-1, 1).astype(F32),
        "w_out": w_out.astype(BF16),
    }


def kernel(x, c, ctx, c_ctx, l0_ada_w, l0_ada_b, l0_norm_g, l0_ffn_wg, l0_ffn_wu, l0_ffn_wd, l0_w_in, l0_mla_qa_g, l0_mla_wqb, l0_mla_kva_g, l0_mla_wkvb, l0_mla_q_g, l0_mla_k_g, l0_swa_q_g, l0_swa_k_g, l0_swa_sink, l0_w_out, l1_ada_w, l1_ada_b, l1_norm_g, l1_ffn_wg, l1_ffn_wu, l1_ffn_wd, l1_w_in, l1_q_g, l1_k_g, l1_lambda_q1, l1_lambda_k1, l1_lambda_q2, l1_lambda_k2, l1_subln_g, l1_w_out):
    bsz, seq, _ = x.shape

    pad = (-(bsz + 1)) % SUBLANES
    cc = jnp.concatenate([c, c_ctx[None, :], jnp.zeros((pad, D_MODEL), F32)], axis=0)

    def mods(ada_w, ada_b):
        m = _ada(cc, ada_w, ada_b).reshape(cc.shape[0], N_MOD, D_MODEL)
        return m[:bsz], m[bsz:bsz + 1]

    def ffn_weights(wg, wu, wd, i):
        return wg[i].astype(BF16), wu[i].astype(BF16), wd[i].astype(BF16)

    mla_tab = _rope_tables(seq, MLA_ROPE, [(0, 16)], [(64, 16)])
    swa_tab = _rope_tables(seq, SWA_HEAD_DIM, [(0, 32)], [(32, 32)])
    diff_tab = _rope_tables(seq, DIFF_HEAD_DIM, [(0, 32), (32, 32)], [(64, 32), (96, 32)])

    mod, mod_c = mods(l0_ada_w, l0_ada_b)
    g = [l0_norm_g[i:i + 1] for i in range(3)]
    w = _prep_l0(l0_w_in, l0_mla_qa_g, l0_mla_wqb, l0_mla_kva_g, l0_mla_wkvb, l0_mla_q_g, l0_mla_k_g,
                 l0_swa_q_g, l0_swa_k_g, l0_w_out)
    f0 = ffn_weights(l0_ffn_wg, l0_ffn_wu, l0_ffn_wd, 0)
    f1 = ffn_weights(l0_ffn_wg, l0_ffn_wu, l0_ffn_wd, 1)

    h = _ffn(x, mod, True, g[0], *f0, k=0)
    hc = _ffn(ctx, mod_c, False, g[0], *f0, k=0)

    qm, km, vmt, qs, ks, vst = _l0_proj(h, mod, True, g[1], w, mla_tab + swa_tab)
    qm_c, km_c, vmt_c, qs_c, ks_c, vst_c = _l0_proj(hc, mod_c, False, g[1], w, None)
    vsc_t = vst_c.transpose(0, 1, 3, 2, 4).reshape(bsz, SWA_KV_HEADS, SWA_HEAD_DIM, -1)

    at = _mla_attn(qm, km_c, vmt_c, km, vmt)
    bt = _swa_attn(l0_swa_sink, qs, ks_c, vsc_t, ks, vst)
    at_c = _mla_ctx_attn(qm_c, km_c, vmt_c)
    bt_c = _swa_ctx_attn(l0_swa_sink, qs_c, ks_c, vsc_t)

    h = _out_ffn(h, mod, True, g[2], [(at, w["wa"]), (bt, w["wb"])], *f1)
    hc = _out_ffn(hc, mod_c, False, g[2], [(at_c, w["wa"]), (bt_c, w["wb"])], *f1)

    mod, mod_c = mods(l1_ada_w, l1_ada_b)
    g = [l1_norm_g[i:i + 1] for i in range(3)]
    w = _prep_l1(l1_w_in, l1_q_g, l1_k_g, l1_w_out)
    f0 = ffn_weights(l1_ffn_wg, l1_ffn_wu, l1_ffn_wd, 0)
    f1 = ffn_weights(l1_ffn_wg, l1_ffn_wu, l1_ffn_wd, 1)

    h = _ffn(h, mod, True, g[0], *f0, k=0)
    hc = _ffn(hc, mod_c, False, g[0], *f0, k=0)

    qd, kd, vdt = _l1_proj(h, mod, True, g[1], w, diff_tab)
    kd_c, vdt_c = _l1_proj(hc, mod_c, False, g[1], w, None)
    lamv = jnp.stack([l1_lambda_q1, l1_lambda_k1, l1_lambda_q2, l1_lambda_k2]).astype(F32)
    ot = _diff_attn(lamv, l1_subln_g.reshape(-1, 1).astype(F32), qd, kd_c, vdt_c, kd, vdt)
    return _out_ffn(h, mod, True, g[2], [(ot, w["w_out"])], *f1)
```

```python
import functools
import math

import numpy as np
import jax
import jax.numpy as jnp
from jax import lax
from jax.experimental import pallas as pl
from jax.experimental.pallas import tpu as pltpu

F32 = jnp.float32
BF16 = jnp.bfloat16

D_MODEL = 1024
D_FF = 2816
N_MOD = 9
EPS = 1e-6
ROPE_BASE = 10000.0
GRID_W = 64
LOG2E = 1.4426950408889634

MLA_HEADS = 8
MLA_Q_RANK = 256
MLA_KV_RANK = 128
MLA_NOPE = 64
MLA_ROPE = 32
MLA_V = 64
SWA_HEADS = 8
SWA_KV_HEADS = 2
SWA_GROUP = SWA_HEADS // SWA_KV_HEADS
SWA_HEAD_DIM = 64
SWA_WINDOW = 128
DIFF_HEADS = 8
DIFF_HEAD_DIM = 64
LAMBDA_INIT_L1 = 0.8 - 0.6 * math.exp(-0.3 * 1)

LANES = 128
SUBLANES = 8
HALF = LANES // 2
MLA_TQ = 512
DIFF_TQ = 256
SWA_BLK = 128
TOK_TILE = 512
FFN_TILE = 1024
ATT_HEADS_PER_STEP = 8
VMEM_LIMIT = 56 * 1024 * 1024
MASKED = -1e30

NT_DIMS = (((1,), (1,)), ((), ()))
TN_DIMS = (((0,), (0,)), ((), ()))


def _mla_lane_map():
    m = np.full(LANES, -1, np.int64)
    m[0:16] = 64 + np.arange(16)
    m[16:48] = np.arange(32)
    m[64:80] = 80 + np.arange(16)
    m[80:112] = 32 + np.arange(32)
    return m


def _swa_lane_map():
    m = np.full(LANES, -1, np.int64)
    m[0:SWA_HEAD_DIM] = np.arange(SWA_HEAD_DIM)
    return m


def _diff_lane_map():
    m = np.zeros(LANES, np.int64)
    m[0:32] = np.arange(32)
    m[32:64] = 64 + np.arange(32)
    m[64:96] = 32 + np.arange(32)
    m[96:128] = 96 + np.arange(32)
    return m


MLA_ROPE_ROWS = MLA_ROPE // 2
SWA_ROPE_ROWS = SWA_HEAD_DIM // 2
DIFF_ROPE_ROWS = HALF


def _take_cols(w, idx):
    idx = np.asarray(idx)
    out = jnp.take(w, jnp.asarray(np.where(idx < 0, 0, idx), jnp.int32), axis=-1)
    return jnp.where(jnp.asarray(idx >= 0), out, jnp.zeros((), w.dtype))


def _cols_as_rows(w, idx):
    idx = np.asarray(idx)
    wt = jnp.concatenate([w.astype(BF16).T, jnp.zeros((1, w.shape[0]), BF16)], axis=0)
    return jnp.take(wt, jnp.asarray(np.where(idx < 0, w.shape[1], idx), jnp.int32), axis=0)


def _per_head(lane_map, n_heads, stride, base=0):
    cols = [np.where(lane_map >= 0, base + h * stride + lane_map, -1) for h in range(n_heads)]
    return np.concatenate(cols)


def _adaln(x, g, shift, scale):
    ms = jnp.mean(x * x, axis=-1, keepdims=True)
    return x * lax.rsqrt(ms + EPS) * g * (1.0 + scale) + shift


def _rms(t, g, inv_dim):
    ms = jnp.sum(t * t, axis=-1, keepdims=True) * inv_dim
    return t * lax.rsqrt(ms + EPS) * g


def _rms_t(t, g, inv_dim):
    ms = jnp.sum(t * t, axis=0, keepdims=True) * inv_dim
    return t * lax.rsqrt(ms + EPS) * g


def _rope_t(t, cos, sin, rows, partner=HALF):
    a, b = t[0:rows], t[partner:partner + rows]
    a2 = a * cos[0:rows] + b * sin[0:rows]
    b2 = b * cos[partner:partner + rows] + a * sin[partner:partner + rows]
    parts = [a2, t[rows:partner], b2, t[partner + rows:]]
    return jnp.concatenate([p for p in parts if p.shape[0] > 0], axis=0)


def _dot(a, b):
    return jnp.dot(a, b, preferred_element_type=F32)


def _dot_nt(a, b):
    return lax.dot_general(a, b, NT_DIMS, preferred_element_type=F32)


def _dot_tn(a, b):
    return lax.dot_general(a, b, TN_DIMS, preferred_element_type=F32)


NT_GROUP_ROWS = 512


def _grouped_nt(w_ref, x):
    done = {}

    def rows(r0, r1):
        g = r0 // NT_GROUP_ROWS
        assert (r1 - 1) // NT_GROUP_ROWS == g
        if g not in done:
            done[g] = _dot_nt(w_ref[g * NT_GROUP_ROWS:(g + 1) * NT_GROUP_ROWS, :], x)
        return done[g][r0 - g * NT_GROUP_ROWS:r1 - g * NT_GROUP_ROWS]

    return rows


def _params(n_axes):
    return pltpu.CompilerParams(dimension_semantics=("arbitrary",) * n_axes,
                                vmem_limit_bytes=VMEM_LIMIT)


def _const_spec(shape):
    nd = len(shape)
    return pl.BlockSpec(shape, lambda *_: (0,) * nd, pipeline_mode=pl.Buffered(1))


def _mod_spec(per_batch):
    if per_batch:
        return pl.BlockSpec((None, N_MOD, D_MODEL), lambda b, t: (b, 0, 0))
    return pl.BlockSpec((None, N_MOD, D_MODEL), lambda b, t: (0, 0, 0))


def _tok_spec(tm):
    return pl.BlockSpec((None, tm, D_MODEL), lambda b, t: (b, t, 0))


def _ada_kernel(c_ref, w_ref, b_ref, o_ref):
    c = c_ref[...]
    a = (c * jax.nn.sigmoid(c)).astype(BF16)
    o_ref[...] = _dot(a, w_ref[...].astype(BF16)) + b_ref[...]


def _ada(cc, w, b):
    rows = cc.shape[0]
    n = w.shape[1]
    tn = 1152
    return pl.pallas_call(
        _ada_kernel,
        grid=(n // tn,),
        in_specs=[pl.BlockSpec((rows, D_MODEL), lambda j: (0, 0)),
                  pl.BlockSpec((D_MODEL, tn), lambda j: (0, j)),
                  pl.BlockSpec((1, tn), lambda j: (0, j))],
        out_specs=pl.BlockSpec((rows, tn), lambda j: (0, j)),
        out_shape=jax.ShapeDtypeStruct((rows, n), F32),
        compiler_params=_params(1),
        name="ada_mod",
    )(cc, w, b.reshape(1, n))


def _rope_table_kernel(inv_ref, userow_ref, sign_ref, cos_ref, sin_ref):
    n_layouts, _, seq = cos_ref.shape
    t = lax.broadcasted_iota(jnp.int32, (LANES, seq), 1)
    row = lax.shift_right_logical(t, GRID_W.bit_length() - 1).astype(F32)
    col = (t & (GRID_W - 1)).astype(F32)
    for i in range(n_layouts):
        pos = jnp.where(userow_ref[i] > 0.5, row, col)
        ang = pos * inv_ref[i]
        sign = sign_ref[i]
        active = sign != 0.0
        cos_ref[i] = jnp.where(active, jnp.cos(ang), 1.0)
        sin_ref[i] = jnp.where(active, sign * jnp.sin(ang), 0.0)


def _rope_tables(seq, layouts):
    n = len(layouts)
    inv = np.zeros((n, LANES, 1), np.float32)
    use_row = np.zeros((n, LANES, 1), np.float32)
    sign = np.zeros((n, LANES, 1), np.float32)
    for i, (rot_dim, x1_slots, x2_slots) in enumerate(layouts):
        n_f = rot_dim // 4
        inv_f = (ROPE_BASE ** (-np.arange(n_f, dtype=np.float64) / n_f)).astype(np.float32)
        inv_half = np.concatenate([inv_f, inv_f])
        use_row_half = np.concatenate([np.ones(n_f), np.zeros(n_f)])
        for slots, sgn in ((x1_slots, -1.0), (x2_slots, 1.0)):
            for start, width in slots:
                assert width == rot_dim // 2
                inv[i, start:start + width, 0] = inv_half
                use_row[i, start:start + width, 0] = use_row_half
                sign[i, start:start + width, 0] = sgn
    vec = pl.BlockSpec((n, LANES, 1), lambda: (0, 0, 0))
    tab = pl.BlockSpec((n, LANES, seq), lambda: (0, 0, 0))
    cos, sin = pl.pallas_call(
        _rope_table_kernel,
        in_specs=[vec, vec, vec],
        out_specs=[tab, tab],
        out_shape=[jax.ShapeDtypeStruct((n, LANES, seq), F32)] * 2,
        name="rope_table",
    )(jnp.asarray(inv), jnp.asarray(use_row), jnp.asarray(sign))
    return [[(cos, i), (sin, i)] for i in range(n)]


def _table_spec(layout, tm):
    return pl.BlockSpec((None, LANES, tm), lambda b, t: (layout, 0, t))


def _ffn_body(x, mod_ref, g, wg_ref, wu_ref, wd_ref, k):
    shift = mod_ref[3 * k:3 * k + 1, :]
    scale = mod_ref[3 * k + 1:3 * k + 2, :]
    gate = mod_ref[3 * k + 2:3 * k + 3, :]
    hn = _adaln(x, g, shift, scale).astype(BF16)
    gg = _dot(hn, wg_ref[...])
    u = _dot(hn, wu_ref[...])
    a = (gg * jax.nn.sigmoid(gg) * u).astype(BF16)
    return x + (0.5 * gate) * _dot(a, wd_ref[...])


FFN_SUB_ROWS = 256


def _ffn_kernel(x_ref, mod_ref, g_ref, wg_ref, wu_ref, wd_ref, o_ref, *, k):
    for r0 in range(0, x_ref.shape[0], FFN_SUB_ROWS):
        rows = slice(r0, r0 + FFN_SUB_ROWS)
        o_ref[rows, :] = _ffn_body(x_ref[rows, :], mod_ref, g_ref[...], wg_ref, wu_ref, wd_ref, k)


def _ffn(x, mod, per_batch, g, wg, wu, wd, k):
    bsz, n, _ = x.shape
    tm = min(FFN_TILE, n)
    return pl.pallas_call(
        functools.partial(_ffn_kernel, k=k),
        grid=(bsz, n // tm),
        in_specs=[_tok_spec(tm), _mod_spec(per_batch), _const_spec((1, D_MODEL)),
                  _const_spec((D_MODEL, D_FF)), _const_spec((D_MODEL, D_FF)),
                  _const_spec((D_FF, D_MODEL))],
        out_specs=_tok_spec(tm),
        out_shape=jax.ShapeDtypeStruct(x.shape, F32),
        compiler_params=_params(2),
        name="ffn",
    )(x, mod, g, wg, wu, wd)


def _attn_out_rows(ot_ref, w_ref, r0, n_rows):
    _, nh, dv, tq = ot_ref.shape
    if tq >= n_rows:
        c0 = r0 % tq
        a = ot_ref[r0 // tq, :, :, c0:c0 + n_rows].reshape(nh * dv, n_rows)
        return _dot_tn(a, w_ref[...])
    parts = [_dot_tn(ot_ref[j].reshape(nh * dv, tq), w_ref[...])
             for j in range(r0 // tq, (r0 + n_rows) // tq)]
    return jnp.concatenate(parts, axis=0)


def _out_ffn_kernel(*refs, n_attn):
    h_ref, mod_ref, g_ref = refs[:3]
    attn = refs[3:3 + 2 * n_attn]
    wg_ref, wu_ref, wd_ref, o_ref = refs[3 + 2 * n_attn:]
    for r0 in range(0, h_ref.shape[0], FFN_SUB_ROWS):
        rows = slice(r0, r0 + FFN_SUB_ROWS)
        y = None
        for a in range(n_attn):
            ya = _attn_out_rows(attn[2 * a], attn[2 * a + 1], r0, FFN_SUB_ROWS)
            y = ya if y is None else y + ya
        x = h_ref[rows, :] + mod_ref[5:6, :] * y
        o_ref[rows, :] = _ffn_body(x, mod_ref, g_ref[...], wg_ref, wu_ref, wd_ref, 2)


def _out_ffn(h, mod, per_batch, g, attn, wg, wu, wd):
    bsz, n, _ = h.shape
    tm = min(FFN_TILE, n)
    in_specs = [_tok_spec(tm), _mod_spec(per_batch), _const_spec((1, D_MODEL))]
    args = [h, mod, g]
    for o_t, w in attn:
        _, _, nh, dv, tq = o_t.shape
        assert tm % tq == 0
        in_specs += [pl.BlockSpec((None, tm // tq, nh, dv, tq), lambda b, t: (b, t, 0, 0, 0)),
                     _const_spec(w.shape)]
        args += [o_t, w]
    in_specs += [_const_spec((D_MODEL, D_FF)), _const_spec((D_MODEL, D_FF)), _const_spec((D_FF, D_MODEL))]
    args += [wg, wu, wd]
    return pl.pallas_call(
        functools.partial(_out_ffn_kernel, n_attn=len(attn)),
        grid=(bsz, n // tm),
        in_specs=in_specs,
        out_specs=_tok_spec(tm),
        out_shape=jax.ShapeDtypeStruct(h.shape, F32),
        compiler_params=_params(2),
        name="out_ffn",
    )(*args)


L0_ST_SQ = 0
L0_ST_SV = L0_ST_SQ + SWA_HEADS * SWA_HEAD_DIM
L0_ST_KR = L0_ST_SV + SWA_KV_HEADS * SWA_HEAD_DIM
L0_ST_SK = L0_ST_KR + LANES
L0_ST_ROWS = L0_ST_SK + SWA_KV_HEADS * LANES


def _l0_proj_kernel(*refs, rope):
    (h_ref, mod_ref, g_ref, w0_ref, wst_ref, qag_ref, wqbt_ref, kvag_ref, wkvt_ref,
     mqg_ref, mkg_ref, sqg_ref, skg_ref) = refs[:13]
    if rope:
        cmt, smt, cst, sst = [r[...] for r in refs[13:17]]
        outs = refs[17:]
    else:
        outs = refs[13:]
    qm_ref, km_ref, vmt_ref, qs_ref, ks_ref, vst_ref = outs

    hn = _adaln(h_ref[...], g_ref[...], mod_ref[3:4, :], mod_ref[4:5, :]).astype(BF16)
    tm = hn.shape[0]
    p = _dot(hn, w0_ref[...])
    st = _grouped_nt(wst_ref, hn)

    mla_inv = 1.0 / (MLA_NOPE + MLA_ROPE)
    swa_inv = 1.0 / SWA_HEAD_DIM
    bcast = lambda v: jnp.broadcast_to(v, (v.shape[0], tm))
    mqg = bcast(mqg_ref[...] * (mla_inv ** 0.5 * LOG2E))
    sqg = bcast(sqg_ref[...] * (swa_inv ** 0.5 * LOG2E))
    mkg = bcast(mkg_ref[...])
    skg = bcast(skg_ref[...])

    def mla_rope(t):
        return _rope_t(t, cmt, smt, MLA_ROPE_ROWS) if rope else t

    def swa_rope(t):
        return _rope_t(t, cst, sst, SWA_ROPE_ROWS, SWA_ROPE_ROWS) if rope else t

    for h in range(SWA_HEADS):
        t = swa_rope(_rms_t(st(L0_ST_SQ + h * SWA_HEAD_DIM, L0_ST_SQ + (h + 1) * SWA_HEAD_DIM), sqg, swa_inv))
        for i in range(tm // SWA_BLK):
            qs_ref[h, i] = t[:, i * SWA_BLK:(i + 1) * SWA_BLK].astype(BF16)
    for j in range(SWA_KV_HEADS):
        t = swa_rope(_rms_t(st(L0_ST_SK + j * LANES, L0_ST_SK + (j + 1) * LANES), skg, swa_inv))
        ks_ref[j] = t.T.astype(BF16)
        vt = st(L0_ST_SV + j * SWA_HEAD_DIM, L0_ST_SV + (j + 1) * SWA_HEAD_DIM)
        for i in range(tm // SWA_BLK):
            vst_ref[j, i] = vt[:, i * SWA_BLK:(i + 1) * SWA_BLK].astype(BF16)

    qa = _rms(p[:, 0:MLA_Q_RANK], qag_ref[...], 1.0 / MLA_Q_RANK).astype(BF16)
    qt = _grouped_nt(wqbt_ref, qa)
    for h in range(MLA_HEADS):
        t = mla_rope(_rms_t(qt(h * LANES, (h + 1) * LANES), mqg, mla_inv))
        tq = qm_ref.shape[-1]
        for j in range(tm // tq):
            qm_ref[h, j] = t[:, j * tq:(j + 1) * tq].astype(BF16)

    kva = _rms(p[:, MLA_Q_RANK:MLA_Q_RANK + MLA_KV_RANK], kvag_ref[...], 1.0 / MLA_KV_RANK).astype(BF16)
    kvt = _grouped_nt(wkvt_ref, kva)
    kr = st(L0_ST_KR, L0_ST_KR + LANES)
    for h in range(MLA_HEADS):
        t = mla_rope(_rms_t(kvt(h * LANES, (h + 1) * LANES) + kr, mkg, mla_inv))
        km_ref[h] = t.T.astype(BF16)
    v0 = MLA_HEADS * LANES
    for h in range(MLA_HEADS):
        vmt_ref[h] = kvt(v0 + h * MLA_V, v0 + (h + 1) * MLA_V).astype(BF16)


def _l0_proj(h, mod, per_batch, g, w, tables):
    bsz, n, _ = h.shape
    tm = min(TOK_TILE, n)
    rope = tables is not None
    weights = [g, w["w0"], w["wst"], w["qa_g"], w["wqbt"], w["kva_g"], w["wkvt"],
               w["mq_g"], w["mk_g"], w["sq_g"], w["sk_g"]]
    in_specs = [_tok_spec(tm), _mod_spec(per_batch)] + [_const_spec(a.shape) for a in weights]
    args = [h, mod] + weights
    if rope:
        in_specs += [_table_spec(i, tm) for _, i in tables]
        args += [a for a, _ in tables]
    k_spec = lambda nh: pl.BlockSpec((None, nh, tm, LANES), lambda b, t: (b, 0, t, 0))
    qt_spec = lambda rows, tq: pl.BlockSpec((None, MLA_HEADS, tm // tq, rows, tq), lambda b, t: (b, 0, t, 0, 0))
    tq = min(MLA_TQ, n)
    out_specs = [qt_spec(LANES, tq), k_spec(MLA_HEADS),
                 pl.BlockSpec((None, MLA_HEADS, MLA_V, tm), lambda b, t: (b, 0, 0, t)),
                 qt_spec(SWA_HEAD_DIM, SWA_BLK), k_spec(SWA_KV_HEADS),
                 pl.BlockSpec((None, SWA_KV_HEADS, tm // SWA_BLK, SWA_HEAD_DIM, SWA_BLK),
                              lambda b, t: (b, 0, t, 0, 0))]
    out_shape = [jax.ShapeDtypeStruct((bsz, MLA_HEADS, n // tq, LANES, tq), BF16),
                 jax.ShapeDtypeStruct((bsz, MLA_HEADS, n, LANES), BF16),
                 jax.ShapeDtypeStruct((bsz, MLA_HEADS, MLA_V, n), BF16),
                 jax.ShapeDtypeStruct((bsz, SWA_HEADS, n // SWA_BLK, SWA_HEAD_DIM, SWA_BLK), BF16),
                 jax.ShapeDtypeStruct((bsz, SWA_KV_HEADS, n, LANES), BF16),
                 jax.ShapeDtypeStruct((bsz, SWA_KV_HEADS, n // SWA_BLK, SWA_HEAD_DIM, SWA_BLK), BF16)]
    return pl.pallas_call(
        functools.partial(_l0_proj_kernel, rope=rope),
        grid=(bsz, n // tm),
        in_specs=in_specs, out_specs=out_specs, out_shape=out_shape,
        compiler_params=_params(2),
        name="l0_proj",
    )(*args)


def _interleave(*stages):
    live = list(stages)
    while live:
        for st in list(live):
            try:
                next(st)
            except StopIteration:
                live.remove(st)


def _two_stage_loop(n_items, produce, consume, slot0, slot1):
    _interleave(produce(0, slot0))

    def pair(j, carry):
        i = 2 * j
        _interleave(produce(i + 1, slot1), consume(i, slot0))
        _interleave(produce(jnp.minimum(i + 2, n_items - 1), slot0), consume(i + 1, slot1))
        return carry

    lax.fori_loop(0, n_items // 2, pair, 0)


def _chunk_fold(x, op):
    return op(x.reshape(x.shape[0] // SUBLANES, SUBLANES, x.shape[-1]), axis=0)


def _score_chunks(s_ref, m_ref, key_chunk, qt, chunks, bias=None, floor=None):
    mx = None
    for k0, size in chunks:
        c = _dot(key_chunk(k0, size), qt)
        b = None if bias is None else bias(k0, size)
        if b is not None:
            c = c + b
        s_ref[k0:k0 + size, :] = c
        part = _chunk_fold(c, jnp.max)
        mx = part if mx is None else jnp.maximum(mx, part)
        yield
    m = jnp.max(mx, axis=0, keepdims=True)
    m_ref[...] = m if floor is None else jnp.maximum(m, floor)


def _exp_pv_chunks(s_ref, m_ref, value_t, chunk, out):
    m = m_ref[...]
    l8 = None
    o = None
    for k0 in range(0, s_ref.shape[0], chunk):
        e = jnp.exp2(s_ref[k0:k0 + chunk, :] - m)
        part = _chunk_fold(e, jnp.sum)
        l8 = part if l8 is None else l8 + part
        d = _dot(value_t(k0), e.astype(BF16))
        o = d if o is None else o + d
        yield
    out.append((o, jnp.sum(l8, axis=0, keepdims=True)))


KEY_CHUNK = 256


def _ctx_then_latent(c_ref, l_ref, n_ctx, token_major):
    def chunk(k0, size=KEY_CHUNK):
        ref, k = (c_ref, k0) if k0 < n_ctx else (l_ref, k0 - n_ctx)
        return ref[k:k + size, :] if token_major else ref[:, k:k + size]
    return chunk


SCORE_CHUNK = KEY_CHUNK


def _score_key_chunks(n_ctx, n_keys):
    assert n_ctx <= SCORE_CHUNK
    return [(0, n_ctx)] + [(k0, min(SCORE_CHUNK, n_keys - k0)) for k0 in range(n_ctx, n_keys, SCORE_CHUNK)]


def _mla_attn_kernel(q_ref, kc_ref, vct_ref, kl_ref, vlt_ref, o_ref, s0_ref, m0_ref, s1_ref, m1_ref):
    n_tiles, n_heads = o_ref.shape[:2]
    n_ctx = kc_ref.shape[1]
    n_keys = s0_ref.shape[0]
    score_chunks = _score_key_chunks(n_ctx, n_keys)

    def split(i):
        return lax.div(i, n_tiles), lax.rem(i, n_tiles)

    def scores(i, slot):
        s_ref, m_ref = slot
        h, t = split(i)
        keys = _ctx_then_latent(kc_ref.at[h], kl_ref.at[h], n_ctx, True)
        yield from _score_chunks(s_ref, m_ref, keys, q_ref[h, t], score_chunks)

    def softmax_pv(i, slot):
        s_ref, m_ref = slot
        h, t = split(i)
        values = _ctx_then_latent(vct_ref.at[h], vlt_ref.at[h], n_ctx, False)
        res = []
        yield from _exp_pv_chunks(s_ref, m_ref, values, KEY_CHUNK, res)
        o, l = res[0]
        o_ref[t, h] = (o * (1.0 / l)).astype(BF16)

    _two_stage_loop(n_tiles * n_heads, scores, softmax_pv, (s0_ref, m0_ref), (s1_ref, m1_ref))


def _mla_attn(q, kc, vct, kl, vlt):
    bsz, nh, n_tiles = q.shape[:3]
    hs = ATT_HEADS_PER_STEP
    n_keys = kc.shape[2] + kl.shape[2]
    grp = lambda shape: pl.BlockSpec((None, hs) + tuple(shape), lambda b, g: (b, g) + (0,) * len(shape))
    tq = q.shape[-1]
    slot = [pltpu.VMEM((n_keys, tq), F32), pltpu.VMEM((1, tq), F32)]
    return pl.pallas_call(
        _mla_attn_kernel,
        grid=(bsz, nh // hs),
        in_specs=[grp(q.shape[2:]), grp(kc.shape[2:]), grp(vct.shape[2:]), grp(kl.shape[2:]), grp(vlt.shape[2:])],
        out_specs=pl.BlockSpec((None, n_tiles, hs, MLA_V, tq), lambda b, g: (b, 0, g, 0, 0)),
        out_shape=jax.ShapeDtypeStruct((bsz, n_tiles, nh, MLA_V, tq), BF16),
        scratch_shapes=slot + slot,
        compiler_params=_params(2),
        name="mla_attn",
    )(q, kc, vct, kl, vlt)


SWA_BAND = 3 * SWA_BLK
SWA_NQ = SWA_GROUP * SWA_BLK


def _swa_query_block(heads):
    qt = jnp.concatenate(heads, axis=1)
    return jnp.concatenate([qt, jnp.zeros((LANES - qt.shape[0], qt.shape[1]), qt.dtype)], axis=0)


def _sink_row(sink_ref, j):
    return jnp.concatenate(
        [jnp.full((1, SWA_BLK), sink_ref[j * SWA_GROUP + g] * LOG2E, F32) for g in range(SWA_GROUP)], axis=1)


def _swa_attn_kernel(sink_ref, q_ref, kc_ref, vct_ref, kl_ref, vlt_ref, o_ref,
                     bias_ref, s0_ref, m0_ref, s1_ref, m1_ref):
    n_blk = o_ref.shape[0]
    n_ctx = kc_ref.shape[1]

    @pl.when(pl.program_id(0) == 0)
    def _():
        r = lax.broadcasted_iota(jnp.int32, (SWA_BAND, SWA_NQ), 0)
        c = lax.broadcasted_iota(jnp.int32, (SWA_BAND, SWA_NQ), 1) & (SWA_BLK - 1)
        for case in range(3):
            dist = r - c - case * SWA_BLK
            bias_ref[case] = jnp.where(jnp.abs(dist) <= SWA_WINDOW, 0.0, MASKED)

    def split(i):
        return lax.div(i, n_blk), lax.rem(i, n_blk)

    def band_start(n):
        return jnp.clip(n - 1, 0, n_blk - 3)

    def scores(i, slot):
        s_ref, m_ref = slot
        j, n = split(i)
        qt = _swa_query_block([q_ref[j * SWA_GROUP + g, n] for g in range(SWA_GROUP)])
        band0 = band_start(n) * SWA_BLK
        case = (n > 0).astype(jnp.int32) + (n == n_blk - 1).astype(jnp.int32)

        def keys(k0, size):
            if k0 < n_ctx:
                return kc_ref[j, k0:k0 + size, :]
            return kl_ref[j, pl.ds(pl.multiple_of(band0 + (k0 - n_ctx), SWA_BLK), size), :]

        def bias(k0, size):
            return None if k0 < n_ctx else bias_ref[case, k0 - n_ctx:k0 - n_ctx + size, :]

        chunks = [(0, n_ctx), (n_ctx, SWA_BAND)]
        yield from _score_chunks(s_ref, m_ref, keys, qt, chunks, bias, _sink_row(sink_ref, j))

    def softmax_pv(i, slot):
        s_ref, m_ref = slot
        j, n = split(i)
        b0 = band_start(n)

        def value_t(k0):
            if k0 < n_ctx:
                return vct_ref[j, :, k0:k0 + SWA_BLK]
            return vlt_ref[j, b0 + (k0 - n_ctx) // SWA_BLK]

        res = []
        yield from _exp_pv_chunks(s_ref, m_ref, value_t, SWA_BLK, res)
        o, l = res[0]
        o = o * (1.0 / (l + jnp.exp2(_sink_row(sink_ref, j) - m_ref[...])))
        for g in range(SWA_GROUP):
            o_ref[n, j * SWA_GROUP + g] = o[:, g * SWA_BLK:(g + 1) * SWA_BLK].astype(BF16)

    _two_stage_loop(SWA_KV_HEADS * n_blk, scores, softmax_pv, (s0_ref, m0_ref), (s1_ref, m1_ref))


def _swa_attn(sink, q, kc, vct, kl, vlt):
    bsz, nh, n_blk = q.shape[:3]
    n_keys = kc.shape[2] + SWA_BAND
    full = lambda a: pl.BlockSpec((None,) + a.shape[1:], lambda b: (b,) + (0,) * (a.ndim - 1))
    slot = [pltpu.VMEM((n_keys, SWA_NQ), F32), pltpu.VMEM((1, SWA_NQ), F32)]
    return pl.pallas_call(
        _swa_attn_kernel,
        grid=(bsz,),
        in_specs=[pl.BlockSpec(memory_space=pltpu.SMEM), full(q), full(kc), full(vct), full(kl), full(vlt)],
        out_specs=pl.BlockSpec((None, n_blk, nh, SWA_HEAD_DIM, SWA_BLK), lambda b: (b, 0, 0, 0, 0)),
        out_shape=jax.ShapeDtypeStruct((bsz, n_blk, nh, SWA_HEAD_DIM, SWA_BLK), BF16),
        scratch_shapes=[pltpu.VMEM((3, SWA_BAND, SWA_NQ), F32)] + slot + slot,
        compiler_params=_params(1),
        name="swa_attn",
    )(sink, q, kc, vct, kl, vlt)


def _ctx_attn_kernel(sink_ref, qm_ref, kmc_ref, vmct_ref, qs_ref, ksc_ref, vsct_ref, om_ref, os_ref):
    for h in range(om_ref.shape[0]):
        sc = _dot(kmc_ref[h], qm_ref[h])
        pc = jnp.exp2(sc - jnp.max(sc, axis=0, keepdims=True))
        l = jnp.sum(pc, axis=0, keepdims=True)
        o = _dot(vmct_ref[h], pc.astype(BF16))
        om_ref[h] = (o * (1.0 / l)).astype(BF16)
    for j in range(SWA_KV_HEADS):
        sink = _sink_row(sink_ref, j)
        for n in range(os_ref.shape[0]):
            qt = _swa_query_block([qs_ref[j * SWA_GROUP + g, n] for g in range(SWA_GROUP)])
            s = _dot(ksc_ref[j], qt)
            m = jnp.maximum(jnp.max(s, axis=0, keepdims=True), sink)
            p = jnp.exp2(s - m)
            l = jnp.sum(p, axis=0, keepdims=True) + jnp.exp2(sink - m)
            o = _dot(vsct_ref[j], p.astype(BF16)) * (1.0 / l)
            for g in range(SWA_GROUP):
                os_ref[n, j * SWA_GROUP + g] = o[:, g * SWA_BLK:(g + 1) * SWA_BLK].astype(BF16)


def _ctx_attn(sink, qm, kmc, vmct, qs, ksc, vsct):
    bsz, nh, n_tiles = qm.shape[:3]
    assert n_tiles == 1
    tq = qm.shape[-1]
    n_blk = qs.shape[2]
    per_b = lambda a: pl.BlockSpec((None,) + a.shape[1:], lambda b: (b,) + (0,) * (a.ndim - 1))
    return pl.pallas_call(
        _ctx_attn_kernel,
        grid=(bsz,),
        in_specs=[pl.BlockSpec(memory_space=pltpu.SMEM),
                  pl.BlockSpec((None, nh, None) + qm.shape[3:], lambda b: (b, 0, 0, 0, 0)),
                  per_b(kmc), per_b(vmct), per_b(qs), per_b(ksc), per_b(vsct)],
        out_specs=[pl.BlockSpec((None, None, nh, MLA_V, tq), lambda b: (b, 0, 0, 0, 0)),
                   pl.BlockSpec((None, n_blk, SWA_HEADS, SWA_HEAD_DIM, SWA_BLK), lambda b: (b, 0, 0, 0, 0))],
        out_shape=[jax.ShapeDtypeStruct((bsz, 1, nh, MLA_V, tq), BF16),
                   jax.ShapeDtypeStruct((bsz, n_blk, SWA_HEADS, SWA_HEAD_DIM, SWA_BLK), BF16)],
        compiler_params=_params(1),
        name="ctx_attn",
    )(sink, qm, kmc, vmct, qs, ksc, vsct)


DIFF_Q = HALF // 2


def _diff_norm_t(t, g):
    sq = t * t
    grp = [jnp.sum(sq[i * DIFF_Q:(i + 1) * DIFF_Q], axis=0, keepdims=True) for i in range(4)]
    inv = 1.0 / DIFF_HEAD_DIM
    r0 = lax.rsqrt((grp[0] + grp[2]) * inv + EPS)
    r1 = lax.rsqrt((grp[1] + grp[3]) * inv + EPS)
    parts = [t[i * DIFF_Q:(i + 1) * DIFF_Q] * (r0 if i % 2 == 0 else r1) for i in range(4)]
    return jnp.concatenate(parts, axis=0) * g


def _l1_proj_kernel(*refs, latent):
    if latent:
        h_ref, mod_ref, g_ref, wt_ref, qg_ref, kg_ref, cdt_ref, sdt_ref, q_ref, k_ref, vt_ref = refs
        cdt, sdt = cdt_ref[...], sdt_ref[...]
    else:
        h_ref, mod_ref, g_ref, wt_ref, kg_ref, k_ref, vt_ref = refs
    hn = _adaln(h_ref[...], g_ref[...], mod_ref[3:4, :], mod_ref[4:5, :]).astype(BF16)
    tm = hn.shape[0]
    width = DIFF_HEADS * LANES
    ft = _grouped_nt(wt_ref, hn)
    bcast = lambda v: jnp.broadcast_to(v, (LANES, tm))
    k0 = 0
    if latent:
        qg = bcast(qg_ref[...] * (DIFF_HEAD_DIM ** -0.5 * LOG2E))
        for h in range(DIFF_HEADS):
            t = _rope_t(_diff_norm_t(ft(h * LANES, (h + 1) * LANES), qg), cdt, sdt, DIFF_ROPE_ROWS)
            tq = q_ref.shape[-1]
            for j in range(tm // tq):
                q_ref[h, j] = t[:, j * tq:(j + 1) * tq].astype(BF16)
        k0 = width

    kg = bcast(kg_ref[...])
    for h in range(DIFF_HEADS):
        t = _diff_norm_t(ft(k0 + h * LANES, k0 + (h + 1) * LANES), kg)
        if latent:
            t = _rope_t(t, cdt, sdt, DIFF_ROPE_ROWS)
        k_ref[h] = t.T.astype(BF16)
    v0 = k0 + width
    for h in range(DIFF_HEADS):
        vt_ref[h] = ft(v0 + h * LANES, v0 + (h + 1) * LANES).astype(BF16)


def _l1_proj(h, mod, per_batch, g, w, tables):
    bsz, n, _ = h.shape
    tm = min(TOK_TILE, n)
    latent = tables is not None
    k_spec = pl.BlockSpec((None, DIFF_HEADS, tm, LANES), lambda b, t: (b, 0, t, 0))
    vt_spec = pl.BlockSpec((None, DIFF_HEADS, LANES, tm), lambda b, t: (b, 0, 0, t))
    k_shape = jax.ShapeDtypeStruct((bsz, DIFF_HEADS, n, LANES), BF16)
    vt_shape = jax.ShapeDtypeStruct((bsz, DIFF_HEADS, LANES, n), BF16)
    if latent:
        weights = [g, w["wqkvt"], w["q_g"], w["k_g"]]
        in_specs = ([_tok_spec(tm), _mod_spec(per_batch)] + [_const_spec(a.shape) for a in weights]
                    + [_table_spec(i, tm) for _, i in tables])
        args = [h, mod] + weights + [a for a, _ in tables]
        tq = min(DIFF_TQ, n)
        out_specs = [pl.BlockSpec((None, DIFF_HEADS, tm // tq, LANES, tq), lambda b, t: (b, 0, t, 0, 0)),
                     k_spec, vt_spec]
        out_shape = [jax.ShapeDtypeStruct((bsz, DIFF_HEADS, n // tq, LANES, tq), BF16), k_shape, vt_shape]
    else:
        weights = [g, w["wkvt"], w["k_g"]]
        in_specs = [_tok_spec(tm), _mod_spec(per_batch)] + [_const_spec(a.shape) for a in weights]
        args = [h, mod] + weights
        out_specs = [k_spec, vt_spec]
        out_shape = [k_shape, vt_shape]
    return pl.pallas_call(
        functools.partial(_l1_proj_kernel, latent=latent),
        grid=(bsz, n // tm),
        in_specs=in_specs, out_specs=out_specs, out_shape=out_shape,
        compiler_params=_params(2),
        name="l1_proj",
    )(*args)


def _diff_attn_kernel(lam_ref, subg_ref, q_ref, kc_ref, vct_ref, kl_ref, vlt_ref, o_ref,
                      s0_ref, m0_ref, s1_ref, m1_ref):
    n_tiles, n_heads = o_ref.shape[:2]
    n_ctx = kc_ref.shape[1]
    n_keys = s0_ref.shape[1]
    score_chunks = _score_key_chunks(n_ctx, n_keys)
    feature = lax.broadcasted_iota(jnp.int32, q_ref.shape[2:], 0)
    sub0 = (feature & (HALF - 1)) < DIFF_Q
    lv = lam_ref[...]
    lam = (jnp.exp(jnp.sum(lv[0:1] * lv[1:2], axis=-1, keepdims=True))
           - jnp.exp(jnp.sum(lv[2:3] * lv[3:4], axis=-1, keepdims=True)) + LAMBDA_INIT_L1)
    subg = subg_ref[...] * (1.0 - LAMBDA_INIT_L1)

    def split(i):
        return lax.div(i, n_tiles), lax.rem(i, n_tiles)

    def scores(i, slot):
        s_ref, m_ref = slot
        h, t = split(i)
        qt = q_ref[h, t]
        zero = jnp.zeros_like(qt)
        keys = _ctx_then_latent(kc_ref.at[h], kl_ref.at[h], n_ctx, True)
        for s, qs in enumerate((jnp.where(sub0, qt, zero), jnp.where(sub0, zero, qt))):
            yield from _score_chunks(s_ref.at[s], m_ref.at[s], keys, qs, score_chunks)

    def softmax_pv(i, slot):
        s_ref, m_ref = slot
        h, t = split(i)
        values = _ctx_then_latent(vct_ref.at[h], vlt_ref.at[h], n_ctx, False)
        res = []
        for s in range(2):
            yield from _exp_pv_chunks(s_ref.at[s], m_ref.at[s], values, KEY_CHUNK, res)
        (o0, l0), (o1, l1) = res
        o = o0 * (1.0 / l0) - o1 * (lam * (1.0 / l1))
        ms = jnp.mean(o * o, axis=0, keepdims=True)
        o_ref[t, h] = (o * lax.rsqrt(ms + EPS) * subg).astype(BF16)

    _two_stage_loop(n_tiles * n_heads, scores, softmax_pv, (s0_ref, m0_ref), (s1_ref, m1_ref))


def _diff_attn(lamv, subg, q, kc, vct, kl, vlt):
    bsz, nh, n_tiles = q.shape[:3]
    hs = ATT_HEADS_PER_STEP
    n_keys = kc.shape[2] + kl.shape[2]
    grp = lambda shape: pl.BlockSpec((None, hs) + tuple(shape), lambda b, g: (b, g) + (0,) * len(shape))
    tq = q.shape[-1]
    slot = [pltpu.VMEM((2, n_keys, tq), F32), pltpu.VMEM((2, 1, tq), F32)]
    return pl.pallas_call(
        _diff_attn_kernel,
        grid=(bsz, nh // hs),
        in_specs=[_const_spec(lamv.shape), _const_spec(subg.shape),
                  grp(q.shape[2:]), grp(kc.shape[2:]), grp(vct.shape[2:]), grp(kl.shape[2:]), grp(vlt.shape[2:])],
        out_specs=pl.BlockSpec((None, n_tiles, hs, LANES, tq), lambda b, g: (b, 0, g, 0, 0)),
        out_shape=jax.ShapeDtypeStruct((bsz, n_tiles, nh, LANES, tq), BF16),
        scratch_shapes=slot + slot,
        compiler_params=_params(2),
        name="diff_attn",
    )(lamv, subg, q, kc, vct, kl, vlt)


def _prep_l0(w_in, qa_g, wqb, kva_g, wkvb, q_g, k_g, sq_g, sk_g, w_out):
    mla = _mla_lane_map()
    swa = _swa_lane_map()
    kr_cols = np.where(mla >= MLA_NOPE, 384 + mla - MLA_NOPE, -1)
    wst_cols = np.concatenate([
        416 + np.arange(SWA_HEADS * SWA_HEAD_DIM),
        np.arange(1056, 1184),
        kr_cols,
        _per_head(swa, SWA_KV_HEADS, SWA_HEAD_DIM, base=928),
    ])
    assert wst_cols.shape[0] == L0_ST_ROWS
    nope_map = np.where((mla >= 0) & (mla < MLA_NOPE), mla, -1)
    v_cols = np.concatenate([h * (MLA_NOPE + MLA_V) + MLA_NOPE + np.arange(MLA_V) for h in range(MLA_HEADS)])
    wkvt_cols = np.concatenate([_per_head(nope_map, MLA_HEADS, MLA_NOPE + MLA_V), v_cols])
    row = lambda v: v.reshape(1, -1).astype(F32)
    col = lambda v: v.reshape(-1, 1).astype(F32)
    return {
        "w0": w_in[:, 0:MLA_Q_RANK + MLA_KV_RANK].astype(BF16),
        "wst": _cols_as_rows(w_in, wst_cols),
        "qa_g": row(qa_g),
        "wqbt": _cols_as_rows(wqb, _per_head(mla, MLA_HEADS, MLA_NOPE + MLA_ROPE)),
        "kva_g": row(kva_g),
        "wkvt": _cols_as_rows(wkvb, wkvt_cols),
        "mq_g": col(_take_cols(q_g, mla)),
        "mk_g": col(_take_cols(k_g, mla)),
        "sq_g": col(sq_g),
        "sk_g": col(_take_cols(sk_g, swa)),
        "wa": w_out[:MLA_HEADS * MLA_V].astype(BF16),
        "wb": w_out[MLA_HEADS * MLA_V:].astype(BF16),
    }


def _prep_l1(w_in, q_g, k_g, w_out):
    dm = _diff_lane_map()
    width = DIFF_HEADS * LANES
    cols = np.concatenate([_per_head(dm, DIFF_HEADS, LANES), _per_head(dm, DIFF_HEADS, LANES, base=width),
                           np.arange(2 * width, 3 * width)])
    wqkvt = _cols_as_rows(w_in, cols)
    return {
        "wqkvt": wqkvt,
        "wkvt": wqkvt[width:],
        "q_g": _take_cols(q_g, dm % DIFF_HEAD_DIM).reshape(-1, 1).astype(F32),
        "k_g": _take_cols(k_g, dm % DIFF_HEAD_DIM).reshape(-1, 1).astype(F32),
        "w_out": w_out.astype(BF16),
    }


def kernel(x, c, ctx, c_ctx, l0_ada_w, l0_ada_b, l0_norm_g, l0_ffn_wg, l0_ffn_wu, l0_ffn_wd, l0_w_in, l0_mla_qa_g, l0_mla_wqb, l0_mla_kva_g, l0_mla_wkvb, l0_mla_q_g, l0_mla_k_g, l0_swa_q_g, l0_swa_k_g, l0_swa_sink, l0_w_out, l1_ada_w, l1_ada_b, l1_norm_g, l1_ffn_wg, l1_ffn_wu, l1_ffn_wd, l1_w_in, l1_q_g, l1_k_g, l1_lambda_q1, l1_lambda_k1, l1_lambda_q2, l1_lambda_k2, l1_subln_g, l1_w_out):
    bsz, seq, _ = x.shape

    pad = (-(bsz + 1)) % SUBLANES
    cc = jnp.concatenate([c, c_ctx[None, :], jnp.zeros((pad, D_MODEL), F32)], axis=0)

    def mods(ada_w, ada_b):
        m = _ada(cc, ada_w, ada_b).reshape(cc.shape[0], N_MOD, D_MODEL)
        return m[:bsz], m[bsz:bsz + 1]

    def ffn_weights(wg, wu, wd, i):
        return wg[i].astype(BF16), wu[i].astype(BF16), wd[i].astype(BF16)

    mla_tab, swa_tab, diff_tab = _rope_tables(seq, [
        (MLA_ROPE, [(0, 16)], [(64, 16)]),
        (SWA_HEAD_DIM, [(0, 32)], [(32, 32)]),
        (DIFF_HEAD_DIM, [(0, 32), (32, 32)], [(64, 32), (96, 32)]),
    ])

    mod, mod_c = mods(l0_ada_w, l0_ada_b)
    g = [l0_norm_g[i:i + 1] for i in range(3)]
    w = _prep_l0(l0_w_in, l0_mla_qa_g, l0_mla_wqb, l0_mla_kva_g, l0_mla_wkvb, l0_mla_q_g, l0_mla_k_g,
                 l0_swa_q_g, l0_swa_k_g, l0_w_out)
    f0 = ffn_weights(l0_ffn_wg, l0_ffn_wu, l0_ffn_wd, 0)
    f1 = ffn_weights(l0_ffn_wg, l0_ffn_wu, l0_ffn_wd, 1)

    h = _ffn(x, mod, True, g[0], *f0, k=0)
    hc = _ffn(ctx, mod_c, False, g[0], *f0, k=0)

    qm, km, vmt, qs, ks, vst = _l0_proj(h, mod, True, g[1], w, mla_tab + swa_tab)
    qm_c, km_c, vmt_c, qs_c, ks_c, vst_c = _l0_proj(hc, mod_c, False, g[1], w, None)
    vsc_t = vst_c.transpose(0, 1, 3, 2, 4).reshape(bsz, SWA_KV_HEADS, SWA_HEAD_DIM, -1)

    at = _mla_attn(qm, km_c, vmt_c, km, vmt)
    bt = _swa_attn(l0_swa_sink, qs, ks_c, vsc_t, ks, vst)
    at_c, bt_c = _ctx_attn(l0_swa_sink, qm_c, km_c, vmt_c, qs_c, ks_c, vsc_t)

    h = _out_ffn(h, mod, True, g[2], [(at, w["wa"]), (bt, w["wb"])], *f1)
    hc = _out_ffn(hc, mod_c, False, g[2], [(at_c, w["wa"]), (bt_c, w["wb"])], *f1)

    mod, mod_c = mods(l1_ada_w, l1_ada_b)
    g = [l1_norm_g[i:i + 1] for i in range(3)]
    w = _prep_l1(l1_w_in, l1_q_g, l1_k_g, l1_w_out)
    f0 = ffn_weights(l1_ffn_wg, l1_ffn_wu, l1_ffn_wd, 0)
    f1 = ffn_weights(l1_ffn_wg, l1_ffn_wu, l1_ffn_wd, 1)

    h = _ffn(h, mod, True, g[0], *f0, k=0)
    hc = _ffn(hc, mod_c, False, g[0], *f0, k=0)

    qd, kd, vdt = _l1_proj(h, mod, True, g[1], w, diff_tab)
    kd_c, vdt_c = _l1_proj(hc, mod_c, False, g[1], w, None)
    lamv = jnp.stack([l1_lambda_q1, l1_lambda_k1, l1_lambda_q2, l1_lambda_k2]).astype(F32)
    ot = _diff_attn(lamv, l1_subln_g.reshape(-1, 1).astype(F32), qd, kd_c, vdt_c, kd, vdt)
    return _out_ffn(h, mod, True, g[2], [(ot, w["w_out"])], *f1)
```

```python
import functools
import math

import numpy as np
import jax
import jax.numpy as jnp
from jax import lax
from jax.experimental import pallas as pl
from jax.experimental.pallas import tpu as pltpu

F32 = jnp.float32
BF16 = jnp.bfloat16

D_MODEL = 1024
D_FF = 2816
N_MOD = 9
EPS = 1e-6
ROPE_BASE = 10000.0
GRID_W = 64
LOG2E = 1.4426950408889634

MLA_HEADS = 8
MLA_Q_RANK = 256
MLA_KV_RANK = 128
MLA_NOPE = 64
MLA_ROPE = 32
MLA_V = 64
SWA_HEADS = 8
SWA_KV_HEADS = 2
SWA_GROUP = SWA_HEADS // SWA_KV_HEADS
SWA_HEAD_DIM = 64
SWA_WINDOW = 128
DIFF_HEADS = 8
DIFF_HEAD_DIM = 64
LAMBDA_INIT_L1 = 0.8 - 0.6 * math.exp(-0.3 * 1)

LANES = 128
SUBLANES = 8
HALF = LANES // 2
MLA_TQ = 512
DIFF_TQ = 256
SWA_BLK = 128
TOK_TILE = 1024
FFN_TILE = 1024
ATT_HEADS_PER_STEP = 8
VMEM_LIMIT = 56 * 1024 * 1024
MASKED = -1e30

NT_DIMS = (((1,), (1,)), ((), ()))
TN_DIMS = (((0,), (0,)), ((), ()))


def _mla_lane_map():
    m = np.full(LANES, -1, np.int64)
    m[0:16] = 64 + np.arange(16)
    m[16:48] = np.arange(32)
    m[64:80] = 80 + np.arange(16)
    m[80:112] = 32 + np.arange(32)
    return m


def _swa_lane_map():
    m = np.full(LANES, -1, np.int64)
    m[0:SWA_HEAD_DIM] = np.arange(SWA_HEAD_DIM)
    return m


def _diff_lane_map():
    m = np.zeros(LANES, np.int64)
    m[0:32] = np.arange(32)
    m[32:64] = 64 + np.arange(32)
    m[64:96] = 32 + np.arange(32)
    m[96:128] = 96 + np.arange(32)
    return m


MLA_ROPE_ROWS = MLA_ROPE // 2
SWA_ROPE_ROWS = SWA_HEAD_DIM // 2
DIFF_ROPE_ROWS = HALF


def _take_cols(w, idx):
    idx = np.asarray(idx)
    out = jnp.take(w, jnp.asarray(np.where(idx < 0, 0, idx), jnp.int32), axis=-1)
    return jnp.where(jnp.asarray(idx >= 0), out, jnp.zeros((), w.dtype))


def _cols_as_rows(w, idx):
    idx = np.asarray(idx)
    wt = jnp.concatenate([w.astype(BF16).T, jnp.zeros((1, w.shape[0]), BF16)], axis=0)
    return jnp.take(wt, jnp.asarray(np.where(idx < 0, w.shape[1], idx), jnp.int32), axis=0)


def _per_head(lane_map, n_heads, stride, base=0):
    cols = [np.where(lane_map >= 0, base + h * stride + lane_map, -1) for h in range(n_heads)]
    return np.concatenate(cols)


def _adaln(x, g, shift, scale):
    ms = jnp.mean(x * x, axis=-1, keepdims=True)
    return x * lax.rsqrt(ms + EPS) * g * (1.0 + scale) + shift


def _rms(t, g, inv_dim):
    ms = jnp.sum(t * t, axis=-1, keepdims=True) * inv_dim
    return t * lax.rsqrt(ms + EPS) * g


def _rms_t(t, g, inv_dim):
    ms = jnp.sum(t * t, axis=0, keepdims=True) * inv_dim
    return t * lax.rsqrt(ms + EPS) * g


def _rope_t(t, cos, sin, rows, partner=HALF):
    a, b = t[0:rows], t[partner:partner + rows]
    a2 = a * cos[0:rows] + b * sin[0:rows]
    b2 = b * cos[partner:partner + rows] + a * sin[partner:partner + rows]
    parts = [a2, t[rows:partner], b2, t[partner + rows:]]
    return jnp.concatenate([p for p in parts if p.shape[0] > 0], axis=0)


def _dot(a, b):
    return jnp.dot(a, b, preferred_element_type=F32)


def _dot_nt(a, b):
    return lax.dot_general(a, b, NT_DIMS, preferred_element_type=F32)


def _dot_tn(a, b):
    return lax.dot_general(a, b, TN_DIMS, preferred_element_type=F32)


NT_GROUP_ROWS = 512


def _grouped_nt(w_ref, x):
    done = {}

    def rows(r0, r1):
        g = r0 // NT_GROUP_ROWS
        assert (r1 - 1) // NT_GROUP_ROWS == g
        if g not in done:
            done[g] = _dot_nt(w_ref[g * NT_GROUP_ROWS:(g + 1) * NT_GROUP_ROWS, :], x)
        return done[g][r0 - g * NT_GROUP_ROWS:r1 - g * NT_GROUP_ROWS]

    return rows


def _params(n_axes):
    return pltpu.CompilerParams(dimension_semantics=("arbitrary",) * n_axes,
                                vmem_limit_bytes=VMEM_LIMIT)


def _const_spec(shape):
    nd = len(shape)
    return pl.BlockSpec(shape, lambda *_: (0,) * nd, pipeline_mode=pl.Buffered(1))


def _mod_spec(per_batch):
    if per_batch:
        return pl.BlockSpec((None, N_MOD, D_MODEL), lambda b, t: (b, 0, 0))
    return pl.BlockSpec((None, N_MOD, D_MODEL), lambda b, t: (0, 0, 0))


def _tok_spec(tm):
    return pl.BlockSpec((None, tm, D_MODEL), lambda b, t: (b, t, 0))


def _ada_kernel(c_ref, w_ref, b_ref, o_ref):
    c = c_ref[...]
    a = (c * jax.nn.sigmoid(c)).astype(BF16)
    o_ref[...] = _dot(a, w_ref[...].astype(BF16)) + b_ref[...]


def _ada(cc, w, b):
    rows = cc.shape[0]
    n = w.shape[1]
    tn = 1152
    return pl.pallas_call(
        _ada_kernel,
        grid=(n // tn,),
        in_specs=[pl.BlockSpec((rows, D_MODEL), lambda j: (0, 0)),
                  pl.BlockSpec((D_MODEL, tn), lambda j: (0, j)),
                  pl.BlockSpec((1, tn), lambda j: (0, j))],
        out_specs=pl.BlockSpec((rows, tn), lambda j: (0, j)),
        out_shape=jax.ShapeDtypeStruct((rows, n), F32),
        compiler_params=_params(1),
        name="ada_mod",
    )(cc, w, b.reshape(1, n))


def _rope_table_kernel(inv_ref, userow_ref, sign_ref, cos_ref, sin_ref):
    n_layouts, _, seq = cos_ref.shape
    t = lax.broadcasted_iota(jnp.int32, (LANES, seq), 1)
    row = lax.shift_right_logical(t, GRID_W.bit_length() - 1).astype(F32)
    col = (t & (GRID_W - 1)).astype(F32)
    for i in range(n_layouts):
        pos = jnp.where(userow_ref[i] > 0.5, row, col)
        ang = pos * inv_ref[i]
        sign = sign_ref[i]
        active = sign != 0.0
        cos_ref[i] = jnp.where(active, jnp.cos(ang), 1.0)
        sin_ref[i] = jnp.where(active, sign * jnp.sin(ang), 0.0)


def _rope_tables(seq, layouts):
    n = len(layouts)
    inv = np.zeros((n, LANES, 1), np.float32)
    use_row = np.zeros((n, LANES, 1), np.float32)
    sign = np.zeros((n, LANES, 1), np.float32)
    for i, (rot_dim, x1_slots, x2_slots) in enumerate(layouts):
        n_f = rot_dim // 4
        inv_f = (ROPE_BASE ** (-np.arange(n_f, dtype=np.float64) / n_f)).astype(np.float32)
        inv_half = np.concatenate([inv_f, inv_f])
        use_row_half = np.concatenate([np.ones(n_f), np.zeros(n_f)])
        for slots, sgn in ((x1_slots, -1.0), (x2_slots, 1.0)):
            for start, width in slots:
                assert width == rot_dim // 2
                inv[i, start:start + width, 0] = inv_half
                use_row[i, start:start + width, 0] = use_row_half
                sign[i, start:start + width, 0] = sgn
    vec = pl.BlockSpec((n, LANES, 1), lambda: (0, 0, 0))
    tab = pl.BlockSpec((n, LANES, seq), lambda: (0, 0, 0))
    cos, sin = pl.pallas_call(
        _rope_table_kernel,
        in_specs=[vec, vec, vec],
        out_specs=[tab, tab],
        out_shape=[jax.ShapeDtypeStruct((n, LANES, seq), F32)] * 2,
        name="rope_table",
    )(jnp.asarray(inv), jnp.asarray(use_row), jnp.asarray(sign))
    return [[(cos, i), (sin, i)] for i in range(n)]


def _table_spec(layout, tm):
    return pl.BlockSpec((None, LANES, tm), lambda b, t: (layout, 0, t))


def _ffn_body(x, mod_ref, g, wg_ref, wu_ref, wd_ref, k):
    shift = mod_ref[3 * k:3 * k + 1, :]
    scale = mod_ref[3 * k + 1:3 * k + 2, :]
    gate = mod_ref[3 * k + 2:3 * k + 3, :]
    hn = _adaln(x, g, shift, scale).astype(BF16)
    gg = _dot(hn, wg_ref[...])
    u = _dot(hn, wu_ref[...])
    a = (gg * jax.nn.sigmoid(gg) * u).astype(BF16)
    return x + (0.5 * gate) * _dot(a, wd_ref[...])


FFN_SUB_ROWS = 256
OUT_FFN_SUB_ROWS = 512


def _ffn_kernel(x_ref, mod_ref, g_ref, wg_ref, wu_ref, wd_ref, o_ref, *, k):
    sub = min(FFN_SUB_ROWS, x_ref.shape[0])
    for r0 in range(0, x_ref.shape[0], sub):
        rows = slice(r0, r0 + sub)
        o_ref[rows, :] = _ffn_body(x_ref[rows, :], mod_ref, g_ref[...], wg_ref, wu_ref, wd_ref, k)


def _ffn(x, mod, per_batch, g, wg, wu, wd, k):
    bsz, n, _ = x.shape
    tm = min(FFN_TILE, n)
    return pl.pallas_call(
        functools.partial(_ffn_kernel, k=k),
        grid=(bsz, n // tm),
        in_specs=[_tok_spec(tm), _mod_spec(per_batch), _const_spec((1, D_MODEL)),
                  _const_spec((D_MODEL, D_FF)), _const_spec((D_MODEL, D_FF)),
                  _const_spec((D_FF, D_MODEL))],
        out_specs=_tok_spec(tm),
        out_shape=jax.ShapeDtypeStruct(x.shape, F32),
        compiler_params=_params(2),
        name="ffn",
    )(x, mod, g, wg, wu, wd)


def _attn_out_rows(ot_ref, w_ref, r0, n_rows):
    _, nh, dv, tq = ot_ref.shape
    if tq >= n_rows:
        c0 = r0 % tq
        a = ot_ref[r0 // tq, :, :, c0:c0 + n_rows].reshape(nh * dv, n_rows)
        return _dot_tn(a, w_ref[...])
    parts = [_dot_tn(ot_ref[j].reshape(nh * dv, tq), w_ref[...])
             for j in range(r0 // tq, (r0 + n_rows) // tq)]
    return jnp.concatenate(parts, axis=0)


def _out_ffn_kernel(*refs, n_attn):
    h_ref, mod_ref, g_ref = refs[:3]
    attn = refs[3:3 + 2 * n_attn]
    wg_ref, wu_ref, wd_ref, o_ref = refs[3 + 2 * n_attn:]
    sub = min(OUT_FFN_SUB_ROWS, h_ref.shape[0])
    for r0 in range(0, h_ref.shape[0], sub):
        rows = slice(r0, r0 + sub)
        y = None
        for a in range(n_attn):
            ya = _attn_out_rows(attn[2 * a], attn[2 * a + 1], r0, sub)
            y = ya if y is None else y + ya
        x = h_ref[rows, :] + mod_ref[5:6, :] * y
        o_ref[rows, :] = _ffn_body(x, mod_ref, g_ref[...], wg_ref, wu_ref, wd_ref, 2)


def _out_ffn(h, mod, per_batch, g, attn, wg, wu, wd):
    bsz, n, _ = h.shape
    tm = min(FFN_TILE, n)
    in_specs = [_tok_spec(tm), _mod_spec(per_batch), _const_spec((1, D_MODEL))]
    args = [h, mod, g]
    for o_t, w in attn:
        _, _, nh, dv, tq = o_t.shape
        assert tm % tq == 0
        in_specs += [pl.BlockSpec((None, tm // tq, nh, dv, tq), lambda b, t: (b, t, 0, 0, 0)),
                     _const_spec(w.shape)]
        args += [o_t, w]
    in_specs += [_const_spec((D_MODEL, D_FF)), _const_spec((D_MODEL, D_FF)), _const_spec((D_FF, D_MODEL))]
    args += [wg, wu, wd]
    return pl.pallas_call(
        functools.partial(_out_ffn_kernel, n_attn=len(attn)),
        grid=(bsz, n // tm),
        in_specs=in_specs,
        out_specs=_tok_spec(tm),
        out_shape=jax.ShapeDtypeStruct(h.shape, F32),
        compiler_params=_params(2),
        name="out_ffn",
    )(*args)


L0_ST_SQ = 0
L0_ST_SV = L0_ST_SQ + SWA_HEADS * SWA_HEAD_DIM
L0_ST_KR = L0_ST_SV + SWA_KV_HEADS * SWA_HEAD_DIM
L0_ST_SK = L0_ST_KR + LANES
L0_ST_ROWS = L0_ST_SK + SWA_KV_HEADS * LANES


def _l0_proj_kernel(*refs, rope):
    (h_ref, mod_ref, g_ref, w0_ref, wst_ref, qag_ref, wqbt_ref, kvag_ref, wkvt_ref,
     mqg_ref, mkg_ref, sqg_ref, skg_ref) = refs[:13]
    if rope:
        cmt, smt, cst, sst = [r[...] for r in refs[13:17]]
        outs = refs[17:]
    else:
        outs = refs[13:]
    qm_ref, km_ref, vmt_ref, qs_ref, ks_ref, vst_ref = outs

    hn = _adaln(h_ref[...], g_ref[...], mod_ref[3:4, :], mod_ref[4:5, :]).astype(BF16)
    tm = hn.shape[0]
    p = _dot(hn, w0_ref[...])
    st = _grouped_nt(wst_ref, hn)

    mla_inv = 1.0 / (MLA_NOPE + MLA_ROPE)
    swa_inv = 1.0 / SWA_HEAD_DIM
    bcast = lambda v: jnp.broadcast_to(v, (v.shape[0], tm))
    mqg = bcast(mqg_ref[...] * (mla_inv ** 0.5 * LOG2E))
    sqg = bcast(sqg_ref[...] * (swa_inv ** 0.5 * LOG2E))
    mkg = bcast(mkg_ref[...])
    skg = bcast(skg_ref[...])

    def mla_rope(t):
        return _rope_t(t, cmt, smt, MLA_ROPE_ROWS) if rope else t

    def swa_rope(t):
        return _rope_t(t, cst, sst, SWA_ROPE_ROWS, SWA_ROPE_ROWS) if rope else t

    for h in range(SWA_HEADS):
        t = swa_rope(_rms_t(st(L0_ST_SQ + h * SWA_HEAD_DIM, L0_ST_SQ + (h + 1) * SWA_HEAD_DIM), sqg, swa_inv))
        for i in range(tm // SWA_BLK):
            qs_ref[h, i] = t[:, i * SWA_BLK:(i + 1) * SWA_BLK].astype(BF16)
    for j in range(SWA_KV_HEADS):
        t = swa_rope(_rms_t(st(L0_ST_SK + j * LANES, L0_ST_SK + (j + 1) * LANES), skg, swa_inv))
        ks_ref[j] = t.T.astype(BF16)
        vt = st(L0_ST_SV + j * SWA_HEAD_DIM, L0_ST_SV + (j + 1) * SWA_HEAD_DIM)
        for i in range(tm // SWA_BLK):
            vst_ref[j, i] = vt[:, i * SWA_BLK:(i + 1) * SWA_BLK].astype(BF16)

    qa = _rms(p[:, 0:MLA_Q_RANK], qag_ref[...], 1.0 / MLA_Q_RANK).astype(BF16)
    qt = _grouped_nt(wqbt_ref, qa)
    for h in range(MLA_HEADS):
        t = mla_rope(_rms_t(qt(h * LANES, (h + 1) * LANES), mqg, mla_inv))
        tq = qm_ref.shape[-1]
        for j in range(tm // tq):
            qm_ref[h, j] = t[:, j * tq:(j + 1) * tq].astype(BF16)

    kva = _rms(p[:, MLA_Q_RANK:MLA_Q_RANK + MLA_KV_RANK], kvag_ref[...], 1.0 / MLA_KV_RANK).astype(BF16)
    kvt = _grouped_nt(wkvt_ref, kva)
    kr = st(L0_ST_KR, L0_ST_KR + LANES)
    for h in range(MLA_HEADS):
        t = mla_rope(_rms_t(kvt(h * LANES, (h + 1) * LANES) + kr, mkg, mla_inv))
        km_ref[h] = t.T.astype(BF16)
    v0 = MLA_HEADS * LANES
    for h in range(MLA_HEADS):
        vmt_ref[h] = kvt(v0 + h * MLA_V, v0 + (h + 1) * MLA_V).astype(BF16)


def _l0_proj(h, mod, per_batch, g, w, tables):
    bsz, n, _ = h.shape
    tm = min(TOK_TILE, n)
    rope = tables is not None
    weights = [g, w["w0"], w["wst"], w["qa_g"], w["wqbt"], w["kva_g"], w["wkvt"],
               w["mq_g"], w["mk_g"], w["sq_g"], w["sk_g"]]
    in_specs = [_tok_spec(tm), _mod_spec(per_batch)] + [_const_spec(a.shape) for a in weights]
    args = [h, mod] + weights
    if rope:
        in_specs += [_table_spec(i, tm) for _, i in tables]
        args += [a for a, _ in tables]
    k_spec = lambda nh: pl.BlockSpec((None, nh, tm, LANES), lambda b, t: (b, 0, t, 0))
    qt_spec = lambda rows, tq: pl.BlockSpec((None, MLA_HEADS, tm // tq, rows, tq), lambda b, t: (b, 0, t, 0, 0))
    tq = min(MLA_TQ, n)
    out_specs = [qt_spec(LANES, tq), k_spec(MLA_HEADS),
                 pl.BlockSpec((None, MLA_HEADS, MLA_V, tm), lambda b, t: (b, 0, 0, t)),
                 qt_spec(SWA_HEAD_DIM, SWA_BLK), k_spec(SWA_KV_HEADS),
                 pl.BlockSpec((None, SWA_KV_HEADS, tm // SWA_BLK, SWA_HEAD_DIM, SWA_BLK),
                              lambda b, t: (b, 0, t, 0, 0))]
    out_shape = [jax.ShapeDtypeStruct((bsz, MLA_HEADS, n // tq, LANES, tq), BF16),
                 jax.ShapeDtypeStruct((bsz, MLA_HEADS, n, LANES), BF16),
                 jax.ShapeDtypeStruct((bsz, MLA_HEADS, MLA_V, n), BF16),
                 jax.ShapeDtypeStruct((bsz, SWA_HEADS, n // SWA_BLK, SWA_HEAD_DIM, SWA_BLK), BF16),
                 jax.ShapeDtypeStruct((bsz, SWA_KV_HEADS, n, LANES), BF16),
                 jax.ShapeDtypeStruct((bsz, SWA_KV_HEADS, n // SWA_BLK, SWA_HEAD_DIM, SWA_BLK), BF16)]
    return pl.pallas_call(
        functools.partial(_l0_proj_kernel, rope=rope),
        grid=(bsz, n // tm),
        in_specs=in_specs, out_specs=out_specs, out_shape=out_shape,
        compiler_params=_params(2),
        name="l0_proj",
    )(*args)


def _interleave(*stages):
    live = list(stages)
    while live:
        for st in list(live):
            try:
                next(st)
            except StopIteration:
                live.remove(st)


def _two_stage_loop(n_items, produce, consume, slot0, slot1):
    _interleave(produce(0, slot0))

    def pair(j, carry):
        i = 2 * j
        _interleave(produce(i + 1, slot1), consume(i, slot0))
        _interleave(produce(jnp.minimum(i + 2, n_items - 1), slot0), consume(i + 1, slot1))
        return carry

    lax.fori_loop(0, n_items // 2, pair, 0)


def _chunk_fold(x, op):
    return op(x.reshape(x.shape[0] // SUBLANES, SUBLANES, x.shape[-1]), axis=0)


def _score_chunks(s_ref, m_ref, key_chunk, qt, chunks, bias=None, floor=None):
    mx = None
    for k0, size in chunks:
        c = _dot(key_chunk(k0, size), qt)
        b = None if bias is None else bias(k0, size)
        if b is not None:
            c = c + b
        s_ref[k0:k0 + size, :] = c
        part = _chunk_fold(c, jnp.max)
        mx = part if mx is None else jnp.maximum(mx, part)
        yield
    m = jnp.max(mx, axis=0, keepdims=True)
    m_ref[...] = m if floor is None else jnp.maximum(m, floor)


def _exp_pv_chunks(s_ref, m_ref, value_t, chunk, out):
    m = m_ref[...]
    l8 = None
    o = None
    for k0 in range(0, s_ref.shape[0], chunk):
        e = jnp.exp2(s_ref[k0:k0 + chunk, :] - m)
        part = _chunk_fold(e, jnp.sum)
        l8 = part if l8 is None else l8 + part
        d = _dot(value_t(k0), e.astype(BF16))
        o = d if o is None else o + d
        yield
    out.append((o, jnp.sum(l8, axis=0, keepdims=True)))


KEY_CHUNK = 256


def _ctx_then_latent(c_ref, l_ref, n_ctx, token_major):
    def chunk(k0, size=KEY_CHUNK):
        ref, k = (c_ref, k0) if k0 < n_ctx else (l_ref, k0 - n_ctx)
        return ref[k:k + size, :] if token_major else ref[:, k:k + size]
    return chunk


SCORE_CHUNK = KEY_CHUNK


def _score_key_chunks(n_ctx, n_keys):
    assert n_ctx <= SCORE_CHUNK
    return [(0, n_ctx)] + [(k0, min(SCORE_CHUNK, n_keys - k0)) for k0 in range(n_ctx, n_keys, SCORE_CHUNK)]


def _mla_attn_kernel(q_ref, kc_ref, vct_ref, kl_ref, vlt_ref, o_ref, s0_ref, m0_ref, s1_ref, m1_ref):
    n_tiles, n_heads = o_ref.shape[:2]
    n_ctx = kc_ref.shape[1]
    n_keys = s0_ref.shape[0]
    score_chunks = _score_key_chunks(n_ctx, n_keys)

    def split(i):
        return lax.div(i, n_tiles), lax.rem(i, n_tiles)

    def scores(i, slot):
        s_ref, m_ref = slot
        h, t = split(i)
        keys = _ctx_then_latent(kc_ref.at[h], kl_ref.at[h], n_ctx, True)
        yield from _score_chunks(s_ref, m_ref, keys, q_ref[h, t], score_chunks)

    def softmax_pv(i, slot):
        s_ref, m_ref = slot
        h, t = split(i)
        values = _ctx_then_latent(vct_ref.at[h], vlt_ref.at[h], n_ctx, False)
        res = []
        yield from _exp_pv_chunks(s_ref, m_ref, values, KEY_CHUNK, res)
        o, l = res[0]
        o_ref[t, h] = (o * (1.0 / l)).astype(BF16)

    _two_stage_loop(n_tiles * n_heads, scores, softmax_pv, (s0_ref, m0_ref), (s1_ref, m1_ref))


def _mla_attn(q, kc, vct, kl, vlt):
    bsz, nh, n_tiles = q.shape[:3]
    hs = ATT_HEADS_PER_STEP
    n_keys = kc.shape[2] + kl.shape[2]
    grp = lambda shape: pl.BlockSpec((None, hs) + tuple(shape), lambda b, g: (b, g) + (0,) * len(shape))
    tq = q.shape[-1]
    slot = [pltpu.VMEM((n_keys, tq), F32), pltpu.VMEM((1, tq), F32)]
    return pl.pallas_call(
        _mla_attn_kernel,
        grid=(bsz, nh // hs),
        in_specs=[grp(q.shape[2:]), grp(kc.shape[2:]), grp(vct.shape[2:]), grp(kl.shape[2:]), grp(vlt.shape[2:])],
        out_specs=pl.BlockSpec((None, n_tiles, hs, MLA_V, tq), lambda b, g: (b, 0, g, 0, 0)),
        out_shape=jax.ShapeDtypeStruct((bsz, n_tiles, nh, MLA_V, tq), BF16),
        scratch_shapes=slot + slot,
        compiler_params=_params(2),
        name="mla_attn",
    )(q, kc, vct, kl, vlt)


SWA_BAND = 3 * SWA_BLK
SWA_NQ = SWA_GROUP * SWA_BLK


def _swa_query_block(heads):
    qt = jnp.concatenate(heads, axis=1)
    return jnp.concatenate([qt, jnp.zeros((LANES - qt.shape[0], qt.shape[1]), qt.dtype)], axis=0)


def _sink_row(sink_ref, j):
    return jnp.concatenate(
        [jnp.full((1, SWA_BLK), sink_ref[j * SWA_GROUP + g] * LOG2E, F32) for g in range(SWA_GROUP)], axis=1)


def _swa_attn_kernel(sink_ref, q_ref, kc_ref, vct_ref, kl_ref, vlt_ref, o_ref,
                     bias_ref, s0_ref, m0_ref, s1_ref, m1_ref):
    n_blk = o_ref.shape[0]
    n_ctx = kc_ref.shape[1]

    @pl.when(pl.program_id(0) == 0)
    def _():
        r = lax.broadcasted_iota(jnp.int32, (SWA_BAND, SWA_NQ), 0)
        c = lax.broadcasted_iota(jnp.int32, (SWA_BAND, SWA_NQ), 1) & (SWA_BLK - 1)
        for case in range(3):
            dist = r - c - case * SWA_BLK
            bias_ref[case] = jnp.where(jnp.abs(dist) <= SWA_WINDOW, 0.0, MASKED)

    def split(i):
        return lax.div(i, n_blk), lax.rem(i, n_blk)

    def band_start(n):
        return jnp.clip(n - 1, 0, n_blk - 3)

    def scores(i, slot):
        s_ref, m_ref = slot
        j, n = split(i)
        qt = _swa_query_block([q_ref[j * SWA_GROUP + g, n] for g in range(SWA_GROUP)])
        band0 = band_start(n) * SWA_BLK
        case = (n > 0).astype(jnp.int32) + (n == n_blk - 1).astype(jnp.int32)

        def keys(k0, size):
            if k0 < n_ctx:
                return kc_ref[j, k0:k0 + size, :]
            return kl_ref[j, pl.ds(pl.multiple_of(band0 + (k0 - n_ctx), SWA_BLK), size), :]

        def bias(k0, size):
            return None if k0 < n_ctx else bias_ref[case, k0 - n_ctx:k0 - n_ctx + size, :]

        chunks = [(0, n_ctx), (n_ctx, SWA_BAND)]
        yield from _score_chunks(s_ref, m_ref, keys, qt, chunks, bias, _sink_row(sink_ref, j))

    def softmax_pv(i, slot):
        s_ref, m_ref = slot
        j, n = split(i)
        b0 = band_start(n)

        def value_t(k0):
            if k0 < n_ctx:
                return vct_ref[j, :, k0:k0 + SWA_BLK]
            return vlt_ref[j, b0 + (k0 - n_ctx) // SWA_BLK]

        res = []
        yield from _exp_pv_chunks(s_ref, m_ref, value_t, SWA_BLK, res)
        o, l = res[0]
        o = o * (1.0 / (l + jnp.exp2(_sink_row(sink_ref, j) - m_ref[...])))
        for g in range(SWA_GROUP):
            o_ref[n, j * SWA_GROUP + g] = o[:, g * SWA_BLK:(g + 1) * SWA_BLK].astype(BF16)

    _two_stage_loop(SWA_KV_HEADS * n_blk, scores, softmax_pv, (s0_ref, m0_ref), (s1_ref, m1_ref))


def _swa_attn(sink, q, kc, vct, kl, vlt):
    bsz, nh, n_blk = q.shape[:3]
    n_keys = kc.shape[2] + SWA_BAND
    full = lambda a: pl.BlockSpec((None,) + a.shape[1:], lambda b: (b,) + (0,) * (a.ndim - 1))
    slot = [pltpu.VMEM((n_keys, SWA_NQ), F32), pltpu.VMEM((1, SWA_NQ), F32)]
    return pl.pallas_call(
        _swa_attn_kernel,
        grid=(bsz,),
        in_specs=[pl.BlockSpec(memory_space=pltpu.SMEM), full(q), full(kc), full(vct), full(kl), full(vlt)],
        out_specs=pl.BlockSpec((None, n_blk, nh, SWA_HEAD_DIM, SWA_BLK), lambda b: (b, 0, 0, 0, 0)),
        out_shape=jax.ShapeDtypeStruct((bsz, n_blk, nh, SWA_HEAD_DIM, SWA_BLK), BF16),
        scratch_shapes=[pltpu.VMEM((3, SWA_BAND, SWA_NQ), F32)] + slot + slot,
        compiler_params=_params(1),
        name="swa_attn",
    )(sink, q, kc, vct, kl, vlt)


def _ctx_attn_kernel(sink_ref, qm_ref, kmc_ref, vmct_ref, qs_ref, ksc_ref, vsct_ref, om_ref, os_ref):
    for h in range(om_ref.shape[0]):
        sc = _dot(kmc_ref[h], qm_ref[h])
        pc = jnp.exp2(sc - jnp.max(sc, axis=0, keepdims=True))
        l = jnp.sum(pc, axis=0, keepdims=True)
        o = _dot(vmct_ref[h], pc.astype(BF16))
        om_ref[h] = (o * (1.0 / l)).astype(BF16)
    for j in range(SWA_KV_HEADS):
        sink = _sink_row(sink_ref, j)
        for n in range(os_ref.shape[0]):
            qt = _swa_query_block([qs_ref[j * SWA_GROUP + g, n] for g in range(SWA_GROUP)])
            s = _dot(ksc_ref[j], qt)
            m = jnp.maximum(jnp.max(s, axis=0, keepdims=True), sink)
            p = jnp.exp2(s - m)
            l = jnp.sum(p, axis=0, keepdims=True) + jnp.exp2(sink - m)
            o = _dot(vsct_ref[j], p.astype(BF16)) * (1.0 / l)
            for g in range(SWA_GROUP):
                os_ref[n, j * SWA_GROUP + g] = o[:, g * SWA_BLK:(g + 1) * SWA_BLK].astype(BF16)


def _ctx_attn(sink, qm, kmc, vmct, qs, ksc, vsct):
    bsz, nh, n_tiles = qm.shape[:3]
    assert n_tiles == 1
    tq = qm.shape[-1]
    n_blk = qs.shape[2]
    per_b = lambda a: pl.BlockSpec((None,) + a.shape[1:], lambda b: (b,) + (0,) * (a.ndim - 1))
    return pl.pallas_call(
        _ctx_attn_kernel,
        grid=(bsz,),
        in_specs=[pl.BlockSpec(memory_space=pltpu.SMEM),
                  pl.BlockSpec((None, nh, None) + qm.shape[3:], lambda b: (b, 0, 0, 0, 0)),
                  per_b(kmc), per_b(vmct), per_b(qs), per_b(ksc), per_b(vsct)],
        out_specs=[pl.BlockSpec((None, None, nh, MLA_V, tq), lambda b: (b, 0, 0, 0, 0)),
                   pl.BlockSpec((None, n_blk, SWA_HEADS, SWA_HEAD_DIM, SWA_BLK), lambda b: (b, 0, 0, 0, 0))],
        out_shape=[jax.ShapeDtypeStruct((bsz, 1, nh, MLA_V, tq), BF16),
                   jax.ShapeDtypeStruct((bsz, n_blk, SWA_HEADS, SWA_HEAD_DIM, SWA_BLK), BF16)],
        compiler_params=_params(1),
        name="ctx_attn",
    )(sink, qm, kmc, vmct, qs, ksc, vsct)


DIFF_Q = HALF // 2


def _diff_norm_t(t, g):
    sq = t * t
    grp = [jnp.sum(sq[i * DIFF_Q:(i + 1) * DIFF_Q], axis=0, keepdims=True) for i in range(4)]
    inv = 1.0 / DIFF_HEAD_DIM
    r0 = lax.rsqrt((grp[0] + grp[2]) * inv + EPS)
    r1 = lax.rsqrt((grp[1] + grp[3]) * inv + EPS)
    parts = [t[i * DIFF_Q:(i + 1) * DIFF_Q] * (r0 if i % 2 == 0 else r1) for i in range(4)]
    return jnp.concatenate(parts, axis=0) * g


def _l1_proj_kernel(*refs, latent):
    if latent:
        h_ref, mod_ref, g_ref, wt_ref, qg_ref, kg_ref, cdt_ref, sdt_ref, q_ref, k_ref, vt_ref = refs
        cdt, sdt = cdt_ref[...], sdt_ref[...]
    else:
        h_ref, mod_ref, g_ref, wt_ref, kg_ref, k_ref, vt_ref = refs
    hn = _adaln(h_ref[...], g_ref[...], mod_ref[3:4, :], mod_ref[4:5, :]).astype(BF16)
    tm = hn.shape[0]
    width = DIFF_HEADS * LANES
    ft = _grouped_nt(wt_ref, hn)
    bcast = lambda v: jnp.broadcast_to(v, (LANES, tm))
    k0 = 0
    if latent:
        qg = bcast(qg_ref[...] * (DIFF_HEAD_DIM ** -0.5 * LOG2E))
        for h in range(DIFF_HEADS):
            t = _rope_t(_diff_norm_t(ft(h * LANES, (h + 1) * LANES), qg), cdt, sdt, DIFF_ROPE_ROWS)
            tq = q_ref.shape[-1]
            for j in range(tm // tq):
                q_ref[h, j] = t[:, j * tq:(j + 1) * tq].astype(BF16)
        k0 = width

    kg = bcast(kg_ref[...])
    for h in range(DIFF_HEADS):
        t = _diff_norm_t(ft(k0 + h * LANES, k0 + (h + 1) * LANES), kg)
        if latent:
            t = _rope_t(t, cdt, sdt, DIFF_ROPE_ROWS)
        k_ref[h] = t.T.astype(BF16)
    v0 = k0 + width
    for h in range(DIFF_HEADS):
        vt_ref[h] = ft(v0 + h * LANES, v0 + (h + 1) * LANES).astype(BF16)


def _l1_proj(h, mod, per_batch, g, w, tables):
    bsz, n, _ = h.shape
    tm = min(TOK_TILE, n)
    latent = tables is not None
    k_spec = pl.BlockSpec((None, DIFF_HEADS, tm, LANES), lambda b, t: (b, 0, t, 0))
    vt_spec = pl.BlockSpec((None, DIFF_HEADS, LANES, tm), lambda b, t: (b, 0, 0, t))
    k_shape = jax.ShapeDtypeStruct((bsz, DIFF_HEADS, n, LANES), BF16)
    vt_shape = jax.ShapeDtypeStruct((bsz, DIFF_HEADS, LANES, n), BF16)
    if latent:
        weights = [g, w["wqkvt"], w["q_g"], w["k_g"]]
        in_specs = ([_tok_spec(tm), _mod_spec(per_batch)] + [_const_spec(a.shape) for a in weights]
                    + [_table_spec(i, tm) for _, i in tables])
        args = [h, mod] + weights + [a for a, _ in tables]
        tq = min(DIFF_TQ, n)
        out_specs = [pl.BlockSpec((None, DIFF_HEADS, tm // tq, LANES, tq), lambda b, t: (b, 0, t, 0, 0)),
                     k_spec, vt_spec]
        out_shape = [jax.ShapeDtypeStruct((bsz, DIFF_HEADS, n // tq, LANES, tq), BF16), k_shape, vt_shape]
    else:
        weights = [g, w["wkvt"], w["k_g"]]
        in_specs = [_tok_spec(tm), _mod_spec(per_batch)] + [_const_spec(a.shape) for a in weights]
        args = [h, mod] + weights
        out_specs = [k_spec, vt_spec]
        out_shape = [k_shape, vt_shape]
    return pl.pallas_call(
        functools.partial(_l1_proj_kernel, latent=latent),
        grid=(bsz, n // tm),
        in_specs=in_specs, out_specs=out_specs, out_shape=out_shape,
        compiler_params=_params(2),
        name="l1_proj",
    )(*args)


def _diff_attn_kernel(lam_ref, subg_ref, q_ref, kc_ref, vct_ref, kl_ref, vlt_ref, o_ref,
                      s0_ref, m0_ref, s1_ref, m1_ref):
    n_tiles, n_heads = o_ref.shape[:2]
    n_ctx = kc_ref.shape[1]
    n_keys = s0_ref.shape[1]
    score_chunks = _score_key_chunks(n_ctx, n_keys)
    feature = lax.broadcasted_iota(jnp.int32, q_ref.shape[2:], 0)
    sub0 = (feature & (HALF - 1)) < DIFF_Q
    lv = lam_ref[...]
    lam = (jnp.exp(jnp.sum(lv[0:1] * lv[1:2], axis=-1, keepdims=True))
           - jnp.exp(jnp.sum(lv[2:3] * lv[3:4], axis=-1, keepdims=True)) + LAMBDA_INIT_L1)
    subg = subg_ref[...] * (1.0 - LAMBDA_INIT_L1)

    def split(i):
        return lax.div(i, n_tiles), lax.rem(i, n_tiles)

    def scores(i, slot):
        s_ref, m_ref = slot
        h, t = split(i)
        qt = q_ref[h, t]
        zero = jnp.zeros_like(qt)
        keys = _ctx_then_latent(kc_ref.at[h], kl_ref.at[h], n_ctx, True)
        for s, qs in enumerate((jnp.where(sub0, qt, zero), jnp.where(sub0, zero, qt))):
            yield from _score_chunks(s_ref.at[s], m_ref.at[s], keys, qs, score_chunks)

    def softmax_pv(i, slot):
        s_ref, m_ref = slot
        h, t = split(i)
        values = _ctx_then_latent(vct_ref.at[h], vlt_ref.at[h], n_ctx, False)
        res = []
        for s in range(2):
            yield from _exp_pv_chunks(s_ref.at[s], m_ref.at[s], values, KEY_CHUNK, res)
        (o0, l0), (o1, l1) = res
        o = o0 * (1.0 / l0) - o1 * (lam * (1.0 / l1))
        ms = jnp.mean(o * o, axis=0, keepdims=True)
        o_ref[t, h] = (o * lax.rsqrt(ms + EPS) * subg).astype(BF16)

    _two_stage_loop(n_tiles * n_heads, scores, softmax_pv, (s0_ref, m0_ref), (s1_ref, m1_ref))


def _diff_attn(lamv, subg, q, kc, vct, kl, vlt):
    bsz, nh, n_tiles = q.shape[:3]
    hs = ATT_HEADS_PER_STEP
    n_keys = kc.shape[2] + kl.shape[2]
    grp = lambda shape: pl.BlockSpec((None, hs) + tuple(shape), lambda b, g: (b, g) + (0,) * len(shape))
    tq = q.shape[-1]
    slot = [pltpu.VMEM((2, n_keys, tq), F32), pltpu.VMEM((2, 1, tq), F32)]
    return pl.pallas_call(
        _diff_attn_kernel,
        grid=(bsz, nh // hs),
        in_specs=[_const_spec(lamv.shape), _const_spec(subg.shape),
                  grp(q.shape[2:]), grp(kc.shape[2:]), grp(vct.shape[2:]), grp(kl.shape[2:]), grp(vlt.shape[2:])],
        out_specs=pl.BlockSpec((None, n_tiles, hs, LANES, tq), lambda b, g: (b, 0, g, 0, 0)),
        out_shape=jax.ShapeDtypeStruct((bsz, n_tiles, nh, LANES, tq), BF16),
        scratch_shapes=slot + slot,
        compiler_params=_params(2),
        name="diff_attn",
    )(lamv, subg, q, kc, vct, kl, vlt)


def _prep_l0(w_in, qa_g, wqb, kva_g, wkvb, q_g, k_g, sq_g, sk_g, w_out):
    mla = _mla_lane_map()
    swa = _swa_lane_map()
    kr_cols = np.where(mla >= MLA_NOPE, 384 + mla - MLA_NOPE, -1)
    wst_cols = np.concatenate([
        416 + np.arange(SWA_HEADS * SWA_HEAD_DIM),
        np.arange(1056, 1184),
        kr_cols,
        _per_head(swa, SWA_KV_HEADS, SWA_HEAD_DIM, base=928),
    ])
    assert wst_cols.shape[0] == L0_ST_ROWS
    nope_map = np.where((mla >= 0) & (mla < MLA_NOPE), mla, -1)
    v_cols = np.concatenate([h * (MLA_NOPE + MLA_V) + MLA_NOPE + np.arange(MLA_V) for h in range(MLA_HEADS)])
    wkvt_cols = np.concatenate([_per_head(nope_map, MLA_HEADS, MLA_NOPE + MLA_V), v_cols])
    row = lambda v: v.reshape(1, -1).astype(F32)
    col = lambda v: v.reshape(-1, 1).astype(F32)
    return {
        "w0": w_in[:, 0:MLA_Q_RANK + MLA_KV_RANK].astype(BF16),
        "wst": _cols_as_rows(w_in, wst_cols),
        "qa_g": row(qa_g),
        "wqbt": _cols_as_rows(wqb, _per_head(mla, MLA_HEADS, MLA_NOPE + MLA_ROPE)),
        "kva_g": row(kva_g),
        "wkvt": _cols_as_rows(wkvb, wkvt_cols),
        "mq_g": col(_take_cols(q_g, mla)),
        "mk_g": col(_take_cols(k_g, mla)),
        "sq_g": col(sq_g),
        "sk_g": col(_take_cols(sk_g, swa)),
        "wa": w_out[:MLA_HEADS * MLA_V].astype(BF16),
        "wb": w_out[MLA_HEADS * MLA_V:].astype(BF16),
    }


def _prep_l1(w_in, q_g, k_g, w_out):
    dm = _diff_lane_map()
    width = DIFF_HEADS * LANES
    cols = np.concatenate([_per_head(dm, DIFF_HEADS, LANES), _per_head(dm, DIFF_HEADS, LANES, base=width),
                           np.arange(2 * width, 3 * width)])
    wqkvt = _cols_as_rows(w_in, cols)
    return {
        "wqkvt": wqkvt,
        "wkvt": wqkvt[width:],
        "q_g": _take_cols(q_g, dm % DIFF_HEAD_DIM).reshape(-1, 1).astype(F32),
        "k_g": _take_cols(k_g, dm % DIFF_HEAD_DIM).reshape(-1, 1).astype(F32),
        "w_out": w_out.astype(BF16),
    }


def kernel(x, c, ctx, c_ctx, l0_ada_w, l0_ada_b, l0_norm_g, l0_ffn_wg, l0_ffn_wu, l0_ffn_wd, l0_w_in, l0_mla_qa_g, l0_mla_wqb, l0_mla_kva_g, l0_mla_wkvb, l0_mla_q_g, l0_mla_k_g, l0_swa_q_g, l0_swa_k_g, l0_swa_sink, l0_w_out, l1_ada_w, l1_ada_b, l1_norm_g, l1_ffn_wg, l1_ffn_wu, l1_ffn_wd, l1_w_in, l1_q_g, l1_k_g, l1_lambda_q1, l1_lambda_k1, l1_lambda_q2, l1_lambda_k2, l1_subln_g, l1_w_out):
    bsz, seq, _ = x.shape

    pad = (-(bsz + 1)) % SUBLANES
    cc = jnp.concatenate([c, c_ctx[None, :], jnp.zeros((pad, D_MODEL), F32)], axis=0)

    def mods(ada_w, ada_b):
        m = _ada(cc, ada_w, ada_b).reshape(cc.shape[0], N_MOD, D_MODEL)
        return m[:bsz], m[bsz:bsz + 1]

    def ffn_weights(wg, wu, wd, i):
        return wg[i].astype(BF16), wu[i].astype(BF16), wd[i].astype(BF16)

    mla_tab, swa_tab, diff_tab = _rope_tables(seq, [
        (MLA_ROPE, [(0, 16)], [(64, 16)]),
        (SWA_HEAD_DIM, [(0, 32)], [(32, 32)]),
        (DIFF_HEAD_DIM, [(0, 32), (32, 32)], [(64, 32), (96, 32)]),
    ])

    mod, mod_c = mods(l0_ada_w, l0_ada_b)
    g = [l0_norm_g[i:i + 1] for i in range(3)]
    w = _prep_l0(l0_w_in, l0_mla_qa_g, l0_mla_wqb, l0_mla_kva_g, l0_mla_wkvb, l0_mla_q_g, l0_mla_k_g,
                 l0_swa_q_g, l0_swa_k_g, l0_w_out)
    f0 = ffn_weights(l0_ffn_wg, l0_ffn_wu, l0_ffn_wd, 0)
    f1 = ffn_weights(l0_ffn_wg, l0_ffn_wu, l0_ffn_wd, 1)

    h = _ffn(x, mod, True, g[0], *f0, k=0)
    hc = _ffn(ctx, mod_c, False, g[0], *f0, k=0)

    qm, km, vmt, qs, ks, vst = _l0_proj(h, mod, True, g[1], w, mla_tab + swa_tab)
    qm_c, km_c, vmt_c, qs_c, ks_c, vst_c = _l0_proj(hc, mod_c, False, g[1], w, None)
    vsc_t = vst_c.transpose(0, 1, 3, 2, 4).reshape(bsz, SWA_KV_HEADS, SWA_HEAD_DIM, -1)

    at = _mla_attn(qm, km_c, vmt_c, km, vmt)
    bt = _swa_attn(l0_swa_sink, qs, ks_c, vsc_t, ks, vst)
    at_c, bt_c = _ctx_attn(l0_swa_sink, qm_c, km_c, vmt_c, qs_c, ks_c, vsc_t)

    h = _out_ffn(h, mod, True, g[2], [(at, w["wa"]), (bt, w["wb"])], *f1)
    hc = _out_ffn(hc, mod_c, False, g[2], [(at_c, w["wa"]), (bt_c, w["wb"])], *f1)

    mod, mod_c = mods(l1_ada_w, l1_ada_b)
    g = [l1_norm_g[i:i + 1] for i in range(3)]
    w = _prep_l1(l1_w_in, l1_q_g, l1_k_g, l1_w_out)
    f0 = ffn_weights(l1_ffn_wg, l1_ffn_wu, l1_ffn_wd, 0)
    f1 = ffn_weights(l1_ffn_wg, l1_ffn_wu, l1_ffn_wd, 1)

    h = _ffn(h, mod, True, g[0], *f0, k=0)
    hc = _ffn(hc, mod_c, False, g[0], *f0, k=0)

    qd, kd, vdt = _l1_proj(h, mod, True, g[1], w, diff_tab)
    kd_c, vdt_c = _l1_proj(hc, mod_c, False, g[1], w, None)
    lamv = jnp.stack([l1_lambda_q1, l1_lambda_k1, l1_lambda_q2, l1_lambda_k2]).astype(F32)
    ot = _diff_attn(lamv, l1_subln_g.reshape(-1, 1).astype(F32), qd, kd_c, vdt_c, kd, vdt)
    return _out_ffn(h, mod, True, g[2], [(ot, w["w_out"])], *f1)
```

```python
import functools
import math

import numpy as np
import jax
import jax.numpy as jnp
from jax import lax
from jax.experimental import pallas as pl
from jax.experimental.pallas import tpu as pltpu

F32 = jnp.float32
BF16 = jnp.bfloat16

D_MODEL = 1024
D_FF = 2816
N_MOD = 9
EPS = 1e-6
ROPE_BASE = 10000.0
GRID_W = 64
LOG2E = 1.4426950408889634

MLA_HEADS = 8
MLA_Q_RANK = 256
MLA_KV_RANK = 128
MLA_NOPE = 64
MLA_ROPE = 32
MLA_V = 64
SWA_HEADS = 8
SWA_KV_HEADS = 2
SWA_GROUP = SWA_HEADS // SWA_KV_HEADS
SWA_HEAD_DIM = 64
SWA_WINDOW = 128
DIFF_HEADS = 8
DIFF_HEAD_DIM = 64
LAMBDA_INIT_L1 = 0.8 - 0.6 * math.exp(-0.3 * 1)

LANES = 128
SUBLANES = 8
HALF = LANES // 2
MLA_TQ = 512
DIFF_TQ = 256
SWA_BLK = 128
TOK_TILE = 1024
FFN_TILE = 1024
ATT_HEADS_PER_STEP = 8
VMEM_LIMIT = 56 * 1024 * 1024
MASKED = -1e30

NT_DIMS = (((1,), (1,)), ((), ()))
TN_DIMS = (((0,), (0,)), ((), ()))


def _mla_lane_map():
    m = np.full(LANES, -1, np.int64)
    m[0:16] = 64 + np.arange(16)
    m[16:48] = np.arange(32)
    m[64:80] = 80 + np.arange(16)
    m[80:112] = 32 + np.arange(32)
    return m


def _swa_lane_map():
    m = np.full(LANES, -1, np.int64)
    m[0:SWA_HEAD_DIM] = np.arange(SWA_HEAD_DIM)
    return m


def _diff_lane_map():
    m = np.zeros(LANES, np.int64)
    m[0:32] = np.arange(32)
    m[32:64] = 64 + np.arange(32)
    m[64:96] = 32 + np.arange(32)
    m[96:128] = 96 + np.arange(32)
    return m


MLA_ROPE_ROWS = MLA_ROPE // 2
SWA_ROPE_ROWS = SWA_HEAD_DIM // 2
DIFF_ROPE_ROWS = HALF


def _take_cols(w, idx):
    idx = np.asarray(idx)
    out = jnp.take(w, jnp.asarray(np.where(idx < 0, 0, idx), jnp.int32), axis=-1)
    return jnp.where(jnp.asarray(idx >= 0), out, jnp.zeros((), w.dtype))


def _cols_as_rows(w, idx):
    idx = np.asarray(idx)
    wt = jnp.concatenate([w.astype(BF16).T, jnp.zeros((1, w.shape[0]), BF16)], axis=0)
    return jnp.take(wt, jnp.asarray(np.where(idx < 0, w.shape[1], idx), jnp.int32), axis=0)


def _per_head(lane_map, n_heads, stride, base=0):
    cols = [np.where(lane_map >= 0, base + h * stride + lane_map, -1) for h in range(n_heads)]
    return np.concatenate(cols)


def _adaln(x, g, shift, scale):
    ms = jnp.mean(x * x, axis=-1, keepdims=True)
    return x * lax.rsqrt(ms + EPS) * g * (1.0 + scale) + shift


def _rms(t, g, inv_dim):
    ms = jnp.sum(t * t, axis=-1, keepdims=True) * inv_dim
    return t * lax.rsqrt(ms + EPS) * g


def _rms_t(t, g, inv_dim):
    ms = jnp.sum(t * t, axis=0, keepdims=True) * inv_dim
    return t * lax.rsqrt(ms + EPS) * g


def _rope_t(t, cos, sin, rows, partner=HALF):
    a, b = t[0:rows], t[partner:partner + rows]
    a2 = a * cos[0:rows] + b * sin[0:rows]
    b2 = b * cos[partner:partner + rows] + a * sin[partner:partner + rows]
    parts = [a2, t[rows:partner], b2, t[partner + rows:]]
    return jnp.concatenate([p for p in parts if p.shape[0] > 0], axis=0)


def _dot(a, b):
    return jnp.dot(a, b, preferred_element_type=F32)


def _dot_nt(a, b):
    return lax.dot_general(a, b, NT_DIMS, preferred_element_type=F32)


def _dot_tn(a, b):
    return lax.dot_general(a, b, TN_DIMS, preferred_element_type=F32)


NT_GROUP_ROWS = 512


def _grouped_nt(w_ref, x):
    done = {}

    def rows(r0, r1):
        g = r0 // NT_GROUP_ROWS
        assert (r1 - 1) // NT_GROUP_ROWS == g
        if g not in done:
            done[g] = _dot_nt(w_ref[g * NT_GROUP_ROWS:(g + 1) * NT_GROUP_ROWS, :], x)
        return done[g][r0 - g * NT_GROUP_ROWS:r1 - g * NT_GROUP_ROWS]

    return rows


def _params(n_axes):
    return pltpu.CompilerParams(dimension_semantics=("arbitrary",) * n_axes,
                                vmem_limit_bytes=VMEM_LIMIT)


def _const_spec(shape):
    nd = len(shape)
    return pl.BlockSpec(shape, lambda *_: (0,) * nd, pipeline_mode=pl.Buffered(1))


def _mod_spec(per_batch):
    if per_batch:
        return pl.BlockSpec((None, N_MOD, D_MODEL), lambda b, t: (b, 0, 0))
    return pl.BlockSpec((None, N_MOD, D_MODEL), lambda b, t: (0, 0, 0))


def _tok_spec(tm):
    return pl.BlockSpec((None, tm, D_MODEL), lambda b, t: (b, t, 0))


def _ada_kernel(c_ref, w_ref, b_ref, o_ref):
    c = c_ref[...]
    a = (c * jax.nn.sigmoid(c)).astype(BF16)
    o_ref[...] = _dot(a, w_ref[...].astype(BF16)) + b_ref[...]


def _ada(cc, w, b):
    rows = cc.shape[0]
    n = w.shape[1]
    tn = 1152
    return pl.pallas_call(
        _ada_kernel,
        grid=(n // tn,),
        in_specs=[pl.BlockSpec((rows, D_MODEL), lambda j: (0, 0)),
                  pl.BlockSpec((D_MODEL, tn), lambda j: (0, j)),
                  pl.BlockSpec((1, tn), lambda j: (0, j))],
        out_specs=pl.BlockSpec((rows, tn), lambda j: (0, j)),
        out_shape=jax.ShapeDtypeStruct((rows, n), F32),
        compiler_params=_params(1),
        name="ada_mod",
    )(cc, w, b.reshape(1, n))


def _rope_table_kernel(inv_ref, userow_ref, sign_ref, cos_ref, sin_ref):
    n_layouts, _, seq = cos_ref.shape
    t = lax.broadcasted_iota(jnp.int32, (LANES, seq), 1)
    row = lax.shift_right_logical(t, GRID_W.bit_length() - 1).astype(F32)
    col = (t & (GRID_W - 1)).astype(F32)
    for i in range(n_layouts):
        pos = jnp.where(userow_ref[i] > 0.5, row, col)
        ang = pos * inv_ref[i]
        sign = sign_ref[i]
        active = sign != 0.0
        cos_ref[i] = jnp.where(active, jnp.cos(ang), 1.0)
        sin_ref[i] = jnp.where(active, sign * jnp.sin(ang), 0.0)


def _rope_tables(seq, layouts):
    n = len(layouts)
    inv = np.zeros((n, LANES, 1), np.float32)
    use_row = np.zeros((n, LANES, 1), np.float32)
    sign = np.zeros((n, LANES, 1), np.float32)
    for i, (rot_dim, x1_slots, x2_slots) in enumerate(layouts):
        n_f = rot_dim // 4
        inv_f = (ROPE_BASE ** (-np.arange(n_f, dtype=np.float64) / n_f)).astype(np.float32)
        inv_half = np.concatenate([inv_f, inv_f])
        use_row_half = np.concatenate([np.ones(n_f), np.zeros(n_f)])
        for slots, sgn in ((x1_slots, -1.0), (x2_slots, 1.0)):
            for start, width in slots:
                assert width == rot_dim // 2
                inv[i, start:start + width, 0] = inv_half
                use_row[i, start:start + width, 0] = use_row_half
                sign[i, start:start + width, 0] = sgn
    vec = pl.BlockSpec((n, LANES, 1), lambda: (0, 0, 0))
    tab = pl.BlockSpec((n, LANES, seq), lambda: (0, 0, 0))
    cos, sin = pl.pallas_call(
        _rope_table_kernel,
        in_specs=[vec, vec, vec],
        out_specs=[tab, tab],
        out_shape=[jax.ShapeDtypeStruct((n, LANES, seq), F32)] * 2,
        name="rope_table",
    )(jnp.asarray(inv), jnp.asarray(use_row), jnp.asarray(sign))
    return [[(cos, i), (sin, i)] for i in range(n)]


def _table_spec(layout, tm):
    return pl.BlockSpec((None, LANES, tm), lambda b, t: (layout, 0, t))


def _ffn_body(x, mod_ref, g, wg_ref, wu_ref, wd_ref, k):
    shift = mod_ref[3 * k:3 * k + 1, :]
    scale = mod_ref[3 * k + 1:3 * k + 2, :]
    gate = mod_ref[3 * k + 2:3 * k + 3, :]
    hn = _adaln(x, g, shift, scale).astype(BF16)
    gg = _dot(hn, wg_ref[...])
    u = _dot(hn, wu_ref[...])
    a = (gg * jax.nn.sigmoid(gg) * u).astype(BF16)
    return x + (0.5 * gate) * _dot(a, wd_ref[...])


FFN_SUB_ROWS = 256
OUT_FFN_SUB_ROWS = 512


def _ffn_kernel(x_ref, mod_ref, g_ref, wg_ref, wu_ref, wd_ref, o_ref, *, k):
    sub = min(FFN_SUB_ROWS, x_ref.shape[0])
    for r0 in range(0, x_ref.shape[0], sub):
        rows = slice(r0, r0 + sub)
        o_ref[rows, :] = _ffn_body(x_ref[rows, :], mod_ref, g_ref[...], wg_ref, wu_ref, wd_ref, k)


def _ffn(x, mod, per_batch, g, wg, wu, wd, k):
    bsz, n, _ = x.shape
    tm = min(FFN_TILE, n)
    return pl.pallas_call(
        functools.partial(_ffn_kernel, k=k),
        grid=(bsz, n // tm),
        in_specs=[_tok_spec(tm), _mod_spec(per_batch), _const_spec((1, D_MODEL)),
                  _const_spec((D_MODEL, D_FF)), _const_spec((D_MODEL, D_FF)),
                  _const_spec((D_FF, D_MODEL))],
        out_specs=_tok_spec(tm),
        out_shape=jax.ShapeDtypeStruct(x.shape, F32),
        compiler_params=_params(2),
        name="ffn",
    )(x, mod, g, wg, wu, wd)


def _attn_out_rows(ot_ref, w_ref, r0, n_rows):
    _, nh, dv, tq = ot_ref.shape
    if tq >= n_rows:
        c0 = r0 % tq
        a = ot_ref[r0 // tq, :, :, c0:c0 + n_rows].reshape(nh * dv, n_rows)
        return _dot_tn(a, w_ref[...])
    parts = [_dot_tn(ot_ref[j].reshape(nh * dv, tq), w_ref[...])
             for j in range(r0 // tq, (r0 + n_rows) // tq)]
    return jnp.concatenate(parts, axis=0)


def _out_ffn_kernel(*refs, n_attn):
    h_ref, mod_ref, g_ref = refs[:3]
    attn = refs[3:3 + 2 * n_attn]
    wg_ref, wu_ref, wd_ref, o_ref = refs[3 + 2 * n_attn:]
    sub = min(OUT_FFN_SUB_ROWS, h_ref.shape[0])
    for r0 in range(0, h_ref.shape[0], sub):
        rows = slice(r0, r0 + sub)
        y = None
        for a in range(n_attn):
            ya = _attn_out_rows(attn[2 * a], attn[2 * a + 1], r0, sub)
            y = ya if y is None else y + ya
        x = h_ref[rows, :] + mod_ref[5:6, :] * y
        o_ref[rows, :] = _ffn_body(x, mod_ref, g_ref[...], wg_ref, wu_ref, wd_ref, 2)


def _out_ffn(h, mod, per_batch, g, attn, wg, wu, wd):
    bsz, n, _ = h.shape
    tm = min(FFN_TILE, n)
    in_specs = [_tok_spec(tm), _mod_spec(per_batch), _const_spec((1, D_MODEL))]
    args = [h, mod, g]
    for o_t, w in attn:
        _, _, nh, dv, tq = o_t.shape
        assert tm % tq == 0
        in_specs += [pl.BlockSpec((None, tm // tq, nh, dv, tq), lambda b, t: (b, t, 0, 0, 0)),
                     _const_spec(w.shape)]
        args += [o_t, w]
    in_specs += [_const_spec((D_MODEL, D_FF)), _const_spec((D_MODEL, D_FF)), _const_spec((D_FF, D_MODEL))]
    args += [wg, wu, wd]
    return pl.pallas_call(
        functools.partial(_out_ffn_kernel, n_attn=len(attn)),
        grid=(bsz, n // tm),
        in_specs=in_specs,
        out_specs=_tok_spec(tm),
        out_shape=jax.ShapeDtypeStruct(h.shape, F32),
        compiler_params=_params(2),
        name="out_ffn",
    )(*args)


L0_ST_SQ = 0
L0_ST_SV = L0_ST_SQ + SWA_HEADS * SWA_HEAD_DIM
L0_ST_KR = L0_ST_SV + SWA_KV_HEADS * SWA_HEAD_DIM
L0_ST_SK = L0_ST_KR + LANES
L0_ST_ROWS = L0_ST_SK + SWA_KV_HEADS * LANES


def _l0_proj_kernel(*refs, rope):
    (h_ref, mod_ref, g_ref, w0_ref, wst_ref, qag_ref, wqbt_ref, kvag_ref, wkvt_ref,
     mqg_ref, mkg_ref, sqg_ref, skg_ref) = refs[:13]
    if rope:
        cmt, smt, cst, sst = [r[...] for r in refs[13:17]]
        outs = refs[17:]
    else:
        outs = refs[13:]
    qm_ref, km_ref, vmt_ref, qs_ref, ks_ref, vst_ref = outs

    hn = _adaln(h_ref[...], g_ref[...], mod_ref[3:4, :], mod_ref[4:5, :]).astype(BF16)
    tm = hn.shape[0]
    p = _dot(hn, w0_ref[...])
    st = _grouped_nt(wst_ref, hn)

    mla_inv = 1.0 / (MLA_NOPE + MLA_ROPE)
    swa_inv = 1.0 / SWA_HEAD_DIM
    bcast = lambda v: jnp.broadcast_to(v, (v.shape[0], tm))
    mqg = bcast(mqg_ref[...] * (mla_inv ** 0.5 * LOG2E))
    sqg = bcast(sqg_ref[...] * (swa_inv ** 0.5 * LOG2E))
    mkg = bcast(mkg_ref[...])
    skg = bcast(skg_ref[...])

    def mla_rope(t):
        return _rope_t(t, cmt, smt, MLA_ROPE_ROWS) if rope else t

    def swa_rope(t):
        return _rope_t(t, cst, sst, SWA_ROPE_ROWS, SWA_ROPE_ROWS) if rope else t

    for h in range(SWA_HEADS):
        t = swa_rope(_rms_t(st(L0_ST_SQ + h * SWA_HEAD_DIM, L0_ST_SQ + (h + 1) * SWA_HEAD_DIM), sqg, swa_inv))
        for i in range(tm // SWA_BLK):
            qs_ref[h, i] = t[:, i * SWA_BLK:(i + 1) * SWA_BLK].astype(BF16)
    for j in range(SWA_KV_HEADS):
        t = swa_rope(_rms_t(st(L0_ST_SK + j * LANES, L0_ST_SK + (j + 1) * LANES), skg, swa_inv))
        ks_ref[j] = t.T.astype(BF16)
        vt = st(L0_ST_SV + j * SWA_HEAD_DIM, L0_ST_SV + (j + 1) * SWA_HEAD_DIM)
        for i in range(tm // SWA_BLK):
            vst_ref[j, i] = vt[:, i * SWA_BLK:(i + 1) * SWA_BLK].astype(BF16)

    qa = _rms(p[:, 0:MLA_Q_RANK], qag_ref[...], 1.0 / MLA_Q_RANK).astype(BF16)
    qt = _grouped_nt(wqbt_ref, qa)
    for h in range(MLA_HEADS):
        t = mla_rope(_rms_t(qt(h * LANES, (h + 1) * LANES), mqg, mla_inv))
        tq = qm_ref.shape[-1]
        for j in range(tm // tq):
            qm_ref[h, j] = t[:, j * tq:(j + 1) * tq].astype(BF16)

    kva = _rms(p[:, MLA_Q_RANK:MLA_Q_RANK + MLA_KV_RANK], kvag_ref[...], 1.0 / MLA_KV_RANK).astype(BF16)
    kvt = _grouped_nt(wkvt_ref, kva)
    kr = st(L0_ST_KR, L0_ST_KR + LANES)
    for h in range(MLA_HEADS):
        t = mla_rope(_rms_t(kvt(h * LANES, (h + 1) * LANES) + kr, mkg, mla_inv))
        km_ref[h] = t.T.astype(BF16)
    v0 = MLA_HEADS * LANES
    for h in range(MLA_HEADS):
        vmt_ref[h] = kvt(v0 + h * MLA_V, v0 + (h + 1) * MLA_V).astype(BF16)


def _l0_proj(h, mod, per_batch, g, w, tables):
    bsz, n, _ = h.shape
    tm = min(TOK_TILE, n)
    rope = tables is not None
    weights = [g, w["w0"], w["wst"], w["qa_g"], w["wqbt"], w["kva_g"], w["wkvt"],
               w["mq_g"], w["mk_g"], w["sq_g"], w["sk_g"]]
    in_specs = [_tok_spec(tm), _mod_spec(per_batch)] + [_const_spec(a.shape) for a in weights]
    args = [h, mod] + weights
    if rope:
        in_specs += [_table_spec(i, tm) for _, i in tables]
        args += [a for a, _ in tables]
    k_spec = lambda nh: pl.BlockSpec((None, nh, tm, LANES), lambda b, t: (b, 0, t, 0))
    qt_spec = lambda rows, tq: pl.BlockSpec((None, MLA_HEADS, tm // tq, rows, tq), lambda b, t: (b, 0, t, 0, 0))
    tq = min(MLA_TQ, n)
    out_specs = [qt_spec(LANES, tq), k_spec(MLA_HEADS),
                 pl.BlockSpec((None, MLA_HEADS, MLA_V, tm), lambda b, t: (b, 0, 0, t)),
                 qt_spec(SWA_HEAD_DIM, SWA_BLK), k_spec(SWA_KV_HEADS),
                 pl.BlockSpec((None, SWA_KV_HEADS, tm // SWA_BLK, SWA_HEAD_DIM, SWA_BLK),
                              lambda b, t: (b, 0, t, 0, 0))]
    out_shape = [jax.ShapeDtypeStruct((bsz, MLA_HEADS, n // tq, LANES, tq), BF16),
                 jax.ShapeDtypeStruct((bsz, MLA_HEADS, n, LANES), BF16),
                 jax.ShapeDtypeStruct((bsz, MLA_HEADS, MLA_V, n), BF16),
                 jax.ShapeDtypeStruct((bsz, SWA_HEADS, n // SWA_BLK, SWA_HEAD_DIM, SWA_BLK), BF16),
                 jax.ShapeDtypeStruct((bsz, SWA_KV_HEADS, n, LANES), BF16),
                 jax.ShapeDtypeStruct((bsz, SWA_KV_HEADS, n // SWA_BLK, SWA_HEAD_DIM, SWA_BLK), BF16)]
    return pl.pallas_call(
        functools.partial(_l0_proj_kernel, rope=rope),
        grid=(bsz, n // tm),
        in_specs=in_specs, out_specs=out_specs, out_shape=out_shape,
        compiler_params=_params(2),
        name="l0_proj",
    )(*args)


def _interleave(*stages):
    live = list(stages)
    while live:
        for st in list(live):
            try:
                next(st)
            except StopIteration:
                live.remove(st)


def _two_stage_loop(n_items, produce, consume, slot0, slot1):
    _interleave(produce(0, slot0))

    def pair(j, carry):
        i = 2 * j
        _interleave(consume(i, slot0), produce(i + 1, slot1))
        _interleave(consume(i + 1, slot1), produce(jnp.minimum(i + 2, n_items - 1), slot0))
        return carry

    lax.fori_loop(0, n_items // 2, pair, 0)


def _chunk_fold(x, op):
    return op(x.reshape(x.shape[0] // SUBLANES, SUBLANES, x.shape[-1]), axis=0)


def _score_chunks(s_ref, m_ref, key_chunk, qt, chunks, bias=None, floor=None):
    mx = None
    for k0, size in chunks:
        c = _dot(key_chunk(k0, size), qt)
        b = None if bias is None else bias(k0, size)
        if b is not None:
            c = c + b
        s_ref[k0:k0 + size, :] = c
        part = _chunk_fold(c, jnp.max)
        mx = part if mx is None else jnp.maximum(mx, part)
        yield
    m = jnp.max(mx, axis=0, keepdims=True)
    m_ref[...] = m if floor is None else jnp.maximum(m, floor)


def _exp_pv_chunks(s_ref, m_ref, value_t, chunk, out):
    m = m_ref[...]
    l8 = None
    o = None
    for k0 in range(0, s_ref.shape[0], chunk):
        e = jnp.exp2(s_ref[k0:k0 + chunk, :] - m)
        part = _chunk_fold(e, jnp.sum)
        l8 = part if l8 is None else l8 + part
        d = _dot(value_t(k0), e.astype(BF16))
        o = d if o is None else o + d
        yield
    out.append((o, jnp.sum(l8, axis=0, keepdims=True)))


KEY_CHUNK = 256


def _ctx_then_latent(c_ref, l_ref, n_ctx, token_major):
    def chunk(k0, size=KEY_CHUNK):
        ref, k = (c_ref, k0) if k0 < n_ctx else (l_ref, k0 - n_ctx)
        return ref[k:k + size, :] if token_major else ref[:, k:k + size]
    return chunk


SCORE_CHUNK = KEY_CHUNK


def _score_key_chunks(n_ctx, n_keys):
    assert n_ctx <= SCORE_CHUNK
    return [(0, n_ctx)] + [(k0, min(SCORE_CHUNK, n_keys - k0)) for k0 in range(n_ctx, n_keys, SCORE_CHUNK)]


def _mla_attn_kernel(q_ref, kc_ref, vct_ref, kl_ref, vlt_ref, o_ref, s0_ref, m0_ref, s1_ref, m1_ref):
    n_tiles, n_heads = o_ref.shape[:2]
    n_ctx = kc_ref.shape[1]
    n_keys = s0_ref.shape[0]
    score_chunks = _score_key_chunks(n_ctx, n_keys)

    def split(i):
        return lax.div(i, n_tiles), lax.rem(i, n_tiles)

    def scores(i, slot):
        s_ref, m_ref = slot
        h, t = split(i)
        keys = _ctx_then_latent(kc_ref.at[h], kl_ref.at[h], n_ctx, True)
        yield from _score_chunks(s_ref, m_ref, keys, q_ref[h, t], score_chunks)

    def softmax_pv(i, slot):
        s_ref, m_ref = slot
        h, t = split(i)
        values = _ctx_then_latent(vct_ref.at[h], vlt_ref.at[h], n_ctx, False)
        res = []
        yield from _exp_pv_chunks(s_ref, m_ref, values, KEY_CHUNK, res)
        o, l = res[0]
        o_ref[t, h] = (o * (1.0 / l)).astype(BF16)

    _two_stage_loop(n_tiles * n_heads, scores, softmax_pv, (s0_ref, m0_ref), (s1_ref, m1_ref))


def _mla_attn(q, kc, vct, kl, vlt):
    bsz, nh, n_tiles = q.shape[:3]
    hs = ATT_HEADS_PER_STEP
    n_keys = kc.shape[2] + kl.shape[2]
    grp = lambda shape: pl.BlockSpec((None, hs) + tuple(shape), lambda b, g: (b, g) + (0,) * len(shape))
    tq = q.shape[-1]
    slot = [pltpu.VMEM((n_keys, tq), F32), pltpu.VMEM((1, tq), F32)]
    return pl.pallas_call(
        _mla_attn_kernel,
        grid=(bsz, nh // hs),
        in_specs=[grp(q.shape[2:]), grp(kc.shape[2:]), grp(vct.shape[2:]), grp(kl.shape[2:]), grp(vlt.shape[2:])],
        out_specs=pl.BlockSpec((None, n_tiles, hs, MLA_V, tq), lambda b, g: (b, 0, g, 0, 0)),
        out_shape=jax.ShapeDtypeStruct((bsz, n_tiles, nh, MLA_V, tq), BF16),
        scratch_shapes=slot + slot,
        compiler_params=_params(2),
        name="mla_attn",
    )(q, kc, vct, kl, vlt)


SWA_BAND = 3 * SWA_BLK
SWA_NQ = SWA_GROUP * SWA_BLK


def _swa_query_block(heads):
    qt = jnp.concatenate(heads, axis=1)
    return jnp.concatenate([qt, jnp.zeros((LANES - qt.shape[0], qt.shape[1]), qt.dtype)], axis=0)


def _sink_row(sink_ref, j):
    return jnp.concatenate(
        [jnp.full((1, SWA_BLK), sink_ref[j * SWA_GROUP + g] * LOG2E, F32) for g in range(SWA_GROUP)], axis=1)


def _swa_attn_kernel(sink_ref, q_ref, kc_ref, vct_ref, kl_ref, vlt_ref, o_ref,
                     bias_ref, s0_ref, m0_ref, s1_ref, m1_ref):
    n_blk = o_ref.shape[0]
    n_ctx = kc_ref.shape[1]

    @pl.when(pl.program_id(0) == 0)
    def _():
        r = lax.broadcasted_iota(jnp.int32, (SWA_BAND, SWA_NQ), 0)
        c = lax.broadcasted_iota(jnp.int32, (SWA_BAND, SWA_NQ), 1) & (SWA_BLK - 1)
        for case in range(3):
            dist = r - c - case * SWA_BLK
            bias_ref[case] = jnp.where(jnp.abs(dist) <= SWA_WINDOW, 0.0, MASKED)

    def split(i):
        return lax.div(i, n_blk), lax.rem(i, n_blk)

    def band_start(n):
        return jnp.clip(n - 1, 0, n_blk - 3)

    def scores(i, slot):
        s_ref, m_ref = slot
        j, n = split(i)
        qt = _swa_query_block([q_ref[j * SWA_GROUP + g, n] for g in range(SWA_GROUP)])
        band0 = band_start(n) * SWA_BLK
        case = (n > 0).astype(jnp.int32) + (n == n_blk - 1).astype(jnp.int32)

        def keys(k0, size):
            if k0 < n_ctx:
                return kc_ref[j, k0:k0 + size, :]
            return kl_ref[j, pl.ds(pl.multiple_of(band0 + (k0 - n_ctx), SWA_BLK), size), :]

        def bias(k0, size):
            return None if k0 < n_ctx else bias_ref[case, k0 - n_ctx:k0 - n_ctx + size, :]

        chunks = [(0, n_ctx), (n_ctx, SWA_BAND)]
        yield from _score_chunks(s_ref, m_ref, keys, qt, chunks, bias, _sink_row(sink_ref, j))

    def softmax_pv(i, slot):
        s_ref, m_ref = slot
        j, n = split(i)
        b0 = band_start(n)

        def value_t(k0):
            if k0 < n_ctx:
                return vct_ref[j, :, k0:k0 + SWA_BLK]
            return vlt_ref[j, b0 + (k0 - n_ctx) // SWA_BLK]

        res = []
        yield from _exp_pv_chunks(s_ref, m_ref, value_t, SWA_BLK, res)
        o, l = res[0]
        o = o * (1.0 / (l + jnp.exp2(_sink_row(sink_ref, j) - m_ref[...])))
        for g in range(SWA_GROUP):
            o_ref[n, j * SWA_GROUP + g] = o[:, g * SWA_BLK:(g + 1) * SWA_BLK].astype(BF16)

    _two_stage_loop(SWA_KV_HEADS * n_blk, scores, softmax_pv, (s0_ref, m0_ref), (s1_ref, m1_ref))


def _swa_attn(sink, q, kc, vct, kl, vlt):
    bsz, nh, n_blk = q.shape[:3]
    n_keys = kc.shape[2] + SWA_BAND
    full = lambda a: pl.BlockSpec((None,) + a.shape[1:], lambda b: (b,) + (0,) * (a.ndim - 1))
    slot = [pltpu.VMEM((n_keys, SWA_NQ), F32), pltpu.VMEM((1, SWA_NQ), F32)]
    return pl.pallas_call(
        _swa_attn_kernel,
        grid=(bsz,),
        in_specs=[pl.BlockSpec(memory_space=pltpu.SMEM), full(q), full(kc), full(vct), full(kl), full(vlt)],
        out_specs=pl.BlockSpec((None, n_blk, nh, SWA_HEAD_DIM, SWA_BLK), lambda b: (b, 0, 0, 0, 0)),
        out_shape=jax.ShapeDtypeStruct((bsz, n_blk, nh, SWA_HEAD_DIM, SWA_BLK), BF16),
        scratch_shapes=[pltpu.VMEM((3, SWA_BAND, SWA_NQ), F32)] + slot + slot,
        compiler_params=_params(1),
        name="swa_attn",
    )(sink, q, kc, vct, kl, vlt)


def _ctx_attn_kernel(sink_ref, qm_ref, kmc_ref, vmct_ref, qs_ref, ksc_ref, vsct_ref, om_ref, os_ref):
    for h in range(om_ref.shape[0]):
        sc = _dot(kmc_ref[h], qm_ref[h])
        pc = jnp.exp2(sc - jnp.max(sc, axis=0, keepdims=True))
        l = jnp.sum(pc, axis=0, keepdims=True)
        o = _dot(vmct_ref[h], pc.astype(BF16))
        om_ref[h] = (o * (1.0 / l)).astype(BF16)
    for j in range(SWA_KV_HEADS):
        sink = _sink_row(sink_ref, j)
        for n in range(os_ref.shape[0]):
            qt = _swa_query_block([qs_ref[j * SWA_GROUP + g, n] for g in range(SWA_GROUP)])
            s = _dot(ksc_ref[j], qt)
            m = jnp.maximum(jnp.max(s, axis=0, keepdims=True), sink)
            p = jnp.exp2(s - m)
            l = jnp.sum(p, axis=0, keepdims=True) + jnp.exp2(sink - m)
            o = _dot(vsct_ref[j], p.astype(BF16)) * (1.0 / l)
            for g in range(SWA_GROUP):
                os_ref[n, j * SWA_GROUP + g] = o[:, g * SWA_BLK:(g + 1) * SWA_BLK].astype(BF16)


def _ctx_attn(sink, qm, kmc, vmct, qs, ksc, vsct):
    bsz, nh, n_tiles = qm.shape[:3]
    assert n_tiles == 1
    tq = qm.shape[-1]
    n_blk = qs.shape[2]
    per_b = lambda a: pl.BlockSpec((None,) + a.shape[1:], lambda b: (b,) + (0,) * (a.ndim - 1))
    return pl.pallas_call(
        _ctx_attn_kernel,
        grid=(bsz,),
        in_specs=[pl.BlockSpec(memory_space=pltpu.SMEM),
                  pl.BlockSpec((None, nh, None) + qm.shape[3:], lambda b: (b, 0, 0, 0, 0)),
                  per_b(kmc), per_b(vmct), per_b(qs), per_b(ksc), per_b(vsct)],
        out_specs=[pl.BlockSpec((None, None, nh, MLA_V, tq), lambda b: (b, 0, 0, 0, 0)),
                   pl.BlockSpec((None, n_blk, SWA_HEADS, SWA_HEAD_DIM, SWA_BLK), lambda b: (b, 0, 0, 0, 0))],
        out_shape=[jax.ShapeDtypeStruct((bsz, 1, nh, MLA_V, tq), BF16),
                   jax.ShapeDtypeStruct((bsz, n_blk, SWA_HEADS, SWA_HEAD_DIM, SWA_BLK), BF16)],
        compiler_params=_params(1),
        name="ctx_attn",
    )(sink, qm, kmc, vmct, qs, ksc, vsct)


DIFF_Q = HALF // 2


def _diff_norm_t(t, g):
    sq = t * t
    grp = [jnp.sum(sq[i * DIFF_Q:(i + 1) * DIFF_Q], axis=0, keepdims=True) for i in range(4)]
    inv = 1.0 / DIFF_HEAD_DIM
    r0 = lax.rsqrt((grp[0] + grp[2]) * inv + EPS)
    r1 = lax.rsqrt((grp[1] + grp[3]) * inv + EPS)
    parts = [t[i * DIFF_Q:(i + 1) * DIFF_Q] * (r0 if i % 2 == 0 else r1) for i in range(4)]
    return jnp.concatenate(parts, axis=0) * g


def _l1_proj_kernel(*refs, latent):
    if latent:
        h_ref, mod_ref, g_ref, wt_ref, qg_ref, kg_ref, cdt_ref, sdt_ref, q_ref, k_ref, vt_ref = refs
        cdt, sdt = cdt_ref[...], sdt_ref[...]
    else:
        h_ref, mod_ref, g_ref, wt_ref, kg_ref, k_ref, vt_ref = refs
    hn = _adaln(h_ref[...], g_ref[...], mod_ref[3:4, :], mod_ref[4:5, :]).astype(BF16)
    tm = hn.shape[0]
    width = DIFF_HEADS * LANES
    ft = _grouped_nt(wt_ref, hn)
    bcast = lambda v: jnp.broadcast_to(v, (LANES, tm))
    k0 = 0
    if latent:
        qg = bcast(qg_ref[...] * (DIFF_HEAD_DIM ** -0.5 * LOG2E))
        for h in range(DIFF_HEADS):
            t = _rope_t(_diff_norm_t(ft(h * LANES, (h + 1) * LANES), qg), cdt, sdt, DIFF_ROPE_ROWS)
            tq = q_ref.shape[-1]
            for j in range(tm // tq):
                q_ref[h, j] = t[:, j * tq:(j + 1) * tq].astype(BF16)
        k0 = width

    kg = bcast(kg_ref[...])
    for h in range(DIFF_HEADS):
        t = _diff_norm_t(ft(k0 + h * LANES, k0 + (h + 1) * LANES), kg)
        if latent:
            t = _rope_t(t, cdt, sdt, DIFF_ROPE_ROWS)
        k_ref[h] = t.T.astype(BF16)
    v0 = k0 + width
    for h in range(DIFF_HEADS):
        vt_ref[h] = ft(v0 + h * LANES, v0 + (h + 1) * LANES).astype(BF16)


def _l1_proj(h, mod, per_batch, g, w, tables):
    bsz, n, _ = h.shape
    tm = min(TOK_TILE, n)
    latent = tables is not None
    k_spec = pl.BlockSpec((None, DIFF_HEADS, tm, LANES), lambda b, t: (b, 0, t, 0))
    vt_spec = pl.BlockSpec((None, DIFF_HEADS, LANES, tm), lambda b, t: (b, 0, 0, t))
    k_shape = jax.ShapeDtypeStruct((bsz, DIFF_HEADS, n, LANES), BF16)
    vt_shape = jax.ShapeDtypeStruct((bsz, DIFF_HEADS, LANES, n), BF16)
    if latent:
        weights = [g, w["wqkvt"], w["q_g"], w["k_g"]]
        in_specs = ([_tok_spec(tm), _mod_spec(per_batch)] + [_const_spec(a.shape) for a in weights]
                    + [_table_spec(i, tm) for _, i in tables])
        args = [h, mod] + weights + [a for a, _ in tables]
        tq = min(DIFF_TQ, n)
        out_specs = [pl.BlockSpec((None, DIFF_HEADS, tm // tq, LANES, tq), lambda b, t: (b, 0, t, 0, 0)),
                     k_spec, vt_spec]
        out_shape = [jax.ShapeDtypeStruct((bsz, DIFF_HEADS, n // tq, LANES, tq), BF16), k_shape, vt_shape]
    else:
        weights = [g, w["wkvt"], w["k_g"]]
        in_specs = [_tok_spec(tm), _mod_spec(per_batch)] + [_const_spec(a.shape) for a in weights]
        args = [h, mod] + weights
        out_specs = [k_spec, vt_spec]
        out_shape = [k_shape, vt_shape]
    return pl.pallas_call(
        functools.partial(_l1_proj_kernel, latent=latent),
        grid=(bsz, n // tm),
        in_specs=in_specs, out_specs=out_specs, out_shape=out_shape,
        compiler_params=_params(2),
        name="l1_proj",
    )(*args)


def _diff_attn_kernel(lam_ref, subg_ref, q_ref, kc_ref, vct_ref, kl_ref, vlt_ref, o_ref,
                      s0_ref, m0_ref, s1_ref, m1_ref):
    n_tiles, n_heads = o_ref.shape[:2]
    n_ctx = kc_ref.shape[1]
    n_keys = s0_ref.shape[1]
    score_chunks = _score_key_chunks(n_ctx, n_keys)
    feature = lax.broadcasted_iota(jnp.int32, q_ref.shape[2:], 0)
    sub0 = (feature & (HALF - 1)) < DIFF_Q
    lv = lam_ref[...]
    lam = (jnp.exp(jnp.sum(lv[0:1] * lv[1:2], axis=-1, keepdims=True))
           - jnp.exp(jnp.sum(lv[2:3] * lv[3:4], axis=-1, keepdims=True)) + LAMBDA_INIT_L1)
    subg = subg_ref[...] * (1.0 - LAMBDA_INIT_L1)

    def split(i):
        return lax.div(i, n_tiles), lax.rem(i, n_tiles)

    def scores(i, slot):
        s_ref, m_ref = slot
        h, t = split(i)
        qt = q_ref[h, t]
        zero = jnp.zeros_like(qt)
        keys = _ctx_then_latent(kc_ref.at[h], kl_ref.at[h], n_ctx, True)
        for s, qs in enumerate((jnp.where(sub0, qt, zero), jnp.where(sub0, zero, qt))):
            yield from _score_chunks(s_ref.at[s], m_ref.at[s], keys, qs, score_chunks)

    def softmax_pv(i, slot):
        s_ref, m_ref = slot
        h, t = split(i)
        values = _ctx_then_latent(vct_ref.at[h], vlt_ref.at[h], n_ctx, False)
        res = []
        for s in range(2):
            yield from _exp_pv_chunks(s_ref.at[s], m_ref.at[s], values, KEY_CHUNK, res)
        (o0, l0), (o1, l1) = res
        o = o0 * (1.0 / l0) - o1 * (lam * (1.0 / l1))
        ms = jnp.mean(o * o, axis=0, keepdims=True)
        o_ref[t, h] = (o * lax.rsqrt(ms + EPS) * subg).astype(BF16)

    _two_stage_loop(n_tiles * n_heads, scores, softmax_pv, (s0_ref, m0_ref), (s1_ref, m1_ref))


def _diff_attn(lamv, subg, q, kc, vct, kl, vlt):
    bsz, nh, n_tiles = q.shape[:3]
    hs = ATT_HEADS_PER_STEP
    n_keys = kc.shape[2] + kl.shape[2]
    grp = lambda shape: pl.BlockSpec((None, hs) + tuple(shape), lambda b, g: (b, g) + (0,) * len(shape))
    tq = q.shape[-1]
    slot = [pltpu.VMEM((2, n_keys, tq), F32), pltpu.VMEM((2, 1, tq), F32)]
    return pl.pallas_call(
        _diff_attn_kernel,
        grid=(bsz, nh // hs),
        in_specs=[_const_spec(lamv.shape), _const_spec(subg.shape),
                  grp(q.shape[2:]), grp(kc.shape[2:]), grp(vct.shape[2:]), grp(kl.shape[2:]), grp(vlt.shape[2:])],
        out_specs=pl.BlockSpec((None, n_tiles, hs, LANES, tq), lambda b, g: (b, 0, g, 0, 0)),
        out_shape=jax.ShapeDtypeStruct((bsz, n_tiles, nh, LANES, tq), BF16),
        scratch_shapes=slot + slot,
        compiler_params=_params(2),
        name="diff_attn",
    )(lamv, subg, q, kc, vct, kl, vlt)


def _prep_l0(w_in, qa_g, wqb, kva_g, wkvb, q_g, k_g, sq_g, sk_g, w_out):
    mla = _mla_lane_map()
    swa = _swa_lane_map()
    kr_cols = np.where(mla >= MLA_NOPE, 384 + mla - MLA_NOPE, -1)
    wst_cols = np.concatenate([
        416 + np.arange(SWA_HEADS * SWA_HEAD_DIM),
        np.arange(1056, 1184),
        kr_cols,
        _per_head(swa, SWA_KV_HEADS, SWA_HEAD_DIM, base=928),
    ])
    assert wst_cols.shape[0] == L0_ST_ROWS
    nope_map = np.where((mla >= 0) & (mla < MLA_NOPE), mla, -1)
    v_cols = np.concatenate([h * (MLA_NOPE + MLA_V) + MLA_NOPE + np.arange(MLA_V) for h in range(MLA_HEADS)])
    wkvt_cols = np.concatenate([_per_head(nope_map, MLA_HEADS, MLA_NOPE + MLA_V), v_cols])
    row = lambda v: v.reshape(1, -1).astype(F32)
    col = lambda v: v.reshape(-1, 1).astype(F32)
    return {
        "w0": w_in[:, 0:MLA_Q_RANK + MLA_KV_RANK].astype(BF16),
        "wst": _cols_as_rows(w_in, wst_cols),
        "qa_g": row(qa_g),
        "wqbt": _cols_as_rows(wqb, _per_head(mla, MLA_HEADS, MLA_NOPE + MLA_ROPE)),
        "kva_g": row(kva_g),
        "wkvt": _cols_as_rows(wkvb, wkvt_cols),
        "mq_g": col(_take_cols(q_g, mla)),
        "mk_g": col(_take_cols(k_g, mla)),
        "sq_g": col(sq_g),
        "sk_g": col(_take_cols(sk_g, swa)),
        "wa": w_out[:MLA_HEADS * MLA_V].astype(BF16),
        "wb": w_out[MLA_HEADS * MLA_V:].astype(BF16),
    }


def _prep_l1(w_in, q_g, k_g, w_out):
    dm = _diff_lane_map()
    width = DIFF_HEADS * LANES
    cols = np.concatenate([_per_head(dm, DIFF_HEADS, LANES), _per_head(dm, DIFF_HEADS, LANES, base=width),
                           np.arange(2 * width, 3 * width)])
    wqkvt = _cols_as_rows(w_in, cols)
    return {
        "wqkvt": wqkvt,
        "wkvt": wqkvt[width:],
        "q_g": _take_cols(q_g, dm % DIFF_HEAD_DIM).reshape(-1, 1).astype(F32),
        "k_g": _take_cols(k_g, dm % DIFF_HEAD_DIM).reshape(-1, 1).astype(F32),
        "w_out": w_out.astype(BF16),
    }


def kernel(x, c, ctx, c_ctx, l0_ada_w, l0_ada_b, l0_norm_g, l0_ffn_wg, l0_ffn_wu, l0_ffn_wd, l0_w_in, l0_mla_qa_g, l0_mla_wqb, l0_mla_kva_g, l0_mla_wkvb, l0_mla_q_g, l0_mla_k_g, l0_swa_q_g, l0_swa_k_g, l0_swa_sink, l0_w_out, l1_ada_w, l1_ada_b, l1_norm_g, l1_ffn_wg, l1_ffn_wu, l1_ffn_wd, l1_w_in, l1_q_g, l1_k_g, l1_lambda_q1, l1_lambda_k1, l1_lambda_q2, l1_lambda_k2, l1_subln_g, l1_w_out):
    bsz, seq, _ = x.shape

    pad = (-(bsz + 1)) % SUBLANES
    cc = jnp.concatenate([c, c_ctx[None, :], jnp.zeros((pad, D_MODEL), F32)], axis=0)

    def mods(ada_w, ada_b):
        m = _ada(cc, ada_w, ada_b).reshape(cc.shape[0], N_MOD, D_MODEL)
        return m[:bsz], m[bsz:bsz + 1]

    def ffn_weights(wg, wu, wd, i):
        return wg[i].astype(BF16), wu[i].astype(BF16), wd[i].astype(BF16)

    mla_tab, swa_tab, diff_tab = _rope_tables(seq, [
        (MLA_ROPE, [(0, 16)], [(64, 16)]),
        (SWA_HEAD_DIM, [(0, 32)], [(32, 32)]),
        (DIFF_HEAD_DIM, [(0, 32), (32, 32)], [(64, 32), (96, 32)]),
    ])

    mod, mod_c = mods(l0_ada_w, l0_ada_b)
    g = [l0_norm_g[i:i + 1] for i in range(3)]
    w = _prep_l0(l0_w_in, l0_mla_qa_g, l0_mla_wqb, l0_mla_kva_g, l0_mla_wkvb, l0_mla_q_g, l0_mla_k_g,
                 l0_swa_q_g, l0_swa_k_g, l0_w_out)
    f0 = ffn_weights(l0_ffn_wg, l0_ffn_wu, l0_ffn_wd, 0)
    f1 = ffn_weights(l0_ffn_wg, l0_ffn_wu, l0_ffn_wd, 1)

    h = _ffn(x, mod, True, g[0], *f0, k=0)
    hc = _ffn(ctx, mod_c, False, g[0], *f0, k=0)

    qm, km, vmt, qs, ks, vst = _l0_proj(h, mod, True, g[1], w, mla_tab + swa_tab)
    qm_c, km_c, vmt_c, qs_c, ks_c, vst_c = _l0_proj(hc, mod_c, False, g[1], w, None)
    vsc_t = vst_c.transpose(0, 1, 3, 2, 4).reshape(bsz, SWA_KV_HEADS, SWA_HEAD_DIM, -1)

    at = _mla_attn(qm, km_c, vmt_c, km, vmt)
    bt = _swa_attn(l0_swa_sink, qs, ks_c, vsc_t, ks, vst)
    at_c, bt_c = _ctx_attn(l0_swa_sink, qm_c, km_c, vmt_c, qs_c, ks_c, vsc_t)

    h = _out_ffn(h, mod, True, g[2], [(at, w["wa"]), (bt, w["wb"])], *f1)
    hc = _out_ffn(hc, mod_c, False, g[2], [(at_c, w["wa"]), (bt_c, w["wb"])], *f1)

    mod, mod_c = mods(l1_ada_w, l1_ada_b)
    g = [l1_norm_g[i:i + 1] for i in range(3)]
    w = _prep_l1(l1_w_in, l1_q_g, l1_k_g, l1_w_out)
    f0 = ffn_weights(l1_ffn_wg, l1_ffn_wu, l1_ffn_wd, 0)
    f1 = ffn_weights(l1_ffn_wg, l1_ffn_wu, l1_ffn_wd, 1)

    h = _ffn(h, mod, True, g[0], *f0, k=0)
    hc = _ffn(hc, mod_c, False, g[0], *f0, k=0)

    qd, kd, vdt = _l1_proj(h, mod, True, g[1], w, diff_tab)
    kd_c, vdt_c = _l1_proj(hc, mod_c, False, g[1], w, None)
    lamv = jnp.stack([l1_lambda_q1, l1_lambda_k1, l1_lambda_q2, l1_lambda_k2]).astype(F32)
    ot = _diff_attn(lamv, l1_subln_g.reshape(-1, 1).astype(F32), qd, kd_c, vdt_c, kd, vdt)
    return _out_ffn(h, mod, True, g[2], [(ot, w["w_out"])], *f1)
```
